```python
import math
import jax
import jax.numpy as jnp
from jax import lax
import numpy as np

D_MODEL = 2048
BATCH = 4
SEQ = 4096
DEPTH = 4

CTX_LEN = 256
GRID_W = 64
HEAD_DIM = 128
N_DIFF_HEADS = D_MODEL // (2 * HEAD_DIM)
N_SUB_HEADS = 2 * N_DIFF_HEADS
V_HEAD_DIM = 2 * HEAD_DIM
ATT_W = N_DIFF_HEADS * V_HEAD_DIM
HY_W = D_MODEL
OFF_K = ATT_W
OFF_V = 2 * ATT_W
OFF_HY = 3 * ATT_W
OFF_G = 3 * ATT_W + 3 * HY_W
IN_COLS = OFF_G + 2 * D_MODEL
ROPE_BASE = 10000.0
ROT_FREQS = HEAD_DIM // 4
Q_BLOCK = 128
HY_EMB = 33
HY_BANDS = (HY_EMB - 1) // 2
HY_FH = 64
HY_N_FILT = 2
HY_TARGET = 1e-2
HY_FAST_PCT = 0.3
HY_SLOW_PCT = 1.5
HY_FILTER_OUT_STD = 0.005
D_FF = 5632
N_EXPERTS = 8
TOP_K = 2
N_DENSE = (DEPTH + 1) // 2
N_MOE = DEPTH // 2
EPS = 1e-6

kernel_name = "hybrid_diffattn_hyena_moe_dit"


def rmsnorm(x, g):
    xf = x.astype(jnp.float32)
    y = xf * lax.rsqrt(jnp.mean(xf * xf, axis=-1, keepdims=True) + EPS)
    return (y * g.astype(jnp.float32)).astype(x.dtype)


def modulate(h, shift, scale):
    return h * (1 + scale) + shift


def adaln_params(cvec, w, b):
    return jnp.split(jax.nn.silu(cvec) @ w + b, 6, axis=-1)


def axial_rope_tables(rows):
    row = jnp.repeat(jnp.arange(rows), GRID_W).astype(jnp.float32)
    col = jnp.tile(jnp.arange(GRID_W), rows).astype(jnp.float32)
    inv = ROPE_BASE ** (-jnp.arange(ROT_FREQS, dtype=jnp.float32) / ROT_FREQS)
    ang = jnp.stack([row[:, None] * inv, col[:, None] * inv], axis=1)
    return jnp.cos(ang), jnp.sin(ang)


def rope2d(x, cos, sin):
    xs = x.reshape(x.shape[:-1] + (2, 2, ROT_FREQS)).astype(jnp.float32)
    a, b = xs[..., 0, :], xs[..., 1, :]
    c = cos[None, :, None]
    s = sin[None, :, None]
    out = jnp.stack([a * c - b * s, a * s + b * c], axis=-2)
    return out.reshape(x.shape).astype(x.dtype)


def diff_attend(q, k, v, lam):
    s = jnp.einsum('bqhd,bkhd->bhqk', q, k, preferred_element_type=jnp.float32) / math.sqrt(HEAD_DIM)
    p = jax.nn.softmax(s, axis=-1)
    bsz, _, nq, nk = p.shape
    p = p.reshape(bsz, N_DIFF_HEADS, 2, nq, nk)
    a = p[:, :, 0] - lam * p[:, :, 1]
    return jnp.einsum('bhqk,bkhd->bqhd', a.astype(v.dtype), v)


def latent_diff_attention(q, k_all, v_all, lam):
    bsz, s = q.shape[:2]
    qb = q.reshape(bsz, s // Q_BLOCK, Q_BLOCK, N_SUB_HEADS, HEAD_DIM).swapaxes(0, 1)
    out = lax.map(lambda qi: diff_attend(qi, k_all, v_all, lam), qb)
    return out.swapaxes(0, 1).reshape(bsz, s, N_DIFF_HEADS, V_HEAD_DIM)


def diff_head_out(o, g, lam_init):
    o = rmsnorm(o, g) * (1.0 - lam_init)
    return o.reshape(o.shape[:2] + (ATT_W,))


def short_conv3(u, w, b):
    up = jnp.pad(u, ((0, 0), (1, 1), (0, 0)))
    return up[:, :-2] * w[0] + up[:, 1:-1] * w[1] + up[:, 2:] * w[2] + b


def hyena_filters(L, w1, b1, w2, b2, w3, b3, w4, freq):
    t = jnp.linspace(0.0, 1.0, L, dtype=jnp.float32)[:, None]
    w = 2.0 * math.pi * jnp.arange(L, dtype=jnp.float32)[:, None] / L
    f = jnp.linspace(1e-4, HY_BANDS - 1, HY_BANDS, dtype=jnp.float32)
    z = jnp.concatenate([t, jnp.cos(w * f), -jnp.sin(w * f)], axis=-1)
    hid = jnp.sin(freq * (z @ w1 + b1))
    hid = jnp.sin(freq * (hid @ w2 + b2))
    hid = jnp.sin(freq * (hid @ w3 + b3))
    h = (hid @ w4).reshape(L, HY_N_FILT, 2, HY_W)
    deltas = jnp.abs(jnp.linspace(math.log(HY_TARGET) / HY_SLOW_PCT,
                                  math.log(HY_TARGET) / HY_FAST_PCT, HY_W, dtype=jnp.float32))
    decay = jnp.exp(-t * deltas)
    return h * decay[:, None, None, :]


def bidir_long_conv(u, h_fwd, h_bwd):
    L = u.shape[1]
    k = jnp.concatenate([h_fwd, jnp.zeros_like(h_fwd[:1]), h_bwd[:0:-1]], axis=0)
    kf = jnp.fft.rfft(k.astype(jnp.float32), n=2 * L, axis=0)
    uf = jnp.fft.rfft(u.astype(jnp.float32), n=2 * L, axis=1)
    y = jnp.fft.irfft(uf * kf[None], n=2 * L, axis=1)[:, :L]
    return y.astype(u.dtype)


def hyena_mix(u, conv_w, conv_b, w1, b1, w2, b2, w3, b3, w4, freq, bias_d):
    u = short_conv3(u, conv_w, conv_b)
    v, x1, x2 = jnp.split(u, 3, axis=-1)
    h = hyena_filters(u.shape[1], w1, b1, w2, b2, w3, b3, w4, freq)
    z = x1 * (bidir_long_conv(v, h[:, 0, 0], h[:, 0, 1]) + bias_d[0] * v)
    return x2 * (bidir_long_conv(z, h[:, 1, 0], h[:, 1, 1]) + bias_d[1] * z)


def merge_branches(ya, yb, gate_logits, b_gate, w_br_a, w_br_b, w_out):
    g = jax.nn.sigmoid(gate_logits + b_gate)
    ga, gb = jnp.split(g, 2, axis=-1)
    return (ga * (ya @ w_br_a) + gb * (yb @ w_br_b)) @ w_out


def swiglu(t, w1, w3, w2):
    return (jax.nn.silu(t @ w1) * (t @ w3)) @ w2


def moe_swiglu(t, router, w1, w3, w2):
    logits = (t @ router).astype(jnp.float32)
    top_v, top_i = lax.top_k(logits, TOP_K)
    wts = jax.nn.softmax(top_v, axis=-1)
    combine = jnp.einsum('nk,nke->ne', wts, jax.nn.one_hot(top_i, N_EXPERTS, dtype=jnp.float32)).astype(t.dtype)
    y = jnp.zeros_like(t)
    for e in range(N_EXPERTS):
        y = y + combine[:, e:e + 1] * swiglu(t, w1[e], w3[e], w2[e])
    return y


def channel_mixer(l, t, ffn_w1, ffn_w3, ffn_w2, router, moe_w1, moe_w3, moe_w2):
    i = l // 2
    if l % 2 == 0:
        return swiglu(t, ffn_w1[i], ffn_w3[i], ffn_w2[i])
    return moe_swiglu(t, router[i], moe_w1[i], moe_w3[i], moe_w2[i])


def setup_inputs(seed: int = 0) -> dict:
    key = jax.random.key(seed)
    ks = iter(jax.random.split(key, 48))
    D = D_MODEL

    def nrm(shape, scale):
        return jax.random.normal(next(ks), shape, jnp.float32) * scale

    def gain(shape):
        return 1.0 + nrm(shape, 0.02)

    return {
        'x': nrm((BATCH, SEQ, D), 1.0),
        'c': nrm((BATCH, D), 1.0),
        'ctx': nrm((BATCH, CTX_LEN, D), 1.0),
        'c_ctx': nrm((D,), 1.0),
        'w_ada': nrm((DEPTH, D, 6 * D), D ** -0.5),
        'b_ada': nrm((DEPTH, 6 * D), 0.02),
        'norm1': gain((DEPTH, D)),
        'norm2': gain((DEPTH, D)),
        'w_in': nrm((DEPTH, D, IN_COLS), D ** -0.5),
        'b_gate': nrm((DEPTH, 2 * D), 0.02),
        'q_norm': gain((DEPTH, HEAD_DIM)),
        'k_norm': gain((DEPTH, HEAD_DIM)),
        'lam_q1': nrm((DEPTH, HEAD_DIM), 0.1),
        'lam_k1': nrm((DEPTH, HEAD_DIM), 0.1),
        'lam_q2': nrm((DEPTH, HEAD_DIM), 0.1),
        'lam_k2': nrm((DEPTH, HEAD_DIM), 0.1),
        'sub_norm': gain((DEPTH, V_HEAD_DIM)),
        'hy_conv_w': nrm((DEPTH, 3, 3 * HY_W), 3 ** -0.5),
        'hy_conv_b': nrm((DEPTH, 3 * HY_W), 0.02),
        'hy_w1': nrm((DEPTH, HY_EMB, HY_FH), HY_EMB ** -0.5),
        'hy_b1': nrm((DEPTH, HY_FH), 0.02),
        'hy_w2': nrm((DEPTH, HY_FH, HY_FH), HY_FH ** -0.5),
        'hy_b2': nrm((DEPTH, HY_FH), 0.02),
        'hy_w3': nrm((DEPTH, HY_FH, HY_FH), HY_FH ** -0.5),
        'hy_b3': nrm((DEPTH, HY_FH), 0.02),
        'hy_w4': nrm((DEPTH, HY_FH, HY_N_FILT * 2 * HY_W), HY_FILTER_OUT_STD),
        'hy_freq': gain((DEPTH, HY_FH)),
        'hy_bias': nrm((DEPTH, HY_N_FILT, HY_W), 0.5),
        'w_br_a': nrm((DEPTH, ATT_W, D), ATT_W ** -0.5),
        'w_br_b': nrm((DEPTH, HY_W, D), HY_W ** -0.5),
        'w_out': nrm((DEPTH, D, D), D ** -0.5),
        'ffn_w1': nrm((N_DENSE, D, D_FF), D ** -0.5),
        'ffn_w3': nrm((N_DENSE, D, D_FF), D ** -0.5),
        'ffn_w2': nrm((N_DENSE, D_FF, D), D_FF ** -0.5),
        'router': nrm((N_MOE, D, N_EXPERTS), D ** -0.5),
        'moe_w1': nrm((N_MOE, N_EXPERTS, D, D_FF), D ** -0.5),
        'moe_w3': nrm((N_MOE, N_EXPERTS, D, D_FF), D ** -0.5),
        'moe_w2': nrm((N_MOE, N_EXPERTS, D_FF, D), D_FF ** -0.5),
    }


def reference(x, c, ctx, c_ctx, w_ada, b_ada, norm1, norm2, w_in, b_gate, q_norm, k_norm,
              lam_q1, lam_k1, lam_q2, lam_k2, sub_norm, hy_conv_w, hy_conv_b,
              hy_w1, hy_b1, hy_w2, hy_b2, hy_w3, hy_b3, hy_w4, hy_freq, hy_bias,
              w_br_a, w_br_b, w_out, ffn_w1, ffn_w3, ffn_w2, router, moe_w1, moe_w3, moe_w2):
    B, S, D = x.shape
    Lc = ctx.shape[1]
    rows = S // GRID_W
    cos, sin = axial_rope_tables(rows)
    xl, xc = x, ctx
    for l in range(DEPTH):
        last = l == DEPTH - 1
        lam_init = 0.8 - 0.6 * math.exp(-0.3 * l)
        lam = (jnp.exp(jnp.sum(lam_q1[l].astype(jnp.float32) * lam_k1[l].astype(jnp.float32)))
               - jnp.exp(jnp.sum(lam_q2[l].astype(jnp.float32) * lam_k2[l].astype(jnp.float32)))
               + lam_init)
        sh1, sc1, g1, sh2, sc2, g2 = adaln_params(c, w_ada[l], b_ada[l])
        sh1c, sc1c, g1c, sh2c, sc2c, g2c = adaln_params(c_ctx, w_ada[l], b_ada[l])
        hl = modulate(rmsnorm(xl, norm1[l]), sh1[:, None], sc1[:, None])
        hc = modulate(rmsnorm(xc, norm1[l]), sh1c, sc1c)
        w_l = w_in[l]
        pl = hl @ w_l
        base = OFF_K if last else 0
        pc = hc @ (w_l[:, OFF_K:OFF_HY] if last else w_l)
        kc = rmsnorm(pc[..., OFF_K - base:OFF_V - base].reshape(B, Lc, N_SUB_HEADS, HEAD_DIM), k_norm[l])
        vc = pc[..., OFF_V - base:OFF_HY - base].reshape(B, Lc, N_DIFF_HEADS, V_HEAD_DIM)
        q = rope2d(rmsnorm(pl[..., :OFF_K].reshape(B, S, N_SUB_HEADS, HEAD_DIM), q_norm[l]), cos, sin)
        k = rope2d(rmsnorm(pl[..., OFF_K:OFF_V].reshape(B, S, N_SUB_HEADS, HEAD_DIM), k_norm[l]), cos, sin)
        v = pl[..., OFF_V:OFF_HY].reshape(B, S, N_DIFF_HEADS, V_HEAD_DIM)
        ya = latent_diff_attention(q, jnp.concatenate([kc, k], axis=1), jnp.concatenate([vc, v], axis=1), lam)
        hp = (hy_conv_w[l], hy_conv_b[l], hy_w1[l], hy_b1[l], hy_w2[l], hy_b2[l], hy_w3[l], hy_b3[l],
              hy_w4[l], hy_freq[l], hy_bias[l])
        yb = hyena_mix(pl[..., OFF_HY:OFF_G], *hp)
        out = merge_branches(diff_head_out(ya, sub_norm[l], lam_init), yb, pl[..., OFF_G:],
                             b_gate[l], w_br_a[l], w_br_b[l], w_out[l])
        xl = xl + g1[:, None] * out
        h2l = modulate(rmsnorm(xl, norm2[l]), sh2[:, None], sc2[:, None]).reshape(B * S, D)
        if last:
            f = channel_mixer(l, h2l, ffn_w1, ffn_w3, ffn_w2, router, moe_w1, moe_w3, moe_w2)
            xl = xl + g2[:, None] * f.reshape(B, S, D)
        else:
            qc = rmsnorm(pc[..., :OFF_K].reshape(B, Lc, N_SUB_HEADS, HEAD_DIM), q_norm[l])
            yac = diff_attend(qc, kc, vc, lam)
            ybc = hyena_mix(pc[..., OFF_HY:OFF_G], *hp)
            outc = merge_branches(diff_head_out(yac, sub_norm[l], lam_init), ybc, pc[..., OFF_G:],
                                  b_gate[l], w_br_a[l], w_br_b[l], w_out[l])
            xc = xc + g1c * outc
            h2c = modulate(rmsnorm(xc, norm2[l]), sh2c, sc2c).reshape(B * Lc, D)
            f = channel_mixer(l, jnp.concatenate([h2l, h2c], axis=0),
                              ffn_w1, ffn_w3, ffn_w2, router, moe_w1, moe_w3, moe_w2)
            xl = xl + g2[:, None] * f[:B * S].reshape(B, S, D)
            xc = xc + g2c * f[B * S:].reshape(B, Lc, D)
    return xl
```

```python
import functools
import math

import numpy as np
import jax
import jax.numpy as jnp
from jax import lax
from jax.experimental import pallas as pl
from jax.experimental.pallas import tpu as pltpu

D_MODEL = 2048
BATCH = 4
SEQ = 4096
DEPTH = 4
CTX_LEN = 256
GRID_W = 64
HEAD_DIM = 128
N_DIFF_HEADS = D_MODEL // (2 * HEAD_DIM)
V_HEAD_DIM = 2 * HEAD_DIM
ATT_W = N_DIFF_HEADS * V_HEAD_DIM
HY_W = D_MODEL
OFF_K = ATT_W
OFF_V = 2 * ATT_W
OFF_HY = 3 * ATT_W
OFF_G = 3 * ATT_W + 3 * HY_W
IN_COLS = OFF_G + 2 * D_MODEL
ROPE_BASE = 10000.0
ROT_FREQS = HEAD_DIM // 4
HY_EMB = 33
HY_BANDS = (HY_EMB - 1) // 2
HY_FH = 64
HY_N_FILT = 2
HY_TARGET = 1e-2
HY_FAST_PCT = 0.3
HY_SLOW_PCT = 1.5
D_FF = 5632
N_EXPERTS = 8
TOP_K = 2
EPS = 1e-6

R_LAT = BATCH * SEQ
R_CTX = BATCH * CTX_LEN
R_ALL = R_LAT + R_CTX
TM = 1024
N_ROW_TILES = R_ALL // TM
CTX_TILE = R_LAT // TM
CTX_SEG = BATCH
TR = 256
LANES = 128
VMEM_LIMIT = 56 * 1024 * 1024

FFT_N = 2 * SEQ
FFT_N1 = 128
FFT_N2 = 64
HY_TC = 128

F32 = jnp.float32
BF16 = jnp.bfloat16
HIGHEST = lax.Precision.HIGHEST


def _cparams(n_axes):
    return pltpu.CompilerParams(
        dimension_semantics=("arbitrary",) * n_axes, vmem_limit_bytes=VMEM_LIMIT)


def _row_seg(i, tm=TM):
    return jnp.where(i >= R_LAT // tm, CTX_SEG, i // (SEQ // tm))


def _adaln_body(c_ref, w_ref, b_ref, o_ref):
    s = c_ref[...]
    s = s * jax.nn.sigmoid(s)
    o_ref[...] = jnp.dot(s, w_ref[...], preferred_element_type=F32, precision=HIGHEST) + b_ref[...]


def _adaln(cs, w_ada, b_ada):
    tn = 1536
    n = 6 * D_MODEL
    return pl.pallas_call(
        _adaln_body,
        grid=(DEPTH, n // tn),
        in_specs=[
            pl.BlockSpec((8, D_MODEL), lambda l, j: (0, 0)),
            pl.BlockSpec((None, D_MODEL, tn), lambda l, j: (l, 0, j)),
            pl.BlockSpec((None, 1, tn), lambda l, j: (l, 0, j)),
        ],
        out_specs=pl.BlockSpec((None, 8, tn), lambda l, j: (l, 0, j)),
        out_shape=jax.ShapeDtypeStruct((DEPTH, 8, n), F32),
        compiler_params=_cparams(2),
        name="adaln",
    )(cs, w_ada, b_ada.reshape(DEPTH, 1, n))


def _normmod_body(x_ref, g_ref, sh_ref, sc_ref, *rest, with_router):
    i = pl.program_id(0)
    seg = jnp.where(i >= R_LAT // TR, CTX_SEG, i // (SEQ // TR))
    x = x_ref[...]
    ms = jnp.mean(x * x, axis=-1, keepdims=True)
    y = x * lax.rsqrt(ms + EPS) * g_ref[...]
    h = y * (1.0 + sc_ref[pl.ds(seg, 1), :]) + sh_ref[pl.ds(seg, 1), :]
    if with_router:
        r_ref, o_ref, lg_ref = rest
        lg_ref[...] = jnp.dot(h, r_ref[...], preferred_element_type=F32, precision=HIGHEST)
    else:
        (o_ref,) = rest
    o_ref[...] = h.astype(BF16)


def _normmod(x, gain, mods, l, which, router=None):
    in_specs = [
        pl.BlockSpec((TR, D_MODEL), lambda i: (i, 0)),
        pl.BlockSpec((None, 1, D_MODEL), lambda i: (l, 0, 0)),
        pl.BlockSpec((None, 8, D_MODEL), lambda i: (l, 0, which)),
        pl.BlockSpec((None, 8, D_MODEL), lambda i: (l, 0, which + 1)),
    ]
    args = [x, gain.reshape(DEPTH, 1, D_MODEL), mods, mods]
    out_specs = pl.BlockSpec((TR, D_MODEL), lambda i: (i, 0))
    out_shape = jax.ShapeDtypeStruct((R_ALL, D_MODEL), BF16)
    if router is not None:
        in_specs.append(pl.BlockSpec((D_MODEL, LANES), lambda i: (0, 0)))
        args.append(router)
        out_specs = [out_specs, pl.BlockSpec((TR, LANES), lambda i: (i, 0))]
        out_shape = [out_shape, jax.ShapeDtypeStruct((R_ALL, LANES), F32)]
    return pl.pallas_call(
        functools.partial(_normmod_body, with_router=router is not None),
        grid=(R_ALL // TR,),
        in_specs=in_specs,
        out_specs=out_specs,
        out_shape=out_shape,
        compiler_params=_cparams(1),
        name="normmod",
    )(*args)


def _mm_body(*refs, n_a, n_w, n_aux, epilogue, a_of_w):
    a_refs = refs[:n_a]
    w_refs = refs[n_a:n_a + n_w]
    aux_refs = refs[n_a + n_w:n_a + n_w + n_aux]
    o_ref = refs[n_a + n_w + n_aux]
    wbf_refs = refs[n_a + n_w + n_aux + 1:]
    j = pl.program_id(0)
    i = pl.program_id(1)

    @pl.when(i == 0)
    def _():
        for w_ref, wbf_ref in zip(w_refs, wbf_refs):
            wbf_ref[...] = w_ref[...].astype(BF16)

    accs = [jnp.dot(a_refs[a_of_w[k]][...], wbf_refs[k][...], preferred_element_type=F32)
            for k in range(n_w)]
    epilogue(accs, aux_refs, o_ref, j, i)


def _mm(a_list, w_list, aux_list, epilogue, *, n_out, out_dtype, tn, a_of_w=None,
        n_rows=R_ALL, tm=TM, alias_aux=None, name="mm"):
    n_a, n_w, n_aux = len(a_list), len(w_list), len(aux_list)
    a_of_w = tuple(a_of_w) if a_of_w is not None else tuple(range(n_w))
    in_specs, args, scratch = [], [], []
    for a in a_list:
        in_specs.append(pl.BlockSpec((tm, a.shape[1]), lambda j, i: (i, 0)))
        args.append(a)
    for w, prefix in w_list:
        k = w.shape[-2]
        in_specs.append(pl.BlockSpec(
            (None,) * len(prefix) + (k, tn),
            functools.partial(lambda j, i, p: p + (0, j), p=tuple(prefix))))
        args.append(w)
        scratch.append(pltpu.VMEM((k, tn), BF16))
    for arr, block, imap in aux_list:
        in_specs.append(pl.BlockSpec(block, imap))
        args.append(arr)
    aliases = {}
    if alias_aux is not None:
        aliases = {n_a + n_w + alias_aux: 0}
    return pl.pallas_call(
        functools.partial(_mm_body, n_a=n_a, n_w=n_w, n_aux=n_aux, epilogue=epilogue, a_of_w=a_of_w),
        grid=(n_out // tn, n_rows // tm),
        in_specs=in_specs,
        out_specs=pl.BlockSpec((tm, tn), lambda j, i: (i, j)),
        out_shape=jax.ShapeDtypeStruct((n_rows, n_out), out_dtype),
        scratch_shapes=scratch,
        input_output_aliases=aliases,
        compiler_params=_cparams(2),
        name=name,
    )(*args)


IN_TN = 512
_J_K = OFF_K // IN_TN
_J_V = OFF_V // IN_TN
_J_G = OFF_G // IN_TN


def _inproj_epilogue(accs, aux, o_ref, j, i):
    gains_ref, cos_ref, sin_ref, bg_ref = aux
    acc = accs[0]

    @pl.when(j < _J_V)
    def _():
        gain = gains_ref[pl.ds(j // _J_K, 1), :]
        cos = cos_ref[...]
        sin = sin_ref[...]
        lane = lax.broadcasted_iota(jnp.int32, (1, HEAD_DIM), 1)
        first_half = (lane % (2 * ROT_FREQS)) < ROT_FREQS
        for h in range(IN_TN // HEAD_DIM):
            xh = acc[:, h * HEAD_DIM:(h + 1) * HEAD_DIM]
            ms = jnp.mean(xh * xh, axis=-1, keepdims=True)
            y = xh * lax.rsqrt(ms + EPS) * gain
            partner = jnp.where(first_half,
                                pltpu.roll(y, HEAD_DIM - ROT_FREQS, 1),
                                pltpu.roll(y, ROT_FREQS, 1))
            o_ref[:, h * HEAD_DIM:(h + 1) * HEAD_DIM] = (y * cos + partner * sin).astype(BF16)

    @pl.when((j >= _J_V) & (j < _J_G))
    def _():
        o_ref[...] = acc.astype(BF16)

    @pl.when(j >= _J_G)
    def _():
        o_ref[...] = jax.nn.sigmoid(acc + bg_ref[...]).astype(BF16)


def _inproj(h, w_in, l, gains, cos_t, sin_t, b_gate):
    def rope_map(j, i):
        return (jnp.where(i >= CTX_TILE, SEQ // TM, i % (SEQ // TM)), 0)

    aux = [
        (gains, (None, 8, HEAD_DIM), lambda j, i: (l, 0, 0)),
        (cos_t, (TM, HEAD_DIM), rope_map),
        (sin_t, (TM, HEAD_DIM), rope_map),
        (b_gate.reshape(DEPTH, 1, 2 * D_MODEL), (None, 1, IN_TN),
         lambda j, i: (l, 0, jnp.maximum(j - _J_G, 0))),
    ]
    return _mm([h], [(w_in, (l,))], aux, _inproj_epilogue, n_out=IN_COLS, out_dtype=BF16,
               tn=IN_TN, name="inproj")


MG_TN = 512


def _merge_epilogue(accs, aux, o_ref, j, i):
    ga_ref, gb_ref = aux
    o_ref[...] = (ga_ref[...].astype(F32) * accs[0] + gb_ref[...].astype(F32) * accs[1]).astype(BF16)


def _merge(ya, yb, p, w_br_a, w_br_b, l):
    jg = OFF_G // MG_TN
    aux = [
        (p, (TM, MG_TN), lambda j, i: (i, jg + j)),
        (p, (TM, MG_TN), lambda j, i: (i, jg + D_MODEL // MG_TN + j)),
    ]
    return _mm([ya, yb], [(w_br_a, (l,)), (w_br_b, (l,))], aux, _merge_epilogue,
               n_out=D_MODEL, out_dtype=BF16, tn=MG_TN, name="merge")


RS_TN = 512
FFD_TM = 512


def _resid_epilogue(accs, aux, o_ref, j, i, with_rowscale, tm, tn):
    x_ref, g_ref = aux[0], aux[1]
    g = g_ref[pl.ds(_row_seg(i, tm), 1), :]
    if with_rowscale:
        rs = aux[2][...]
        for c in range(tn // LANES):
            sl = slice(c * LANES, (c + 1) * LANES)
            o_ref[:, sl] = x_ref[:, sl] + g[:, sl] * rs * accs[0][:, sl]
    else:
        o_ref[...] = x_ref[...] + g * accs[0]


def _resid_proj(a, w, prefix, x, mods, l, which, rowscale=None, tm=TM, tn=RS_TN, name="resid"):
    nb = D_MODEL // tn
    aux = [
        (x, (tm, tn), lambda j, i: (i, j)),
        (mods, (None, 8, tn), lambda j, i: (l, 0, which * nb + j)),
    ]
    if rowscale is not None:
        aux.append((rowscale, (tm, LANES), lambda j, i: (i, 0)))
    epi = functools.partial(_resid_epilogue, with_rowscale=rowscale is not None, tm=tm, tn=tn)
    return _mm([a], [(w, prefix)], aux, epi, n_out=D_MODEL, out_dtype=F32, tm=tm, tn=tn,
               alias_aux=0, name=name)


FF_TN = 512


def _swiglu_epilogue(accs, aux, o_ref, j, i):
    a = accs[0]
    o_ref[...] = (a * jax.nn.sigmoid(a) * accs[1]).astype(BF16)


def _swiglu_up(h, w1, w3, prefix):
    return _mm([h], [(w1, prefix), (w3, prefix)], [], _swiglu_epilogue, n_out=D_FF,
               out_dtype=BF16, tn=FF_TN, a_of_w=(0, 0), name="swiglu_up")


AT_TQ = 256


def _attn_body(lam_ref, q_ref, *rest, n_seg, post_scale):
    k_refs = rest[:n_seg]
    v_refs = rest[n_seg:2 * n_seg]
    sub_ref = rest[2 * n_seg]
    o_ref = rest[2 * n_seg + 1]
    lam = lam_ref[0]
    outs = []
    for s in range(2):
        sl = slice(s * HEAD_DIM, (s + 1) * HEAD_DIM)
        qs = q_ref[:, sl]
        scores = [lax.dot_general(qs, k_ref[:, sl], (((1,), (1,)), ((), ())),
                                  preferred_element_type=F32) for k_ref in k_refs]
        m = scores[0].max(axis=-1, keepdims=True)
        for sc in scores[1:]:
            m = jnp.maximum(m, sc.max(axis=-1, keepdims=True))
        den = None
        num = None
        for sc, v_ref in zip(scores, v_refs):
            p = jnp.exp(sc - m)
            ps = p.sum(axis=-1, keepdims=True)
            pv = jnp.dot(p.astype(BF16), v_ref[...], preferred_element_type=F32)
            den = ps if den is None else den + ps
            num = pv if num is None else num + pv
        outs.append(num / den)
    d = outs[0] - lam * outs[1]
    ms = jnp.mean(d * d, axis=-1, keepdims=True)
    o_ref[...] = (d * lax.rsqrt(ms + EPS) * sub_ref[...] * post_scale).astype(BF16)


def _attention(p, lam, sub_norm, l, lam_init, ya_prev=None):
    kb = OFF_K // V_HEAD_DIM
    vb = OFF_V // V_HEAD_DIM
    ctx_blk0 = R_LAT // CTX_LEN
    ctx_only = ya_prev is not None
    if ctx_only:
        tq = CTX_LEN
        grid = (BATCH, N_DIFF_HEADS, 1)
        q_map = lambda b, h, t: (ctx_blk0 + b, h)
        segs = [CTX_LEN]
    else:
        tq = AT_TQ
        grid = (BATCH, N_DIFF_HEADS, SEQ // tq)
        q_map = lambda b, h, t: (b * (SEQ // tq) + t, h)
        segs = [CTX_LEN, SEQ]
    in_specs = [pl.BlockSpec(memory_space=pltpu.SMEM),
                pl.BlockSpec((tq, V_HEAD_DIM), q_map)]
    args = [lam.reshape(1), p]
    for off in (kb, vb):
        for seg_len in segs:
            if seg_len == CTX_LEN:
                imap = functools.partial(lambda b, h, t, o: (ctx_blk0 + b, o + h), o=off)
            else:
                imap = functools.partial(lambda b, h, t, o: (b, o + h), o=off)
            in_specs.append(pl.BlockSpec((seg_len, V_HEAD_DIM), imap))
            args.append(p)
    in_specs.append(pl.BlockSpec((None, 1, V_HEAD_DIM), lambda b, h, t: (l, 0, 0)))
    args.append(sub_norm.reshape(DEPTH, 1, V_HEAD_DIM))
    aliases = {}
    if ctx_only:
        in_specs.append(pl.BlockSpec(memory_space=pl.ANY))
        args.append(ya_prev)
        aliases = {len(args) - 1: 0}
    body = functools.partial(_attn_body, n_seg=len(segs), post_scale=1.0 - lam_init)
    if ctx_only:
        inner = body
        body = lambda *refs: inner(*refs[:-2], refs[-1])
    return pl.pallas_call(
        body,
        grid=grid,
        in_specs=in_specs,
        out_specs=pl.BlockSpec((tq, V_HEAD_DIM), q_map),
        out_shape=jax.ShapeDtypeStruct((R_ALL, ATT_W), BF16),
        input_output_aliases=aliases,
        compiler_params=_cparams(3),
        name="attn_ctx" if ctx_only else "attn",
    )(*args)


def _dft_tables():
    n, n1, n2 = FFT_N, FFT_N1, FFT_N2
    k1 = np.arange(n1)[:, None]
    tabs_d, tabs_s, tabs_i = [], [], []
    for b in range(n2):
        cols = n2 * np.arange(n1)[None, :] + b
        g = np.exp(-2j * np.pi * ((k1 * cols) % n) / n)
        gd = g[:, :n1 // 2]
        tabs_d.append(np.block([[gd.real, -gd.imag], [gd.imag, gd.real]]))
        tabs_s.append(np.concatenate([g.real, g.imag], axis=0))
        gi = np.conj(gd).T / n1
        tabs_i.append(np.block([[gi.real, -gi.imag], [gi.imag, gi.real]]))
    kk = np.arange(n2)
    f2 = np.exp(-2j * np.pi * ((kk[:, None] * kk[None, :]) % n2) / n2)
    g2 = np.block([[f2.real, -f2.imag], [f2.imag, f2.real]])
    f2i = np.conj(f2) / n2
    g2i = np.block([[f2i.real, -f2i.imag], [f2i.imag, f2i.real]])
    as_bf = lambda a: jnp.asarray(np.asarray(a, np.float32), dtype=BF16)
    return (as_bf(np.stack(tabs_d)), as_bf(np.stack(tabs_s)), as_bf(np.stack(tabs_i)),
            as_bf(g2), as_bf(g2i))


def _spectrum_body(k_ref, g1_ref, g2_ref, or_ref, oi_ref, ar_ref, ai_ref):
    n1, n2 = FFT_N1, FFT_N2

    def stage1(b, carry):
        x = k_ref[pl.ds(b, n1, stride=n2), :].astype(BF16)
        a = jnp.dot(g1_ref[b], x, preferred_element_type=F32)
        row = pl.multiple_of(b * n1, n1)
        ar_ref[pl.ds(row, n1), :] = a[:n1]
        ai_ref[pl.ds(row, n1), :] = a[n1:]
        return carry

    lax.fori_loop(0, n2, stage1, 0)

    def stage2(k1, carry):
        a = jnp.concatenate([ar_ref[pl.ds(k1, n2, stride=n1), :],
                             ai_ref[pl.ds(k1, n2, stride=n1), :]], axis=0).astype(BF16)
        x = jnp.dot(g2_ref[...], a, preferred_element_type=F32)
        row = pl.multiple_of(k1 * n2, n2)
        or_ref[pl.ds(row, n2), :] = x[:n2].astype(BF16)
        oi_ref[pl.ds(row, n2), :] = x[n2:].astype(BF16)
        return carry

    lax.fori_loop(0, n1, stage2, 0)


def _spectrum(kfilt, g1s, g2):
    nf, _, c = kfilt.shape
    tc = HY_TC
    const3 = lambda f, ct: (0, 0, 0)
    spec = pl.BlockSpec((None, FFT_N, tc), lambda f, ct: (f, 0, ct))
    return pl.pallas_call(
        _spectrum_body,
        grid=(nf, c // tc),
        in_specs=[spec,
                  pl.BlockSpec((FFT_N2, 2 * FFT_N1, FFT_N1), const3),
                  pl.BlockSpec((2 * FFT_N2, 2 * FFT_N2), lambda f, ct: (0, 0))],
        out_specs=[spec, spec],
        out_shape=[jax.ShapeDtypeStruct((nf, FFT_N, c), BF16)] * 2,
        scratch_shapes=[pltpu.VMEM((FFT_N, tc), F32)] * 2,
        compiler_params=_cparams(2),
        name="hy_spectrum",
    )(kfilt, g1s, g2)


def _conv3(u, w_ref, b_ref):
    n = u.shape[0]
    row = lax.broadcasted_iota(jnp.int32, u.shape, 0)
    prev = jnp.where(row == 0, 0.0, pltpu.roll(u, 1, 0))
    nxt = jnp.where(row == n - 1, 0.0, pltpu.roll(u, n - 1, 0))
    return prev * w_ref[0:1, :] + u * w_ref[1:2, :] + nxt * w_ref[2:3, :] + b_ref[...]


def _hyconv_body(s0_ref, s1_ref, m0_ref, m1_ref, cws_ref, cbs_ref, cwm_ref, cbm_ref, bias_ref,
                 kr_ref, ki_ref, g1_ref, g1i_ref, g2_ref, g2i_ref, o_ref,
                 vr_ref, vi_ref, ar_ref, ai_ref, *, conv_signal):
    n1, n2 = FFT_N1, FFT_N2
    h1 = n1 // 2
    for s_ref, v_ref in ((s0_ref, vr_ref), (s1_ref, vi_ref)):
        u = s_ref[...].astype(F32)
        v_ref[...] = _conv3(u, cws_ref, cbs_ref) if conv_signal else u

    def stage1(b, carry):
        x = jnp.concatenate([vr_ref[pl.ds(b, h1, stride=n2), :],
                             vi_ref[pl.ds(b, h1, stride=n2), :]], axis=0).astype(BF16)
        a = jnp.dot(g1_ref[b], x, preferred_element_type=F32)
        row = pl.multiple_of(b * n1, n1)
        ar_ref[pl.ds(row, n1), :] = a[:n1]
        ai_ref[pl.ds(row, n1), :] = a[n1:]
        return carry

    lax.fori_loop(0, n2, stage1, 0)

    def stage2(k1, carry):
        a = jnp.concatenate([ar_ref[pl.ds(k1, n2, stride=n1), :],
                             ai_ref[pl.ds(k1, n2, stride=n1), :]], axis=0).astype(BF16)
        x = jnp.dot(g2_ref[...], a, preferred_element_type=F32)
        row = pl.multiple_of(k1 * n2, n2)
        fr = kr_ref[pl.ds(row, n2), :].astype(F32)
        fi = ki_ref[pl.ds(row, n2), :].astype(F32)
        xr, xi = x[:n2], x[n2:]
        y = jnp.concatenate([xr * fr - xi * fi, xr * fi + xi * fr], axis=0).astype(BF16)
        bq = jnp.dot(g2i_ref[...], y, preferred_element_type=F32)
        ar_ref[pl.ds(k1, n2, stride=n1), :] = bq[:n2]
        ai_ref[pl.ds(k1, n2, stride=n1), :] = bq[n2:]
        return carry

    lax.fori_loop(0, n1, stage2, 0)

    bias = bias_ref[...]

    def stage1_inv(b, carry):
        row = pl.multiple_of(b * n1, n1)
        bq = jnp.concatenate([ar_ref[pl.ds(row, n1), :], ai_ref[pl.ds(row, n1), :]],
                             axis=0).astype(BF16)
        y = jnp.dot(g1i_ref[b], bq, preferred_element_type=F32)
        vr_ref[pl.ds(b, h1, stride=n2), :] = y[:h1] + bias * vr_ref[pl.ds(b, h1, stride=n2), :]
        vi_ref[pl.ds(b, h1, stride=n2), :] = y[h1:] + bias * vi_ref[pl.ds(b, h1, stride=n2), :]
        return carry

    lax.fori_loop(0, n2, stage1_inv, 0)

    for half, (m_ref, v_ref) in enumerate(((m0_ref, vr_ref), (m1_ref, vi_ref))):
        mult = _conv3(m_ref[...].astype(F32), cwm_ref, cbm_ref)
        o_ref[half * SEQ:(half + 1) * SEQ, :] = (mult * v_ref[...]).astype(o_ref.dtype)


def _hy_param_specs(conv_w, conv_b, bias, l, filt, sig_ch, mul_ch, tc):
    cw3 = conv_w.reshape(DEPTH, 3, 3 * HY_W)
    cb3 = conv_b.reshape(DEPTH, 1, 3 * HY_W)
    chan = lambda ch0, rows: pl.BlockSpec(
        (None, rows, tc), functools.partial(lambda ct, pr, c0: (l, 0, c0 + ct), c0=ch0))
    specs = [chan(sig_ch, 3), chan(sig_ch, 1), chan(mul_ch, 3), chan(mul_ch, 1),
             pl.BlockSpec((None, None, 1, tc), lambda ct, pr: (l, filt, 0, ct))]
    args = [cw3, cb3, cw3, cb3, bias.reshape(DEPTH, HY_N_FILT, 1, HY_W)]
    return specs, args


def _hyconv(sig, sig_cb, mul, mul_cb, conv_w, conv_b, bias, l, filt, sig_ch, mul_ch,
            kr, ki, tabs, conv_signal):
    g1d, g1i, g2, g2i = tabs
    tc = HY_TC
    blk = lambda arr_cb, odd: pl.BlockSpec(
        (SEQ, tc), functools.partial(lambda ct, pr, cb, o: (2 * pr + o, cb + ct), cb=arr_cb, o=odd))
    pspecs, pargs = _hy_param_specs(conv_w, conv_b, bias, l, filt, sig_ch, mul_ch, tc)
    kspec = pl.BlockSpec((None, FFT_N, tc), lambda ct, pr: (filt, 0, ct))
    const3 = lambda ct, pr: (0, 0, 0)
    const2 = lambda ct, pr: (0, 0)
    in_specs = [blk(sig_cb, 0), blk(sig_cb, 1), blk(mul_cb, 0), blk(mul_cb, 1)] + pspecs + [
        kspec, kspec,
        pl.BlockSpec((FFT_N2, 2 * FFT_N1, FFT_N1), const3),
        pl.BlockSpec((FFT_N2, FFT_N1, 2 * FFT_N1), const3),
        pl.BlockSpec((2 * FFT_N2, 2 * FFT_N2), const2),
        pl.BlockSpec((2 * FFT_N2, 2 * FFT_N2), const2),
    ]
    return pl.pallas_call(
        functools.partial(_hyconv_body, conv_signal=conv_signal),
        grid=(HY_W // tc, BATCH // 2),
        in_specs=in_specs,
        out_specs=pl.BlockSpec((2 * SEQ, tc), lambda ct, pr: (pr, ct)),
        out_shape=jax.ShapeDtypeStruct((R_ALL, HY_W), BF16),
        scratch_shapes=[pltpu.VMEM((SEQ, tc), F32)] * 2 + [pltpu.VMEM((FFT_N, tc), F32)] * 2,
        compiler_params=_cparams(2),
        name="hyconv",
    )(sig, sig, mul, mul, *pargs, kr, ki, g1d, g1i, g2, g2i)


def _ctx_dft_tables():
    n = 2 * CTX_LEN
    kk = np.arange(n)
    f = np.exp(-2j * np.pi * ((kk[:, None] * kk[None, :]) % n) / n)
    fd = f[:, :CTX_LEN]
    fwd = np.block([[fd.real, -fd.imag], [fd.imag, fd.real]])
    spec = np.concatenate([f.real, f.imag], axis=0)
    fi = np.conj(f)[:CTX_LEN, :] / n
    inv = np.block([[fi.real, -fi.imag], [fi.imag, fi.real]])
    as_bf = lambda a: jnp.asarray(np.asarray(a, np.float32), dtype=BF16)
    return as_bf(fwd), as_bf(spec), as_bf(inv)


def _hyconv_ctx_body(s0_ref, s1_ref, m0_ref, m1_ref, cws_ref, cbs_ref, cwm_ref, cbm_ref, bias_ref,
                     k_ref, fwd_ref, spec_ref, inv_ref, prev_ref, o_ref, *, conv_signal):
    del prev_ref
    n = 2 * CTX_LEN
    vs = []
    for s_ref in (s0_ref, s1_ref):
        u = s_ref[...].astype(F32)
        vs.append(_conv3(u, cws_ref, cbs_ref) if conv_signal else u)
    x = jnp.dot(fwd_ref[...], jnp.concatenate(vs, axis=0).astype(BF16), preferred_element_type=F32)
    kf = jnp.dot(spec_ref[...], k_ref[...].astype(BF16), preferred_element_type=F32)
    kf = kf.astype(BF16).astype(F32)
    xr, xi, fr, fi = x[:n], x[n:], kf[:n], kf[n:]
    y = jnp.concatenate([xr * fr - xi * fi, xr * fi + xi * fr], axis=0).astype(BF16)
    conv = jnp.dot(inv_ref[...], y, preferred_element_type=F32)
    bias = bias_ref[...]
    for half, m_ref in enumerate((m0_ref, m1_ref)):
        sl = slice(half * CTX_LEN, (half + 1) * CTX_LEN)
        mult = _conv3(m_ref[...].astype(F32), cwm_ref, cbm_ref)
        o_ref[sl, :] = (mult * (conv[sl] + bias * vs[half])).astype(o_ref.dtype)


def _hyconv_ctx(sig, sig_cb, mul, mul_cb, conv_w, conv_b, bias, l, filt, sig_ch, mul_ch,
                kctx, tabs, conv_signal, prev):
    fwd, spec, inv = tabs
    tc = HY_TC
    blk0 = R_LAT // CTX_LEN
    blk = lambda arr_cb, odd: pl.BlockSpec(
        (CTX_LEN, tc),
        functools.partial(lambda ct, pr, cb, o: (blk0 + 2 * pr + o, cb + ct), cb=arr_cb, o=odd))
    pspecs, pargs = _hy_param_specs(conv_w, conv_b, bias, l, filt, sig_ch, mul_ch, tc)
    const2 = lambda ct, pr: (0, 0)
    n = 2 * CTX_LEN
    in_specs = [blk(sig_cb, 0), blk(sig_cb, 1), blk(mul_cb, 0), blk(mul_cb, 1)] + pspecs + [
        pl.BlockSpec((None, n, tc), lambda ct, pr: (filt, 0, ct)),
        pl.BlockSpec((2 * n, n), const2),
        pl.BlockSpec((2 * n, n), const2),
        pl.BlockSpec((n, 2 * n), const2),
        pl.BlockSpec(memory_space=pl.ANY),
    ]
    args = [sig, sig, mul, mul, *pargs, kctx, fwd, spec, inv, prev]
    return pl.pallas_call(
        functools.partial(_hyconv_ctx_body, conv_signal=conv_signal),
        grid=(HY_W // tc, BATCH // 2),
        in_specs=in_specs,
        out_specs=pl.BlockSpec((2 * CTX_LEN, tc), lambda ct, pr: (R_LAT // (2 * CTX_LEN) + pr, ct)),
        out_shape=jax.ShapeDtypeStruct((R_ALL, HY_W), BF16),
        input_output_aliases={len(args) - 1: 0},
        compiler_params=_cparams(2),
        name="hyconv_ctx",
    )(*args)


def _rope_tables():
    rows = SEQ // GRID_W
    row = jnp.repeat(jnp.arange(rows), GRID_W).astype(F32)
    col = jnp.tile(jnp.arange(GRID_W), rows).astype(F32)
    inv = ROPE_BASE ** (-jnp.arange(ROT_FREQS, dtype=F32) / ROT_FREQS)
    ar, ac = row[:, None] * inv, col[:, None] * inv
    cos = jnp.concatenate([jnp.cos(ar), jnp.cos(ar), jnp.cos(ac), jnp.cos(ac)], axis=1)
    sin = jnp.concatenate([-jnp.sin(ar), jnp.sin(ar), -jnp.sin(ac), jnp.sin(ac)], axis=1)
    cos = jnp.concatenate([cos, jnp.ones((TM, HEAD_DIM), F32)], axis=0)
    sin = jnp.concatenate([sin, jnp.zeros((TM, HEAD_DIM), F32)], axis=0)
    return cos, sin


def _hyena_filters(L, w1, b1, w2, b2, w3, b3, w4, freq):
    t = jnp.linspace(0.0, 1.0, L, dtype=F32)[:, None]
    w = 2.0 * math.pi * jnp.arange(L, dtype=F32)[:, None] / L
    f = jnp.linspace(1e-4, HY_BANDS - 1, HY_BANDS, dtype=F32)
    z = jnp.concatenate([t, jnp.cos(w * f), -jnp.sin(w * f)], axis=-1)
    dot = functools.partial(jnp.dot, precision=HIGHEST)
    hid = jnp.sin(freq * (dot(z, w1) + b1))
    hid = jnp.sin(freq * (dot(hid, w2) + b2))
    hid = jnp.sin(freq * (dot(hid, w3) + b3))
    h = dot(hid, w4).reshape(L, HY_N_FILT, 2, HY_W)
    deltas = jnp.abs(jnp.linspace(math.log(HY_TARGET) / HY_SLOW_PCT,
                                  math.log(HY_TARGET) / HY_FAST_PCT, HY_W, dtype=F32))
    decay = jnp.exp(-t * deltas)
    h = h * decay[:, None, None, :]
    k = jnp.concatenate([h[:, :, 0], jnp.zeros((1, HY_N_FILT, HY_W), F32), h[:0:-1, :, 1]], axis=0)
    return k.transpose(1, 0, 2)


def kernel(x, c, ctx, c_ctx, w_ada, b_ada, norm1, norm2, w_in, b_gate, q_norm, k_norm, lam_q1, lam_k1, lam_q2, lam_k2, sub_norm, hy_conv_w, hy_conv_b, hy_w1, hy_b1, hy_w2, hy_b2, hy_w3, hy_b3, hy_w4, hy_freq, hy_bias, w_br_a, w_br_b, w_out, ffn_w1, ffn_w3, ffn_w2, router, moe_w1, moe_w3, moe_w2):
    xs = jnp.concatenate([x.reshape(R_LAT, D_MODEL), ctx.reshape(R_CTX, D_MODEL)], axis=0)
    cs = jnp.concatenate([c, c_ctx[None], jnp.zeros((8 - BATCH - 1, D_MODEL), F32)], axis=0)
    mods = _adaln(cs, w_ada, b_ada)
    cos_t, sin_t = _rope_tables()
    g1d, g1s, g1i, g2, g2i = _dft_tables()
    ctx_tabs = _ctx_dft_tables()
    gains = jnp.zeros((DEPTH, 8, HEAD_DIM), F32)
    gains = gains.at[:, 0].set(q_norm / math.sqrt(HEAD_DIM)).at[:, 1].set(k_norm)
    router_pad = jnp.zeros((router.shape[0], D_MODEL, LANES), F32).at[:, :, :N_EXPERTS].set(router)
    hy_cb = OFF_HY // HY_TC
    ch = HY_W // HY_TC

    for l in range(DEPTH):
        lam_init = 0.8 - 0.6 * math.exp(-0.3 * l)
        lam = (jnp.exp(jnp.sum(lam_q1[l] * lam_k1[l])) - jnp.exp(jnp.sum(lam_q2[l] * lam_k2[l]))
               + lam_init)
        h = _normmod(xs, norm1, mods, l, 0)
        p = _inproj(h, w_in, l, gains, cos_t, sin_t, b_gate)

        ya = _attention(p, lam, sub_norm, l, lam_init)
        ya = _attention(p, lam, sub_norm, l, lam_init, ya_prev=ya)

        hp = (hy_w1[l], hy_b1[l], hy_w2[l], hy_b2[l], hy_w3[l], hy_b3[l], hy_w4[l], hy_freq[l])
        kr, ki = _spectrum(_hyena_filters(SEQ, *hp), g1s, g2)
        kctx = _hyena_filters(CTX_LEN, *hp)
        common = (hy_conv_w, hy_conv_b, hy_bias, l)
        z = _hyconv(p, hy_cb, p, hy_cb + ch, *common, 0, 0, ch, kr, ki, (g1d, g1i, g2, g2i), True)
        z = _hyconv_ctx(p, hy_cb, p, hy_cb + ch, *common, 0, 0, ch, kctx, ctx_tabs, True, z)
        yb = _hyconv(z, 0, p, hy_cb + 2 * ch, *common, 1, 0, 2 * ch, kr, ki, (g1d, g1i, g2, g2i), False)
        yb = _hyconv_ctx(z, 0, p, hy_cb + 2 * ch, *common, 1, 0, 2 * ch, kctx, ctx_tabs, False, yb)

        m = _merge(ya, yb, p, w_br_a, w_br_b, l)
        xs = _resid_proj(m, w_out, (l,), xs, mods, l, 2, name="out_proj")

        i = l // 2
        if l % 2 == 0:
            h2 = _normmod(xs, norm2, mods, l, 3)
            g = _swiglu_up(h2, ffn_w1, ffn_w3, (i,))
            xs = _resid_proj(g, ffn_w2, (i,), xs, mods, l, 5, tm=FFD_TM, name="ffn_down")
        else:
            h2, logits = _normmod(xs, norm2, mods, l, 3, router=router_pad[i])
            top_v, top_i = lax.top_k(logits[:, :N_EXPERTS], TOP_K)
            wts = jax.nn.softmax(top_v, axis=-1)
            combine = jnp.einsum('nk,nke->ne', wts, jax.nn.one_hot(top_i, N_EXPERTS, dtype=F32))
            for e in range(N_EXPERTS):
                g = _swiglu_up(h2, moe_w1, moe_w3, (i, e))
                rs = jnp.broadcast_to(combine[:, e:e + 1], (R_ALL, LANES))
                xs = _resid_proj(g, moe_w2, (i, e), xs, mods, l, 5, rowscale=rs, tm=FFD_TM, name="moe_down")
    return xs[:R_LAT].reshape(BATCH, SEQ, D_MODEL)
```

```python
import functools
import math

import numpy as np
import jax
import jax.numpy as jnp
from jax import lax
from jax.experimental import pallas as pl
from jax.experimental.pallas import tpu as pltpu

D_MODEL = 2048
BATCH = 4
SEQ = 4096
DEPTH = 4
CTX_LEN = 256
GRID_W = 64
HEAD_DIM = 128
N_DIFF_HEADS = D_MODEL // (2 * HEAD_DIM)
V_HEAD_DIM = 2 * HEAD_DIM
ATT_W = N_DIFF_HEADS * V_HEAD_DIM
HY_W = D_MODEL
OFF_K = ATT_W
OFF_V = 2 * ATT_W
OFF_HY = 3 * ATT_W
OFF_G = 3 * ATT_W + 3 * HY_W
IN_COLS = OFF_G + 2 * D_MODEL
ROPE_BASE = 10000.0
ROT_FREQS = HEAD_DIM // 4
HY_EMB = 33
HY_BANDS = (HY_EMB - 1) // 2
HY_FH = 64
HY_N_FILT = 2
HY_TARGET = 1e-2
HY_FAST_PCT = 0.3
HY_SLOW_PCT = 1.5
D_FF = 5632
N_EXPERTS = 8
TOP_K = 2
EPS = 1e-6

R_LAT = BATCH * SEQ
R_CTX = BATCH * CTX_LEN
R_ALL = R_LAT + R_CTX
TM = 1024
CTX_TILE = R_LAT // TM
CTX_SEG = BATCH
TR = 256
LANES = 128
VMEM_LIMIT = 56 * 1024 * 1024

FFT_N = 2 * SEQ
FFT_N1 = 128
FFT_N2 = 64
HY_TC = 128

F32 = jnp.float32
BF16 = jnp.bfloat16
HIGHEST = lax.Precision.HIGHEST


def _cparams(n_axes):
    return pltpu.CompilerParams(
        dimension_semantics=("arbitrary",) * n_axes, vmem_limit_bytes=VMEM_LIMIT)


def _row_seg(i, tm=TM):
    return jnp.where(i >= R_LAT // tm, CTX_SEG, i // (SEQ // tm))


def _adaln_body(c_ref, w_ref, b_ref, o_ref):
    s = c_ref[...]
    s = s * jax.nn.sigmoid(s)
    o_ref[...] = jnp.dot(s, w_ref[...], preferred_element_type=F32, precision=HIGHEST) + b_ref[...]


def _adaln(cs, w_ada, b_ada):
    tn = 1536
    n = 6 * D_MODEL
    return pl.pallas_call(
        _adaln_body,
        grid=(DEPTH, n // tn),
        in_specs=[
            pl.BlockSpec((8, D_MODEL), lambda l, j: (0, 0)),
            pl.BlockSpec((None, D_MODEL, tn), lambda l, j: (l, 0, j)),
            pl.BlockSpec((None, 1, tn), lambda l, j: (l, 0, j)),
        ],
        out_specs=pl.BlockSpec((None, 8, tn), lambda l, j: (l, 0, j)),
        out_shape=jax.ShapeDtypeStruct((DEPTH, 8, n), F32),
        compiler_params=_cparams(2),
        name="adaln",
    )(cs, w_ada, b_ada.reshape(DEPTH, 1, n))


def _normmod_body(x_ref, g_ref, sh_ref, sc_ref, *rest, with_router):
    i = pl.program_id(0)
    seg = jnp.where(i >= R_LAT // TR, CTX_SEG, i // (SEQ // TR))
    x = x_ref[...]
    ms = jnp.mean(x * x, axis=-1, keepdims=True)
    y = x * lax.rsqrt(ms + EPS) * g_ref[...]
    h = y * (1.0 + sc_ref[pl.ds(seg, 1), :]) + sh_ref[pl.ds(seg, 1), :]
    if with_router:
        r_ref, o_ref, lg_ref = rest
        lg_ref[...] = jnp.dot(h, r_ref[...], preferred_element_type=F32, precision=HIGHEST)
    else:
        (o_ref,) = rest
    o_ref[...] = h.astype(o_ref.dtype)


def _normmod(x, gain, mods, l, which, router=None):
    in_specs = [
        pl.BlockSpec((TR, D_MODEL), lambda i: (i, 0)),
        pl.BlockSpec((None, 1, D_MODEL), lambda i: (l, 0, 0)),
        pl.BlockSpec((None, 8, D_MODEL), lambda i: (l, 0, which)),
        pl.BlockSpec((None, 8, D_MODEL), lambda i: (l, 0, which + 1)),
    ]
    args = [x, gain.reshape(DEPTH, 1, D_MODEL), mods, mods]
    out_specs = pl.BlockSpec((TR, D_MODEL), lambda i: (i, 0))
    out_shape = jax.ShapeDtypeStruct((R_ALL, D_MODEL), BF16 if router is None else F32)
    if router is not None:
        in_specs.append(pl.BlockSpec((D_MODEL, LANES), lambda i: (0, 0)))
        args.append(router)
        out_specs = [out_specs, pl.BlockSpec((TR, LANES), lambda i: (i, 0))]
        out_shape = [out_shape, jax.ShapeDtypeStruct((R_ALL, LANES), F32)]
    return pl.pallas_call(
        functools.partial(_normmod_body, with_router=router is not None),
        grid=(R_ALL // TR,),
        in_specs=in_specs,
        out_specs=out_specs,
        out_shape=out_shape,
        compiler_params=_cparams(1),
        name="normmod",
    )(*args)


def _mm_body(*refs, n_pref, n_a, n_w, n_aux, epilogue, a_of_w, split_ctx):
    pref = refs[:n_pref]
    refs = refs[n_pref:]
    if split_ctx:
        a_refs, a_ctx_refs = refs[:2 * n_a:2], refs[1:2 * n_a:2]
        refs = refs[n_a:]
    else:
        a_refs = a_ctx_refs = refs[:n_a]
    w_refs = refs[n_a:n_a + n_w]
    aux_refs = refs[n_a + n_w:n_a + n_w + n_aux]
    o_ref = refs[n_a + n_w + n_aux]
    wbf_refs = refs[n_a + n_w + n_aux + 1:]
    j = pl.program_id(0)
    i = pl.program_id(1)
    if n_pref:
        te_ref, nu_ref = pref
        new_weights = (i == 0) | (te_ref[i] != te_ref[jnp.maximum(i - 1, 0)])
        valid = i < nu_ref[0]
    else:
        new_weights = i == 0
        valid = None

    @pl.when(new_weights)
    def _():
        for w_ref, wbf_ref in zip(w_refs, wbf_refs):
            wbf_ref[...] = w_ref[...].astype(BF16)

    def compute(srcs):
        accs = [jnp.dot(srcs[a_of_w[k]][...], wbf_refs[k][...], preferred_element_type=F32)
                for k in range(n_w)]
        epilogue(accs, aux_refs, o_ref, j, i)

    if split_ctx:
        pl.when(i < CTX_TILE)(functools.partial(compute, a_refs))
        pl.when(i >= CTX_TILE)(functools.partial(compute, a_ctx_refs))
    elif valid is None:
        compute(a_refs)
    else:
        pl.when(valid)(functools.partial(compute, a_refs))

        @pl.when(jnp.logical_not(valid))
        def _():
            o_ref[...] = jnp.zeros(o_ref.shape, o_ref.dtype)


def _mm(a_list, w_list, aux_list, epilogue, *, n_out, out_dtype, tn, a_of_w=None,
        n_rows=R_ALL, tm=TM, alias_aux=None, group=None, split_ctx=False, name="mm"):
    n_a, n_w, n_aux = len(a_list), len(w_list), len(aux_list)
    n_pref = 0 if group is None else 2
    a_of_w = tuple(a_of_w) if a_of_w is not None else tuple(range(n_w))
    in_specs, args, scratch = [], [], []
    for a in a_list:
        if split_ctx:
            lat, ctx = a
            assert tm == R_CTX and group is None
            in_specs.append(pl.BlockSpec((tm, lat.shape[1]),
                                         lambda j, i: (jnp.minimum(i, CTX_TILE - 1), 0)))
            in_specs.append(pl.BlockSpec((tm, ctx.shape[1]), lambda j, i: (0, 0),
                                         pipeline_mode=pl.Buffered(1)))
            args += [lat, ctx]
        else:
            in_specs.append(pl.BlockSpec((tm, a.shape[1]), lambda j, i, *_: (i, 0)))
            args.append(a)
    for w, prefix in w_list:
        k = w.shape[-2]
        if group is None:
            wmap = functools.partial(lambda j, i, p: p + (0, j), p=tuple(prefix))
        else:
            wmap = functools.partial(lambda j, i, te, nu, p: p[:-1] + (te[i], 0, j), p=tuple(prefix))
        in_specs.append(pl.BlockSpec((None,) * len(prefix) + (k, tn), wmap))
        args.append(w)
        scratch.append(pltpu.VMEM((k, tn), BF16))
    for arr, block, imap in aux_list:
        in_specs.append(pl.BlockSpec(block, functools.partial(lambda j, i, *_, f: f(j, i), f=imap)))
        args.append(arr)
    aliases = {}
    if alias_aux is not None:
        aliases = {n_pref + (2 * n_a if split_ctx else n_a) + n_w + alias_aux: 0}
    grid_spec = pltpu.PrefetchScalarGridSpec(
        num_scalar_prefetch=n_pref,
        grid=(n_out // tn, n_rows // tm),
        in_specs=in_specs,
        out_specs=pl.BlockSpec((tm, tn), lambda j, i, *_: (i, j)),
        scratch_shapes=scratch,
    )
    return pl.pallas_call(
        functools.partial(_mm_body, n_pref=n_pref, n_a=n_a, n_w=n_w, n_aux=n_aux,
                          epilogue=epilogue, a_of_w=a_of_w, split_ctx=split_ctx),
        grid_spec=grid_spec,
        out_shape=jax.ShapeDtypeStruct((n_rows, n_out), out_dtype),
        input_output_aliases=aliases,
        compiler_params=_cparams(2),
        name=name,
    )(*(list(group) if group is not None else []), *args)


IN_TN = 512
_J_K = OFF_K // IN_TN
_J_V = OFF_V // IN_TN
_J_G = OFF_G // IN_TN


def _inproj_epilogue(accs, aux, o_ref, j, i):
    gains_ref, cos_ref, sin_ref, bg_ref = aux
    acc = accs[0]

    @pl.when(j < _J_V)
    def _():
        gain = gains_ref[pl.ds(j // _J_K, 1), :]
        cos = cos_ref[...]
        sin = sin_ref[...]
        lane = lax.broadcasted_iota(jnp.int32, (1, HEAD_DIM), 1)
        first_half = (lane % (2 * ROT_FREQS)) < ROT_FREQS
        for h in range(IN_TN // HEAD_DIM):
            xh = acc[:, h * HEAD_DIM:(h + 1) * HEAD_DIM]
            ms = jnp.mean(xh * xh, axis=-1, keepdims=True)
            y = xh * lax.rsqrt(ms + EPS) * gain
            partner = jnp.where(first_half,
                                pltpu.roll(y, HEAD_DIM - ROT_FREQS, 1),
                                pltpu.roll(y, ROT_FREQS, 1))
            o_ref[:, h * HEAD_DIM:(h + 1) * HEAD_DIM] = (y * cos + partner * sin).astype(BF16)

    @pl.when((j >= _J_V) & (j < _J_G))
    def _():
        o_ref[...] = acc.astype(BF16)

    @pl.when(j >= _J_G)
    def _():
        o_ref[...] = jax.nn.sigmoid(acc + bg_ref[...]).astype(BF16)


def _inproj(h, w_in, l, gains, cos_t, sin_t, b_gate):
    def rope_map(j, i):
        return (jnp.where(i >= CTX_TILE, SEQ // TM, i % (SEQ // TM)), 0)

    aux = [
        (gains, (None, 8, HEAD_DIM), lambda j, i: (l, 0, 0)),
        (cos_t, (TM, HEAD_DIM), rope_map),
        (sin_t, (TM, HEAD_DIM), rope_map),
        (b_gate.reshape(DEPTH, 1, 2 * D_MODEL), (None, 1, IN_TN),
         lambda j, i: (l, 0, jnp.maximum(j - _J_G, 0))),
    ]
    return _mm([h], [(w_in, (l,))], aux, _inproj_epilogue, n_out=IN_COLS, out_dtype=BF16,
               tn=IN_TN, name="inproj")


MG_TN = 256


def _merge_epilogue(accs, aux, o_ref, j, i):
    ga_ref, gb_ref = aux
    o_ref[...] = (ga_ref[...].astype(F32) * accs[0] + gb_ref[...].astype(F32) * accs[1]).astype(BF16)


def _merge(ya, yb, p, w_br_a, w_br_b, l):
    jg = OFF_G // MG_TN
    aux = [
        (p, (TM, MG_TN), lambda j, i: (i, jg + j)),
        (p, (TM, MG_TN), lambda j, i: (i, jg + D_MODEL // MG_TN + j)),
    ]
    return _mm([ya, yb], [(w_br_a, (l,)), (w_br_b, (l,))], aux, _merge_epilogue,
               n_out=D_MODEL, out_dtype=BF16, tn=MG_TN, split_ctx=True, name="merge")


RS_TN = 512
FFD_TM = 512


def _resid_epilogue(accs, aux, o_ref, j, i, tm):
    x_ref, g_ref = aux
    o_ref[...] = x_ref[...] + g_ref[pl.ds(_row_seg(i, tm), 1), :] * accs[0]


def _resid_proj(a, w, prefix, x, mods, l, which, tm=TM, name="resid"):
    nb = D_MODEL // RS_TN
    aux = [
        (x, (tm, RS_TN), lambda j, i: (i, j)),
        (mods, (None, 8, RS_TN), lambda j, i: (l, 0, which * nb + j)),
    ]
    return _mm([a], [(w, prefix)], aux, functools.partial(_resid_epilogue, tm=tm),
               n_out=D_MODEL, out_dtype=F32, tm=tm, tn=RS_TN, alias_aux=0, name=name)


FF_TN = 512


def _swiglu_epilogue(accs, aux, o_ref, j, i):
    a = accs[0]
    o_ref[...] = (a * jax.nn.sigmoid(a) * accs[1]).astype(BF16)


def _swiglu_up(h, w1, w3, prefix, group=None):
    kw = dict(n_rows=MOE_P, tm=MOE_TM) if group is not None else {}
    return _mm([h], [(w1, prefix), (w3, prefix)], [], _swiglu_epilogue, n_out=D_FF,
               out_dtype=BF16, tn=FF_TN, a_of_w=(0, 0), group=group, name="swiglu_up", **kw)


MOE_TM = 512
MOE_A = TOP_K * R_ALL
MOE_NT = MOE_A // MOE_TM + N_EXPERTS
MOE_P = MOE_NT * MOE_TM
GATHER_ROWS = 256


def _route(logits):
    top_v, top_i = lax.top_k(logits[:, :N_EXPERTS], TOP_K)
    wts = jax.nn.softmax(top_v, axis=-1).reshape(-1)
    e_flat = top_i.reshape(-1).astype(jnp.int32)
    onehot = (e_flat[:, None] == jnp.arange(N_EXPERTS, dtype=jnp.int32)[None, :]).astype(jnp.int32)
    before = jnp.cumsum(onehot, axis=0) - onehot
    rank = jnp.sum(before * onehot, axis=1)
    counts = jnp.sum(onehot, axis=0)
    tiles = (counts + MOE_TM - 1) // MOE_TM
    tile_end = jnp.cumsum(tiles)
    pos = ((tile_end - tiles)[e_flat] * MOE_TM + rank).astype(jnp.int32)
    n_used = tile_end[-1:].astype(jnp.int32)
    tile_ids = jnp.minimum(jnp.arange(MOE_NT, dtype=jnp.int32), n_used[0] - 1)
    tile_expert = jnp.sum((tile_ids[:, None] >= tile_end[None, :]).astype(jnp.int32), axis=1)
    token = jnp.arange(MOE_A, dtype=jnp.int32) // TOP_K
    src_token = jnp.zeros((MOE_P,), jnp.int32).at[pos].set(token)
    w_sorted = jnp.zeros((MOE_P,), F32).at[pos].set(wts)
    return pos, src_token, w_sorted, tile_expert.astype(jnp.int32), n_used


def _gather_body(src_ref, nu_ref, h_ref, o_ref, buf_ref, sem):
    t = pl.program_id(0)
    base = t * GATHER_ROWS

    def row_copy(r, src_row):
        return pltpu.make_async_copy(h_ref.at[pl.ds(src_row, 1), :], buf_ref.at[pl.ds(r, 1), :], sem)

    @pl.when(base < nu_ref[0] * MOE_TM)
    def _():
        def issue(r, carry):
            row_copy(r, src_ref[base + r]).start()
            return carry

        lax.fori_loop(0, GATHER_ROWS, issue, 0, unroll=8)

        def drain(r, carry):
            row_copy(r, 0).wait()
            return carry

        lax.fori_loop(0, GATHER_ROWS, drain, 0, unroll=8)
        o_ref[...] = buf_ref[...].astype(BF16)

    @pl.when(base >= nu_ref[0] * MOE_TM)
    def _():
        o_ref[...] = jnp.zeros(o_ref.shape, o_ref.dtype)


def _moe_gather(h32, src_token, n_used):
    grid_spec = pltpu.PrefetchScalarGridSpec(
        num_scalar_prefetch=2,
        grid=(MOE_P // GATHER_ROWS,),
        in_specs=[pl.BlockSpec(memory_space=pl.ANY)],
        out_specs=pl.BlockSpec((GATHER_ROWS, D_MODEL), lambda t, *_: (t, 0)),
        scratch_shapes=[pltpu.VMEM((GATHER_ROWS, D_MODEL), F32), pltpu.SemaphoreType.DMA(())],
    )
    return pl.pallas_call(
        _gather_body,
        grid_spec=grid_spec,
        out_shape=jax.ShapeDtypeStruct((MOE_P, D_MODEL), BF16),
        compiler_params=_cparams(1),
        name="moe_gather",
    )(src_token, n_used, h32)


def _moe_down_epilogue(accs, aux, o_ref, j, i):
    rs = aux[0][...]
    for c in range(RS_TN // LANES):
        sl = slice(c * LANES, (c + 1) * LANES)
        o_ref[:, sl] = rs * accs[0][:, sl]


def _combine_body(pos_ref, y_ref, x_ref, g_ref, o_ref, buf_ref, sem):
    t = pl.program_id(0)
    base = t * (TOP_K * TR)

    def row_copy(slot, r, src_row):
        return pltpu.make_async_copy(y_ref.at[pl.ds(src_row, 1), :],
                                     buf_ref.at[slot, pl.ds(r, 1), :], sem)

    def issue(r, carry):
        for slot in range(TOP_K):
            row_copy(slot, r, pos_ref[base + TOP_K * r + slot]).start()
        return carry

    lax.fori_loop(0, TR, issue, 0, unroll=4)

    def drain(r, carry):
        for slot in range(TOP_K):
            row_copy(slot, r, 0).wait()
        return carry

    lax.fori_loop(0, TR, drain, 0, unroll=4)
    g = g_ref[pl.ds(_row_seg(t, TR), 1), :]
    o_ref[...] = x_ref[...] + g * (buf_ref[0] + buf_ref[1])


def _moe_combine(ys, pos, x, mods, l, which):
    grid_spec = pltpu.PrefetchScalarGridSpec(
        num_scalar_prefetch=1,
        grid=(R_ALL // TR,),
        in_specs=[pl.BlockSpec(memory_space=pl.ANY),
                  pl.BlockSpec((TR, D_MODEL), lambda t, *_: (t, 0)),
                  pl.BlockSpec((None, 8, D_MODEL), lambda t, *_: (l, 0, which))],
        out_specs=pl.BlockSpec((TR, D_MODEL), lambda t, *_: (t, 0)),
        scratch_shapes=[pltpu.VMEM((TOP_K, TR, D_MODEL), F32), pltpu.SemaphoreType.DMA(())],
    )
    return pl.pallas_call(
        _combine_body,
        grid_spec=grid_spec,
        out_shape=jax.ShapeDtypeStruct((R_ALL, D_MODEL), F32),
        input_output_aliases={2: 0},
        compiler_params=_cparams(1),
        name="moe_combine",
    )(pos, ys, x, mods)


def _moe(h32, logits, xs, mods, l, i, moe_w1, moe_w3, moe_w2):
    pos, src_token, w_sorted, tile_expert, n_used = _route(logits)
    group = (tile_expert, n_used)
    hs = _moe_gather(h32, src_token, n_used)
    g = _swiglu_up(hs, moe_w1, moe_w3, (i, 0), group=group)
    rs = jnp.broadcast_to(w_sorted[:, None], (MOE_P, LANES))
    ys = _mm([g], [(moe_w2, (i, 0))], [(rs, (MOE_TM, LANES), lambda j, i_: (i_, 0))],
             _moe_down_epilogue, n_out=D_MODEL, out_dtype=F32, tn=RS_TN, n_rows=MOE_P, tm=MOE_TM,
             group=group, name="moe_down")
    return _moe_combine(ys, pos, xs, mods, l, 5)


AT_TQ = 512
AT_TK = 1024


def _attn_body(lam_ref, q_ref, *rest, n_seg, post_scale):
    k_refs = rest[:n_seg]
    v_refs = rest[n_seg:2 * n_seg]
    sub_ref = rest[2 * n_seg]
    o_ref = rest[2 * n_seg + 1]
    lam = lam_ref[0]
    outs = []
    for s in range(2):
        sl = slice(s * HEAD_DIM, (s + 1) * HEAD_DIM)
        qs = q_ref[:, sl]
        m = den = num = None
        for k_ref, v_ref in zip(k_refs, v_refs):
            seg_len = k_ref.shape[0]
            for c0 in range(0, seg_len, AT_TK):
                c1 = min(c0 + AT_TK, seg_len)
                sc = lax.dot_general(qs, k_ref[c0:c1, sl], (((1,), (1,)), ((), ())),
                                     preferred_element_type=F32)
                mc = sc.max(axis=-1, keepdims=True)
                m_new = mc if m is None else jnp.maximum(m, mc)
                p = jnp.exp2(sc - m_new)
                ps = p.sum(axis=-1, keepdims=True)
                pv = jnp.dot(p.astype(BF16), v_ref[c0:c1, :], preferred_element_type=F32)
                if m is None:
                    den, num = ps, pv
                else:
                    alpha = jnp.exp2(m - m_new)
                    den = alpha * den + ps
                    num = alpha * num + pv
                m = m_new
        outs.append(num / den)
    d = outs[0] - lam * outs[1]
    ms = jnp.mean(d * d, axis=-1, keepdims=True)
    o_ref[...] = (d * lax.rsqrt(ms + EPS) * sub_ref[...] * post_scale).astype(BF16)


def _attention(p, lam, sub_norm, l, lam_init, ctx_only=False):
    kb = OFF_K // V_HEAD_DIM
    vb = OFF_V // V_HEAD_DIM
    ctx_blk0 = R_LAT // CTX_LEN
    if ctx_only:
        tq = CTX_LEN
        grid = (BATCH, N_DIFF_HEADS, 1)
        q_map = lambda b, h, t: (ctx_blk0 + b, h)
        o_map = lambda b, h, t: (b, h)
        segs = [CTX_LEN]
    else:
        tq = AT_TQ
        grid = (BATCH, N_DIFF_HEADS, SEQ // tq)
        q_map = o_map = lambda b, h, t: (b * (SEQ // tq) + t, h)
        segs = [CTX_LEN, SEQ]
    in_specs = [pl.BlockSpec(memory_space=pltpu.SMEM),
                pl.BlockSpec((tq, V_HEAD_DIM), q_map)]
    args = [lam.reshape(1), p]
    for off in (kb, vb):
        for seg_len in segs:
            if seg_len == CTX_LEN:
                imap = functools.partial(lambda b, h, t, o: (ctx_blk0 + b, o + h), o=off)
            else:
                imap = functools.partial(lambda b, h, t, o: (b, o + h), o=off)
            in_specs.append(pl.BlockSpec((seg_len, V_HEAD_DIM), imap))
            args.append(p)
    in_specs.append(pl.BlockSpec((None, 1, V_HEAD_DIM), lambda b, h, t: (l, 0, 0)))
    args.append(sub_norm.reshape(DEPTH, 1, V_HEAD_DIM))
    return pl.pallas_call(
        functools.partial(_attn_body, n_seg=len(segs), post_scale=1.0 - lam_init),
        grid=grid,
        in_specs=in_specs,
        out_specs=pl.BlockSpec((tq, V_HEAD_DIM), o_map),
        out_shape=jax.ShapeDtypeStruct((R_CTX if ctx_only else R_LAT, ATT_W), BF16),
        compiler_params=_cparams(3),
        name="attn_ctx" if ctx_only else "attn",
    )(*args)


def _dft_tables():
    n, n1, n2 = FFT_N, FFT_N1, FFT_N2
    k1 = np.arange(n1)[:, None]
    tabs_d, tabs_s, tabs_i = [], [], []
    for b in range(n2):
        cols = n2 * np.arange(n1)[None, :] + b
        g = np.exp(-2j * np.pi * ((k1 * cols) % n) / n)
        gd = g[:, :n1 // 2]
        tabs_d.append(np.block([[gd.real, -gd.imag], [gd.imag, gd.real]]))
        tabs_s.append(np.concatenate([g.real, g.imag], axis=0))
        gi = np.conj(gd).T / n1
        tabs_i.append(np.block([[gi.real, -gi.imag], [gi.imag, gi.real]]))
    kk = np.arange(n2)
    f2 = np.exp(-2j * np.pi * ((kk[:, None] * kk[None, :]) % n2) / n2)
    g2 = np.block([[f2.real, -f2.imag], [f2.imag, f2.real]])
    f2i = np.conj(f2) / n2
    g2i = np.block([[f2i.real, -f2i.imag], [f2i.imag, f2i.real]])
    as_bf = lambda a: jnp.asarray(np.asarray(a, np.float32)).astype(BF16)
    return (as_bf(np.stack(tabs_d)), as_bf(np.stack(tabs_s)), as_bf(np.stack(tabs_i)),
            as_bf(g2), as_bf(g2i))


FFT_UNROLL = 16


def _pack_pair(a, b):
    ab = lax.bitcast_convert_type(a.astype(BF16).astype(F32), jnp.uint32)
    bb = lax.bitcast_convert_type(b.astype(BF16).astype(F32), jnp.uint32)
    return lax.bitcast_convert_type(ab | (bb >> 16), F32)


def _unpack_pair(w):
    bits = lax.bitcast_convert_type(w, jnp.uint32)
    hi = lax.bitcast_convert_type(bits & jnp.uint32(0xFFFF0000), F32)
    lo = lax.bitcast_convert_type(bits << 16, F32)
    return hi, lo


def _spectrum_body(hf_ref, hr_ref, wf_ref, wb_ref, dl_ref, g1_ref, g2_ref, or_ref, oi_ref,
                   k_ref, a_ref):
    n1, n2 = FFT_N1, FFT_N2
    delta = dl_ref[...]
    step = 1.0 / (SEQ - 1)
    rows = 512
    row_iota = lax.broadcasted_iota(jnp.int32, (rows, 1), 0)

    def taps(c, carry):
        r0 = pl.multiple_of(c * rows, rows)
        pos = (row_iota + r0).astype(F32)
        top = jnp.dot(hf_ref[pl.ds(r0, rows), :].astype(BF16), wf_ref[...].astype(BF16),
                      preferred_element_type=F32)
        bot = jnp.dot(hr_ref[pl.ds(r0, rows), :].astype(BF16), wb_ref[...].astype(BF16),
                      preferred_element_type=F32)
        k_ref[pl.ds(r0, rows), :] = top * jnp.exp(-(pos * step) * delta)
        k_ref[pl.ds(SEQ + r0, rows), :] = bot * jnp.exp(-((SEQ - pos) * step) * delta)
        return carry

    lax.fori_loop(0, SEQ // rows, taps, 0)

    def stage1(b, carry):
        x = k_ref[pl.ds(b, n1, stride=n2), :].astype(BF16)
        a = jnp.dot(g1_ref[b], x, preferred_element_type=F32)
        row = pl.multiple_of(b * n1, n1)
        a_ref[pl.ds(row, n1), :] = _pack_pair(a[:n1], a[n1:])
        return carry

    lax.fori_loop(0, n2, stage1, 0, unroll=FFT_UNROLL)

    def stage2(k1, carry):
        a = jnp.concatenate(_unpack_pair(a_ref[pl.ds(k1, n2, stride=n1), :]), axis=0).astype(BF16)
        x = jnp.dot(g2_ref[...], a, preferred_element_type=F32)
        row = pl.multiple_of(k1 * n2, n2)
        or_ref[pl.ds(row, n2), :] = x[:n2].astype(BF16)
        oi_ref[pl.ds(row, n2), :] = x[n2:].astype(BF16)
        return carry

    lax.fori_loop(0, n1, stage2, 0, unroll=FFT_UNROLL)


def _const_spec(shape, n_grid):
    nd = len(shape)
    imap = (lambda a, b: (0,) * nd) if n_grid == 2 else (lambda a: (0,) * nd)
    return pl.BlockSpec(shape, imap, pipeline_mode=pl.Buffered(1))


def _spectrum(hid_f, hid_r, w4p, deltas, l, g1s, g2):
    tc = HY_TC
    nct = HY_W // tc
    hspec = pl.BlockSpec((SEQ, LANES), lambda f, ct: (0, 0))
    wspec = lambda d: pl.BlockSpec(
        (None, LANES, tc), functools.partial(lambda f, ct, d: (l, 0, (2 * f + d) * nct + ct), d=d))
    ospec = pl.BlockSpec((None, FFT_N, tc), lambda f, ct: (f, 0, ct))
    return pl.pallas_call(
        _spectrum_body,
        grid=(HY_N_FILT, nct),
        in_specs=[hspec, hspec, wspec(0), wspec(1),
                  pl.BlockSpec((1, tc), lambda f, ct: (0, ct)),
                  _const_spec((FFT_N2, 2 * FFT_N1, FFT_N1), 2),
                  _const_spec((2 * FFT_N2, 2 * FFT_N2), 2)],
        out_specs=[ospec, ospec],
        out_shape=[jax.ShapeDtypeStruct((HY_N_FILT, FFT_N, HY_W), BF16)] * 2,
        scratch_shapes=[pltpu.VMEM((FFT_N, tc), F32)] * 2,
        compiler_params=_cparams(2),
        name="hy_spectrum",
    )(hid_f, hid_r, w4p, w4p, deltas, g1s, g2)


def _conv3(u, w_ref, b_ref):
    n = u.shape[0]
    row = lax.broadcasted_iota(jnp.int32, u.shape, 0)
    prev = jnp.where(row == 0, 0.0, pltpu.roll(u, 1, 0))
    nxt = jnp.where(row == n - 1, 0.0, pltpu.roll(u, n - 1, 0))
    return prev * w_ref[0:1, :] + u * w_ref[1:2, :] + nxt * w_ref[2:3, :] + b_ref[...]


def _hyconv_body(s0_ref, s1_ref, m0_ref, m1_ref, cws_ref, cbs_ref, cwm_ref, cbm_ref, bias_ref,
                 kr_ref, ki_ref, g1_ref, g1i_ref, g2_ref, g2i_ref, o_ref,
                 vr_ref, vi_ref, vp_ref, a_ref, b_ref, *, conv_signal):
    n1, n2 = FFT_N1, FFT_N2
    h1 = n1 // 2
    for s_ref, v_ref in ((s0_ref, vr_ref), (s1_ref, vi_ref)):
        u = s_ref[...].astype(F32)
        v_ref[...] = _conv3(u, cws_ref, cbs_ref) if conv_signal else u
    vp_ref[...] = _pack_pair(vr_ref[...], vi_ref[...])

    def stage1(b, carry):
        x = jnp.concatenate(_unpack_pair(vp_ref[pl.ds(b, h1, stride=n2), :]), axis=0).astype(BF16)
        a = jnp.dot(g1_ref[b], x, preferred_element_type=F32)
        row = pl.multiple_of(b * n1, n1)
        a_ref[pl.ds(row, n1), :] = _pack_pair(a[:n1], a[n1:])
        return carry

    lax.fori_loop(0, n2, stage1, 0, unroll=FFT_UNROLL)

    def stage2(k1, carry):
        a = jnp.concatenate(_unpack_pair(a_ref[pl.ds(k1, n2, stride=n1), :]), axis=0).astype(BF16)
        x = jnp.dot(g2_ref[...], a, preferred_element_type=F32)
        row = pl.multiple_of(k1 * n2, n2)
        fr = kr_ref[pl.ds(row, n2), :].astype(F32)
        fi = ki_ref[pl.ds(row, n2), :].astype(F32)
        xr, xi = x[:n2], x[n2:]
        y = jnp.concatenate([xr * fr - xi * fi, xr * fi + xi * fr], axis=0).astype(BF16)
        bq = jnp.dot(g2i_ref[...], y, preferred_element_type=F32)
        b_ref[pl.ds(row, n2), :] = _pack_pair(bq[:n2], bq[n2:])
        return carry

    lax.fori_loop(0, n1, stage2, 0, unroll=FFT_UNROLL)

    def stage1_inv(b, carry):
        bq = jnp.concatenate(_unpack_pair(b_ref[pl.ds(b, n1, stride=n2), :]), axis=0).astype(BF16)
        y = jnp.dot(g1i_ref[b], bq, preferred_element_type=F32)
        row = pl.multiple_of(b * h1, h1)
        a_ref[pl.ds(row, h1), :] = _pack_pair(y[:h1], y[h1:])
        return carry

    lax.fori_loop(0, n2, stage1_inv, 0, unroll=FFT_UNROLL)

    bias = bias_ref[...]

    def unpermute(i, carry):
        row = pl.multiple_of(i * n2, n2)
        yr, yi = _unpack_pair(a_ref[pl.ds(i, n2, stride=h1), :])
        vr_ref[pl.ds(row, n2), :] = yr + bias * vr_ref[pl.ds(row, n2), :]
        vi_ref[pl.ds(row, n2), :] = yi + bias * vi_ref[pl.ds(row, n2), :]
        return carry

    lax.fori_loop(0, h1, unpermute, 0, unroll=FFT_UNROLL)

    for half, (m_ref, v_ref) in enumerate(((m0_ref, vr_ref), (m1_ref, vi_ref))):
        mult = _conv3(m_ref[...].astype(F32), cwm_ref, cbm_ref)
        o_ref[half * SEQ:(half + 1) * SEQ, :] = (mult * v_ref[...]).astype(o_ref.dtype)


def _hy_param_specs(conv_w, conv_b, bias, l, filt, sig_ch, mul_ch, tc):
    cw3 = conv_w.reshape(DEPTH, 3, 3 * HY_W)
    cb3 = conv_b.reshape(DEPTH, 1, 3 * HY_W)
    chan = lambda ch0, rows: pl.BlockSpec(
        (None, rows, tc), functools.partial(lambda ct, pr, c0: (l, 0, c0 + ct), c0=ch0))
    specs = [chan(sig_ch, 3), chan(sig_ch, 1), chan(mul_ch, 3), chan(mul_ch, 1),
             pl.BlockSpec((None, None, 1, tc), lambda ct, pr: (l, filt, 0, ct))]
    args = [cw3, cb3, cw3, cb3, bias.reshape(DEPTH, HY_N_FILT, 1, HY_W)]
    return specs, args


def _hyconv(sig, sig_cb, mul, mul_cb, conv_w, conv_b, bias, l, filt, sig_ch, mul_ch,
            kr, ki, tabs, conv_signal):
    g1d, g1i, g2, g2i = tabs
    tc = HY_TC
    blk = lambda arr_cb, odd: pl.BlockSpec(
        (SEQ, tc), functools.partial(lambda ct, pr, cb, o: (2 * pr + o, cb + ct), cb=arr_cb, o=odd))
    pspecs, pargs = _hy_param_specs(conv_w, conv_b, bias, l, filt, sig_ch, mul_ch, tc)
    kspec = pl.BlockSpec((None, FFT_N, tc), lambda ct, pr: (filt, 0, ct))
    in_specs = [blk(sig_cb, 0), blk(sig_cb, 1), blk(mul_cb, 0), blk(mul_cb, 1)] + pspecs + [
        kspec, kspec,
        _const_spec((FFT_N2, 2 * FFT_N1, FFT_N1), 2),
        _const_spec((FFT_N2, FFT_N1, 2 * FFT_N1), 2),
        _const_spec((2 * FFT_N2, 2 * FFT_N2), 2),
        _const_spec((2 * FFT_N2, 2 * FFT_N2), 2),
    ]
    return pl.pallas_call(
        functools.partial(_hyconv_body, conv_signal=conv_signal),
        grid=(HY_W // tc, BATCH // 2),
        in_specs=in_specs,
        out_specs=pl.BlockSpec((2 * SEQ, tc), lambda ct, pr: (pr, ct)),
        out_shape=jax.ShapeDtypeStruct((R_LAT, HY_W), BF16),
        scratch_shapes=[pltpu.VMEM((SEQ, tc), F32)] * 3 + [pltpu.VMEM((FFT_N, tc), F32)] * 2,
        compiler_params=_cparams(2),
        name="hyconv",
    )(sig, sig, mul, mul, *pargs, kr, ki, g1d, g1i, g2, g2i)


def _ctx_dft_tables():
    n = 2 * CTX_LEN
    kk = np.arange(n)
    f = np.exp(-2j * np.pi * ((kk[:, None] * kk[None, :]) % n) / n)
    fd = f[:, :CTX_LEN]
    fwd = np.block([[fd.real, -fd.imag], [fd.imag, fd.real]])
    spec = np.concatenate([f.real, f.imag], axis=0)
    fi = np.conj(f)[:CTX_LEN, :] / n
    inv = np.block([[fi.real, -fi.imag], [fi.imag, fi.real]])
    as_bf = lambda a: jnp.asarray(np.asarray(a, np.float32)).astype(BF16)
    return as_bf(fwd), as_bf(spec), as_bf(inv)


def _hyconv_ctx_body(s0_ref, s1_ref, m0_ref, m1_ref, cws_ref, cbs_ref, cwm_ref, cbm_ref, bias_ref,
                     k_ref, fwd_ref, spec_ref, inv_ref, o_ref, *, conv_signal):
    n = 2 * CTX_LEN
    vs = []
    for s_ref in (s0_ref, s1_ref):
        u = s_ref[...].astype(F32)
        vs.append(_conv3(u, cws_ref, cbs_ref) if conv_signal else u)
    x = jnp.dot(fwd_ref[...], jnp.concatenate(vs, axis=0).astype(BF16), preferred_element_type=F32)
    kf = jnp.dot(spec_ref[...], k_ref[...].astype(BF16), preferred_element_type=F32)
    kf = kf.astype(BF16).astype(F32)
    xr, xi, fr, fi = x[:n], x[n:], kf[:n], kf[n:]
    y = jnp.concatenate([xr * fr - xi * fi, xr * fi + xi * fr], axis=0).astype(BF16)
    conv = jnp.dot(inv_ref[...], y, preferred_element_type=F32)
    bias = bias_ref[...]
    for half, m_ref in enumerate((m0_ref, m1_ref)):
        sl = slice(half * CTX_LEN, (half + 1) * CTX_LEN)
        mult = _conv3(m_ref[...].astype(F32), cwm_ref, cbm_ref)
        o_ref[sl, :] = (mult * (conv[sl] + bias * vs[half])).astype(o_ref.dtype)


def _hyconv_ctx(sig, sig_rb, sig_cb, mul, mul_rb, mul_cb, conv_w, conv_b, bias, l, filt,
                sig_ch, mul_ch, kctx, tabs, conv_signal):
    fwd, spec, inv = tabs
    tc = HY_TC
    blk = lambda rb, arr_cb, odd: pl.BlockSpec(
        (CTX_LEN, tc),
        functools.partial(lambda ct, pr, rb, cb, o: (rb + 2 * pr + o, cb + ct), rb=rb, cb=arr_cb, o=odd))
    pspecs, pargs = _hy_param_specs(conv_w, conv_b, bias, l, filt, sig_ch, mul_ch, tc)
    n = 2 * CTX_LEN
    in_specs = [blk(sig_rb, sig_cb, 0), blk(sig_rb, sig_cb, 1),
                blk(mul_rb, mul_cb, 0), blk(mul_rb, mul_cb, 1)] + pspecs + [
        pl.BlockSpec((None, n, tc), lambda ct, pr: (filt, 0, ct)),
        _const_spec((2 * n, n), 2),
        _const_spec((2 * n, n), 2),
        _const_spec((n, 2 * n), 2),
    ]
    return pl.pallas_call(
        functools.partial(_hyconv_ctx_body, conv_signal=conv_signal),
        grid=(HY_W // tc, BATCH // 2),
        in_specs=in_specs,
        out_specs=pl.BlockSpec((2 * CTX_LEN, tc), lambda ct, pr: (pr, ct)),
        out_shape=jax.ShapeDtypeStruct((R_CTX, HY_W), BF16),
        compiler_params=_cparams(2),
        name="hyconv_ctx",
    )(sig, sig, mul, mul, *pargs, kctx, fwd, spec, inv)


def _rope_tables():
    rows = SEQ // GRID_W
    row = jnp.repeat(jnp.arange(rows), GRID_W).astype(F32)
    col = jnp.tile(jnp.arange(GRID_W), rows).astype(F32)
    inv = ROPE_BASE ** (-jnp.arange(ROT_FREQS, dtype=F32) / ROT_FREQS)
    ar, ac = row[:, None] * inv, col[:, None] * inv
    cos = jnp.concatenate([jnp.cos(ar), jnp.cos(ar), jnp.cos(ac), jnp.cos(ac)], axis=1)
    sin = jnp.concatenate([-jnp.sin(ar), jnp.sin(ar), -jnp.sin(ac), jnp.sin(ac)], axis=1)
    cos = jnp.concatenate([cos, jnp.ones((TM, HEAD_DIM), F32)], axis=0)
    sin = jnp.concatenate([sin, jnp.zeros((TM, HEAD_DIM), F32)], axis=0)
    return cos, sin


def _hyena_hidden(L, w1, b1, w2, b2, w3, b3, freq):
    t = jnp.linspace(0.0, 1.0, L, dtype=F32)[:, None]
    w = 2.0 * math.pi * jnp.arange(L, dtype=F32)[:, None] / L
    f = jnp.linspace(1e-4, HY_BANDS - 1, HY_BANDS, dtype=F32)
    z = jnp.concatenate([t, jnp.cos(w * f), -jnp.sin(w * f)], axis=-1)
    dot = functools.partial(jnp.dot, precision=HIGHEST)
    hid = jnp.sin(freq * (dot(z, w1) + b1))
    hid = jnp.sin(freq * (dot(hid, w2) + b2))
    return jnp.sin(freq * (dot(hid, w3) + b3))


def _hyena_deltas():
    return jnp.abs(jnp.linspace(math.log(HY_TARGET) / HY_SLOW_PCT,
                                math.log(HY_TARGET) / HY_FAST_PCT, HY_W, dtype=F32))


def _hyena_filters_ctx(hid, w4):
    L = hid.shape[0]
    t = jnp.linspace(0.0, 1.0, L, dtype=F32)[:, None]
    h = jnp.dot(hid, w4, precision=HIGHEST).reshape(L, HY_N_FILT, 2, HY_W)
    h = h * jnp.exp(-t * _hyena_deltas())[:, None, None, :]
    k = jnp.concatenate([h[:, :, 0], jnp.zeros((1, HY_N_FILT, HY_W), F32), h[:0:-1, :, 1]], axis=0)
    return k.transpose(1, 0, 2)


def kernel(x, c, ctx, c_ctx, w_ada, b_ada, norm1, norm2, w_in, b_gate, q_norm, k_norm, lam_q1, lam_k1, lam_q2, lam_k2, sub_norm, hy_conv_w, hy_conv_b, hy_w1, hy_b1, hy_w2, hy_b2, hy_w3, hy_b3, hy_w4, hy_freq, hy_bias, w_br_a, w_br_b, w_out, ffn_w1, ffn_w3, ffn_w2, router, moe_w1, moe_w3, moe_w2):
    xs = jnp.concatenate([x.reshape(R_LAT, D_MODEL), ctx.reshape(R_CTX, D_MODEL)], axis=0)
    cs = jnp.concatenate([c, c_ctx[None], jnp.zeros((8 - BATCH - 1, D_MODEL), F32)], axis=0)
    mods = _adaln(cs, w_ada, b_ada)
    cos_t, sin_t = _rope_tables()
    g1d, g1s, g1i, g2, g2i = _dft_tables()
    ctx_tabs = _ctx_dft_tables()
    gains = jnp.zeros((DEPTH, 8, HEAD_DIM), F32)
    gains = gains.at[:, 0].set(q_norm * (math.log2(math.e) / math.sqrt(HEAD_DIM))).at[:, 1].set(k_norm)
    router_pad = jnp.zeros((router.shape[0], D_MODEL, LANES), F32).at[:, :, :N_EXPERTS].set(router)
    hy_cb = OFF_HY // HY_TC
    ch = HY_W // HY_TC
    w4p = jnp.pad(hy_w4, ((0, 0), (0, LANES - HY_FH), (0, 0)))
    deltas = _hyena_deltas()[None]

    for l in range(DEPTH):
        lam_init = 0.8 - 0.6 * math.exp(-0.3 * l)
        lam = (jnp.exp(jnp.sum(lam_q1[l] * lam_k1[l])) - jnp.exp(jnp.sum(lam_q2[l] * lam_k2[l]))
               + lam_init)
        h = _normmod(xs, norm1, mods, l, 0)
        p = _inproj(h, w_in, l, gains, cos_t, sin_t, b_gate)

        ya = (_attention(p, lam, sub_norm, l, lam_init),
              _attention(p, lam, sub_norm, l, lam_init, ctx_only=True))

        hp = (hy_w1[l], hy_b1[l], hy_w2[l], hy_b2[l], hy_w3[l], hy_b3[l], hy_freq[l])
        hid = jnp.pad(_hyena_hidden(SEQ, *hp), ((0, 0), (0, LANES - HY_FH)))
        hid_rev = jnp.concatenate([jnp.zeros((1, LANES), F32), hid[:0:-1]], axis=0)
        kr, ki = _spectrum(hid, hid_rev, w4p, deltas, l, g1s, g2)
        kctx = _hyena_filters_ctx(_hyena_hidden(CTX_LEN, *hp), hy_w4[l])
        common = (hy_conv_w, hy_conv_b, hy_bias, l)
        tabs = (g1d, g1i, g2, g2i)
        crb = R_LAT // CTX_LEN
        z = _hyconv(p, hy_cb, p, hy_cb + ch, *common, 0, 0, ch, kr, ki, tabs, True)
        zc = _hyconv_ctx(p, crb, hy_cb, p, crb, hy_cb + ch, *common, 0, 0, ch, kctx, ctx_tabs, True)
        yb = (_hyconv(z, 0, p, hy_cb + 2 * ch, *common, 1, 0, 2 * ch, kr, ki, tabs, False),
              _hyconv_ctx(zc, 0, 0, p, crb, hy_cb + 2 * ch, *common, 1, 0, 2 * ch, kctx, ctx_tabs, False))

        m = _merge(ya, yb, p, w_br_a, w_br_b, l)
        xs = _resid_proj(m, w_out, (l,), xs, mods, l, 2, name="out_proj")

        i = l // 2
        if l % 2 == 0:
            h2 = _normmod(xs, norm2, mods, l, 3)
            g = _swiglu_up(h2, ffn_w1, ffn_w3, (i,))
            xs = _resid_proj(g, ffn_w2, (i,), xs, mods, l, 5, tm=FFD_TM, name="ffn_down")
        else:
            h32, logits = _normmod(xs, norm2, mods, l, 3, router=router_pad[i])
            xs = _moe(h32, logits, xs, mods, l, i, moe_w1, moe_w3, moe_w2)
    return xs[:R_LAT].reshape(BATCH, SEQ, D_MODEL)
```

```python
import functools
import math

import numpy as np
import jax
import jax.numpy as jnp
from jax import lax
from jax.experimental import pallas as pl
from jax.experimental.pallas import tpu as pltpu

D_MODEL = 2048
BATCH = 4
SEQ = 4096
DEPTH = 4
CTX_LEN = 256
GRID_W = 64
HEAD_DIM = 128
N_DIFF_HEADS = D_MODEL // (2 * HEAD_DIM)
V_HEAD_DIM = 2 * HEAD_DIM
ATT_W = N_DIFF_HEADS * V_HEAD_DIM
HY_W = D_MODEL
OFF_K = ATT_W
OFF_V = 2 * ATT_W
OFF_HY = 3 * ATT_W
OFF_G = 3 * ATT_W + 3 * HY_W
IN_COLS = OFF_G + 2 * D_MODEL
ROPE_BASE = 10000.0
ROT_FREQS = HEAD_DIM // 4
HY_EMB = 33
HY_BANDS = (HY_EMB - 1) // 2
HY_FH = 64
HY_N_FILT = 2
HY_TARGET = 1e-2
HY_FAST_PCT = 0.3
HY_SLOW_PCT = 1.5
D_FF = 5632
N_EXPERTS = 8
TOP_K = 2
EPS = 1e-6

R_LAT = BATCH * SEQ
R_CTX = BATCH * CTX_LEN
R_ALL = R_LAT + R_CTX
TM = 1024
CTX_TILE = R_LAT // TM
CTX_SEG = BATCH
TR = 256
LANES = 128
VMEM_LIMIT = 56 * 1024 * 1024

FFT_N = 2 * SEQ
FFT_N1 = 128
FFT_N2 = 64
HY_TC = 128

F32 = jnp.float32
BF16 = jnp.bfloat16
HIGHEST = lax.Precision.HIGHEST


def _cparams(n_axes):
    return pltpu.CompilerParams(
        dimension_semantics=("arbitrary",) * n_axes, vmem_limit_bytes=VMEM_LIMIT)


def _row_seg(i, tm=TM):
    return jnp.where(i >= R_LAT // tm, CTX_SEG, i // (SEQ // tm))


def _adaln_body(c_ref, w_ref, b_ref, o_ref):
    s = c_ref[...]
    s = s * jax.nn.sigmoid(s)
    o_ref[...] = jnp.dot(s, w_ref[...], preferred_element_type=F32, precision=HIGHEST) + b_ref[...]


def _adaln(cs, w_ada, b_ada):
    tn = 1536
    n = 6 * D_MODEL
    return pl.pallas_call(
        _adaln_body,
        grid=(DEPTH, n // tn),
        in_specs=[
            pl.BlockSpec((8, D_MODEL), lambda l, j: (0, 0)),
            pl.BlockSpec((None, D_MODEL, tn), lambda l, j: (l, 0, j)),
            pl.BlockSpec((None, 1, tn), lambda l, j: (l, 0, j)),
        ],
        out_specs=pl.BlockSpec((None, 8, tn), lambda l, j: (l, 0, j)),
        out_shape=jax.ShapeDtypeStruct((DEPTH, 8, n), F32),
        compiler_params=_cparams(2),
        name="adaln",
    )(cs, w_ada, b_ada.reshape(DEPTH, 1, n))


def _normmod_body(x_ref, g_ref, sh_ref, sc_ref, *rest, with_router):
    i = pl.program_id(0)
    seg = jnp.where(i >= R_LAT // TR, CTX_SEG, i // (SEQ // TR))
    x = x_ref[...]
    ms = jnp.mean(x * x, axis=-1, keepdims=True)
    y = x * lax.rsqrt(ms + EPS) * g_ref[...]
    h = y * (1.0 + sc_ref[pl.ds(seg, 1), :]) + sh_ref[pl.ds(seg, 1), :]
    if with_router:
        r_ref, o_ref, lg_ref = rest
        lg_ref[...] = jnp.dot(h, r_ref[...], preferred_element_type=F32, precision=HIGHEST)
    else:
        (o_ref,) = rest
    o_ref[...] = h.astype(o_ref.dtype)


def _normmod(x, gain, mods, l, which, router=None):
    in_specs = [
        pl.BlockSpec((TR, D_MODEL), lambda i: (i, 0)),
        pl.BlockSpec((None, 1, D_MODEL), lambda i: (l, 0, 0)),
        pl.BlockSpec((None, 8, D_MODEL), lambda i: (l, 0, which)),
        pl.BlockSpec((None, 8, D_MODEL), lambda i: (l, 0, which + 1)),
    ]
    args = [x, gain.reshape(DEPTH, 1, D_MODEL), mods, mods]
    out_specs = pl.BlockSpec((TR, D_MODEL), lambda i: (i, 0))
    out_shape = jax.ShapeDtypeStruct((R_ALL, D_MODEL), BF16 if router is None else F32)
    if router is not None:
        in_specs.append(pl.BlockSpec((D_MODEL, LANES), lambda i: (0, 0)))
        args.append(router)
        out_specs = [out_specs, pl.BlockSpec((TR, LANES), lambda i: (i, 0))]
        out_shape = [out_shape, jax.ShapeDtypeStruct((R_ALL, LANES), F32)]
    return pl.pallas_call(
        functools.partial(_normmod_body, with_router=router is not None),
        grid=(R_ALL // TR,),
        in_specs=in_specs,
        out_specs=out_specs,
        out_shape=out_shape,
        compiler_params=_cparams(1),
        name="normmod",
    )(*args)


def _mm_body(*refs, n_pref, n_a, n_w, n_aux, epilogue, a_of_w, split_ctx):
    pref = refs[:n_pref]
    refs = refs[n_pref:]
    if split_ctx:
        a_refs, a_ctx_refs = refs[:2 * n_a:2], refs[1:2 * n_a:2]
        refs = refs[n_a:]
    else:
        a_refs = a_ctx_refs = refs[:n_a]
    w_refs = refs[n_a:n_a + n_w]
    aux_refs = refs[n_a + n_w:n_a + n_w + n_aux]
    o_ref = refs[n_a + n_w + n_aux]
    wbf_refs = refs[n_a + n_w + n_aux + 1:n_a + 2 * n_w + n_aux + 1]
    aux_refs = tuple(aux_refs) + tuple(refs[n_a + 2 * n_w + n_aux + 1:])
    j = pl.program_id(0)
    i = pl.program_id(1)
    if n_pref:
        te_ref, nu_ref = pref
        new_weights = (i == 0) | (te_ref[i] != te_ref[jnp.maximum(i - 1, 0)])
        valid = i < nu_ref[0]
    else:
        new_weights = i == 0
        valid = None

    @pl.when(new_weights)
    def _():
        for w_ref, wbf_ref in zip(w_refs, wbf_refs):
            wbf_ref[...] = w_ref[...].astype(BF16)

    def compute(srcs):
        accs = [jnp.dot(srcs[a_of_w[k]][...], wbf_refs[k][...], preferred_element_type=F32)
                for k in range(n_w)]
        epilogue(accs, aux_refs, o_ref, j, i)

    if split_ctx:
        pl.when(i < CTX_TILE)(functools.partial(compute, a_refs))
        pl.when(i >= CTX_TILE)(functools.partial(compute, a_ctx_refs))
    elif valid is None:
        compute(a_refs)
    else:
        pl.when(valid)(functools.partial(compute, a_refs))

        @pl.when(jnp.logical_not(valid))
        def _():
            o_ref[...] = jnp.zeros(o_ref.shape, o_ref.dtype)


def _mm(a_list, w_list, aux_list, epilogue, *, n_out, out_dtype, tn, a_of_w=None,
        n_rows=R_ALL, tm=TM, alias_aux=None, group=None, split_ctx=False, w_col0=0,
        epilogue_scratch=(), name="mm"):
    n_a, n_w, n_aux = len(a_list), len(w_list), len(aux_list)
    n_pref = 0 if group is None else 2
    a_of_w = tuple(a_of_w) if a_of_w is not None else tuple(range(n_w))
    in_specs, args, scratch = [], [], []
    for a in a_list:
        if split_ctx:
            lat, ctx = a
            assert tm == R_CTX and group is None
            in_specs.append(pl.BlockSpec((tm, lat.shape[1]),
                                         lambda j, i: (jnp.minimum(i, CTX_TILE - 1), 0)))
            in_specs.append(pl.BlockSpec((tm, ctx.shape[1]), lambda j, i: (0, 0),
                                         pipeline_mode=pl.Buffered(1)))
            args += [lat, ctx]
        else:
            in_specs.append(pl.BlockSpec((tm, a.shape[1]), lambda j, i, *_: (i, 0)))
            args.append(a)
    for w, prefix in w_list:
        k = w.shape[-2]
        if group is None:
            wmap = functools.partial(lambda j, i, p: p + (0, w_col0 + j), p=tuple(prefix))
        else:
            wmap = functools.partial(lambda j, i, te, nu, p: p[:-1] + (te[i], 0, j), p=tuple(prefix))
        in_specs.append(pl.BlockSpec((None,) * len(prefix) + (k, tn), wmap))
        args.append(w)
        scratch.append(pltpu.VMEM((k, tn), BF16))
    for arr, block, imap in aux_list:
        in_specs.append(pl.BlockSpec(block, functools.partial(lambda j, i, *_, f: f(j, i), f=imap)))
        args.append(arr)
    aliases = {}
    if alias_aux is not None:
        aliases = {n_pref + (2 * n_a if split_ctx else n_a) + n_w + alias_aux: 0}
    grid_spec = pltpu.PrefetchScalarGridSpec(
        num_scalar_prefetch=n_pref,
        grid=(n_out // tn, n_rows // tm),
        in_specs=in_specs,
        out_specs=pl.BlockSpec((tm, tn), lambda j, i, *_: (i, j)),
        scratch_shapes=scratch + list(epilogue_scratch),
    )
    return pl.pallas_call(
        functools.partial(_mm_body, n_pref=n_pref, n_a=n_a, n_w=n_w, n_aux=n_aux,
                          epilogue=epilogue, a_of_w=a_of_w, split_ctx=split_ctx),
        grid_spec=grid_spec,
        out_shape=jax.ShapeDtypeStruct((n_rows, n_out), out_dtype),
        input_output_aliases=aliases,
        compiler_params=_cparams(2),
        name=name,
    )(*(list(group) if group is not None else []), *args)


IN_TN = 512
IN_TN_WIDE = 1024

QK_ROWS = 512


def _qk_epilogue(accs, aux, o_ref, j, i):
    gains_ref, cos_ref, sin_ref, acc_ref = aux
    acc_ref[...] = accs[0]
    gain = gains_ref[pl.ds(j // (OFF_K // IN_TN), 1), :]
    lane = lax.broadcasted_iota(jnp.int32, (1, HEAD_DIM), 1)
    first_half = (lane % (2 * ROT_FREQS)) < ROT_FREQS

    def rows(r, carry):
        r0 = pl.multiple_of(r * QK_ROWS, QK_ROWS)
        cos = cos_ref[pl.ds(r0, QK_ROWS), :]
        sin = sin_ref[pl.ds(r0, QK_ROWS), :]
        for h in range(IN_TN // HEAD_DIM):
            xh = acc_ref[pl.ds(r0, QK_ROWS), h * HEAD_DIM:(h + 1) * HEAD_DIM]
            ms = jnp.mean(xh * xh, axis=-1, keepdims=True)
            y = xh * lax.rsqrt(ms + EPS) * gain
            partner = jnp.where(first_half,
                                pltpu.roll(y, HEAD_DIM - ROT_FREQS, 1),
                                pltpu.roll(y, ROT_FREQS, 1))
            o_ref[pl.ds(r0, QK_ROWS), h * HEAD_DIM:(h + 1) * HEAD_DIM] = (
                y * cos + partner * sin).astype(BF16)
        return carry

    lax.fori_loop(0, TM // QK_ROWS, rows, 0)


def _cast_epilogue(accs, aux, o_ref, j, i):
    o_ref[...] = accs[0].astype(BF16)


def _gate_epilogue(accs, aux, o_ref, j, i):
    o_ref[...] = jax.nn.sigmoid(accs[0] + aux[0][...]).astype(BF16)


def _inproj(h, w_in, l, gains, cos_t, sin_t, b_gate):
    def rope_map(j, i):
        return (jnp.where(i >= CTX_TILE, SEQ // TM, i % (SEQ // TM)), 0)

    w = [(w_in, (l,))]
    qk = _mm([h], w, [(gains, (None, 8, HEAD_DIM), lambda j, i: (l, 0, 0)),
                      (cos_t, (TM, HEAD_DIM), rope_map), (sin_t, (TM, HEAD_DIM), rope_map)],
             _qk_epilogue, n_out=OFF_V, out_dtype=BF16, tn=IN_TN,
             epilogue_scratch=[pltpu.VMEM((TM, IN_TN), F32)], name="inproj_qk")
    vu = _mm([h], w, [], _cast_epilogue, n_out=OFF_G - OFF_V, out_dtype=BF16, tn=IN_TN_WIDE,
             w_col0=OFF_V // IN_TN_WIDE, name="inproj_vu")
    gates = _mm([h], w, [(b_gate.reshape(DEPTH, 1, 2 * D_MODEL), (None, 1, IN_TN_WIDE),
                          lambda j, i: (l, 0, j))],
                _gate_epilogue, n_out=2 * D_MODEL, out_dtype=BF16, tn=IN_TN_WIDE,
                w_col0=OFF_G // IN_TN_WIDE, name="inproj_gate")
    return qk, vu, gates


MG_TN = 512


def _merge_epilogue(accs, aux, o_ref, j, i):
    ga_ref, gb_ref = aux
    o_ref[...] = (ga_ref[...].astype(F32) * accs[0] + gb_ref[...].astype(F32) * accs[1]).astype(BF16)


def _merge(ya, yb, gates, w_br_a, w_br_b, l):
    aux = [
        (gates, (TM, MG_TN), lambda j, i: (i, j)),
        (gates, (TM, MG_TN), lambda j, i: (i, D_MODEL // MG_TN + j)),
    ]
    return _mm([ya, yb], [(w_br_a, (l,)), (w_br_b, (l,))], aux, _merge_epilogue,
               n_out=D_MODEL, out_dtype=BF16, tn=MG_TN, split_ctx=True, name="merge")


RS_TN = 512
FFD_TM = 512


def _resid_epilogue(accs, aux, o_ref, j, i, tm):
    x_ref, g_ref = aux
    o_ref[...] = x_ref[...] + g_ref[pl.ds(_row_seg(i, tm), 1), :] * accs[0]


def _resid_proj(a, w, prefix, x, mods, l, which, tm=TM, name="resid"):
    nb = D_MODEL // RS_TN
    aux = [
        (x, (tm, RS_TN), lambda j, i: (i, j)),
        (mods, (None, 8, RS_TN), lambda j, i: (l, 0, which * nb + j)),
    ]
    return _mm([a], [(w, prefix)], aux, functools.partial(_resid_epilogue, tm=tm),
               n_out=D_MODEL, out_dtype=F32, tm=tm, tn=RS_TN, alias_aux=0, name=name)


FF_TN = 512


def _swiglu_epilogue(accs, aux, o_ref, j, i):
    a = accs[0]
    o_ref[...] = (a * jax.nn.sigmoid(a) * accs[1]).astype(BF16)


def _swiglu_up(h, w1, w3, prefix, group=None):
    kw = dict(n_rows=MOE_P, tm=MOE_TM) if group is not None else {}
    return _mm([h], [(w1, prefix), (w3, prefix)], [], _swiglu_epilogue, n_out=D_FF,
               out_dtype=BF16, tn=FF_TN, a_of_w=(0, 0), group=group, name="swiglu_up", **kw)


MOE_TM = 512
MOE_A = TOP_K * R_ALL
MOE_NT = MOE_A // MOE_TM + N_EXPERTS
MOE_P = MOE_NT * MOE_TM
GATHER_ROWS = 256


def _route(logits):
    top_v, top_i = lax.top_k(logits[:, :N_EXPERTS], TOP_K)
    wts = jax.nn.softmax(top_v, axis=-1).reshape(-1)
    e_flat = top_i.reshape(-1).astype(jnp.int32)
    onehot = (e_flat[:, None] == jnp.arange(N_EXPERTS, dtype=jnp.int32)[None, :]).astype(jnp.int32)
    before = jnp.cumsum(onehot, axis=0) - onehot
    rank = jnp.sum(before * onehot, axis=1)
    counts = jnp.sum(onehot, axis=0)
    tiles = (counts + MOE_TM - 1) // MOE_TM
    tile_end = jnp.cumsum(tiles)
    pos = ((tile_end - tiles)[e_flat] * MOE_TM + rank).astype(jnp.int32)
    n_used = tile_end[-1:].astype(jnp.int32)
    tile_ids = jnp.minimum(jnp.arange(MOE_NT, dtype=jnp.int32), n_used[0] - 1)
    tile_expert = jnp.sum((tile_ids[:, None] >= tile_end[None, :]).astype(jnp.int32), axis=1)
    token = jnp.arange(MOE_A, dtype=jnp.int32) // TOP_K
    src_token = jnp.zeros((MOE_P,), jnp.int32).at[pos].set(token)
    return pos, src_token, wts, tile_expert.astype(jnp.int32), n_used


def _gather_body(src_ref, nu_ref, h_ref, o_ref, buf_ref, sem):
    t = pl.program_id(0)
    n_rows = nu_ref[0] * MOE_TM
    slot = t % 2

    def row_copy(slot_, r, src_row):
        return pltpu.make_async_copy(h_ref.at[pl.ds(src_row, 1), :],
                                     buf_ref.at[slot_, pl.ds(r, 1), :], sem.at[slot_])

    def issue(step, slot_):
        def body(r, carry):
            row_copy(slot_, r, src_ref[step * GATHER_ROWS + r]).start()
            return carry

        lax.fori_loop(0, GATHER_ROWS, body, 0, unroll=8)

    @pl.when((t == 0) & (n_rows > 0))
    def _():
        issue(0, 0)

    @pl.when((t + 1 < pl.num_programs(0)) & ((t + 1) * GATHER_ROWS < n_rows))
    def _():
        issue(t + 1, 1 - slot)

    @pl.when(t * GATHER_ROWS < n_rows)
    def _():
        def drain(r, carry):
            row_copy(slot, r, 0).wait()
            return carry

        lax.fori_loop(0, GATHER_ROWS, drain, 0, unroll=8)
        o_ref[...] = buf_ref[slot].astype(BF16)

    @pl.when(t * GATHER_ROWS >= n_rows)
    def _():
        o_ref[...] = jnp.zeros(o_ref.shape, o_ref.dtype)


def _moe_gather(h32, src_token, n_used):
    grid_spec = pltpu.PrefetchScalarGridSpec(
        num_scalar_prefetch=2,
        grid=(MOE_P // GATHER_ROWS,),
        in_specs=[pl.BlockSpec(memory_space=pl.ANY)],
        out_specs=pl.BlockSpec((GATHER_ROWS, D_MODEL), lambda t, *_: (t, 0)),
        scratch_shapes=[pltpu.VMEM((2, GATHER_ROWS, D_MODEL), F32), pltpu.SemaphoreType.DMA((2,))],
    )
    return pl.pallas_call(
        _gather_body,
        grid_spec=grid_spec,
        out_shape=jax.ShapeDtypeStruct((MOE_P, D_MODEL), BF16),
        compiler_params=_cparams(1),
        name="moe_gather",
    )(src_token, n_used, h32)


def _moe_down_epilogue(accs, aux, o_ref, j, i):
    o_ref[...] = accs[0]


def _combine_body(pos_ref, y_ref, w_ref, x_ref, g_ref, o_ref, buf_ref, sem):
    t = pl.program_id(0)
    slot = t % 2

    def row_copy(slot_, k, r, src_row):
        return pltpu.make_async_copy(y_ref.at[pl.ds(src_row, 1), :],
                                     buf_ref.at[slot_, k, pl.ds(r, 1), :], sem.at[slot_])

    def issue(step, slot_):
        def body(r, carry):
            for k in range(TOP_K):
                row_copy(slot_, k, r, pos_ref[TOP_K * (step * TR + r) + k]).start()
            return carry

        lax.fori_loop(0, TR, body, 0, unroll=4)

    @pl.when(t == 0)
    def _():
        issue(0, 0)

    @pl.when(t + 1 < pl.num_programs(0))
    def _():
        issue(t + 1, 1 - slot)

    def drain(r, carry):
        for k in range(TOP_K):
            row_copy(slot, k, r, 0).wait()
        return carry

    lax.fori_loop(0, TR, drain, 0, unroll=4)
    g = g_ref[pl.ds(_row_seg(t, TR), 1), :]
    y = w_ref[0] * buf_ref[slot, 0]
    for k in range(1, TOP_K):
        y = y + w_ref[k] * buf_ref[slot, k]
    o_ref[...] = x_ref[...] + g * y


def _moe_combine(ys, pos, wts, x, mods, l, which):
    w_cols = jnp.broadcast_to(wts.reshape(R_ALL, TOP_K).T[:, :, None], (TOP_K, R_ALL, 1))
    grid_spec = pltpu.PrefetchScalarGridSpec(
        num_scalar_prefetch=1,
        grid=(R_ALL // TR,),
        in_specs=[pl.BlockSpec(memory_space=pl.ANY),
                  pl.BlockSpec((TOP_K, TR, 1), lambda t, *_: (0, t, 0)),
                  pl.BlockSpec((TR, D_MODEL), lambda t, *_: (t, 0)),
                  pl.BlockSpec((None, 8, D_MODEL), lambda t, *_: (l, 0, which))],
        out_specs=pl.BlockSpec((TR, D_MODEL), lambda t, *_: (t, 0)),
        scratch_shapes=[pltpu.VMEM((2, TOP_K, TR, D_MODEL), F32), pltpu.SemaphoreType.DMA((2,))],
    )
    return pl.pallas_call(
        _combine_body,
        grid_spec=grid_spec,
        out_shape=jax.ShapeDtypeStruct((R_ALL, D_MODEL), F32),
        input_output_aliases={3: 0},
        compiler_params=_cparams(1),
        name="moe_combine",
    )(pos, ys, w_cols, x, mods)


def _moe(h32, logits, xs, mods, l, i, moe_w1, moe_w3, moe_w2):
    pos, src_token, wts, tile_expert, n_used = _route(logits)
    group = (tile_expert, n_used)
    hs = _moe_gather(h32, src_token, n_used)
    g = _swiglu_up(hs, moe_w1, moe_w3, (i, 0), group=group)
    ys = _mm([g], [(moe_w2, (i, 0))], [], _moe_down_epilogue, n_out=D_MODEL, out_dtype=F32,
             tn=RS_TN, n_rows=MOE_P, tm=MOE_TM, group=group, name="moe_down")
    return _moe_combine(ys, pos, wts, xs, mods, l, 5)


AT_TQ = 512
AT_TK = 1024


def _attn_body(lam_ref, q_ref, *rest, n_seg, post_scale):
    k_refs = rest[:n_seg]
    v_refs = rest[n_seg:2 * n_seg]
    sub_ref = rest[2 * n_seg]
    o_ref = rest[2 * n_seg + 1]
    lam = lam_ref[0]
    outs = []
    for s in range(2):
        sl = slice(s * HEAD_DIM, (s + 1) * HEAD_DIM)
        qs = q_ref[:, sl]
        m = den = num = None
        for k_ref, v_ref in zip(k_refs, v_refs):
            seg_len = k_ref.shape[0]
            for c0 in range(0, seg_len, AT_TK):
                c1 = min(c0 + AT_TK, seg_len)
                sc = lax.dot_general(qs, k_ref[c0:c1, sl], (((1,), (1,)), ((), ())),
                                     preferred_element_type=F32)
                mc = sc.max(axis=-1, keepdims=True)
                m_new = mc if m is None else jnp.maximum(m, mc)
                p = jnp.exp2(sc - m_new)
                ps = p.sum(axis=-1, keepdims=True)
                pv = jnp.dot(p.astype(BF16), v_ref[c0:c1, :], preferred_element_type=F32)
                if m is None:
                    den, num = ps, pv
                else:
                    alpha = jnp.exp2(m - m_new)
                    den = alpha * den + ps
                    num = alpha * num + pv
                m = m_new
        outs.append(num / den)
    d = outs[0] - lam * outs[1]
    ms = jnp.mean(d * d, axis=-1, keepdims=True)
    o_ref[...] = (d * lax.rsqrt(ms + EPS) * sub_ref[...] * post_scale).astype(BF16)


def _attention(qk, vu, lam, sub_norm, l, lam_init, ctx_only=False):
    kb = OFF_K // V_HEAD_DIM
    ctx_blk0 = R_LAT // CTX_LEN
    if ctx_only:
        tq = CTX_LEN
        grid = (BATCH, N_DIFF_HEADS, 1)
        q_map = lambda b, h, t: (ctx_blk0 + b, h)
        o_map = lambda b, h, t: (b, h)
        segs = [CTX_LEN]
    else:
        tq = AT_TQ
        grid = (BATCH, N_DIFF_HEADS, SEQ // tq)
        q_map = o_map = lambda b, h, t: (b * (SEQ // tq) + t, h)
        segs = [CTX_LEN, SEQ]
    in_specs = [pl.BlockSpec(memory_space=pltpu.SMEM),
                pl.BlockSpec((tq, V_HEAD_DIM), q_map)]
    args = [lam.reshape(1), qk]
    for arr, off in ((qk, kb), (vu, 0)):
        for seg_len in segs:
            if seg_len == CTX_LEN:
                imap = functools.partial(lambda b, h, t, o: (ctx_blk0 + b, o + h), o=off)
            else:
                imap = functools.partial(lambda b, h, t, o: (b, o + h), o=off)
            in_specs.append(pl.BlockSpec((seg_len, V_HEAD_DIM), imap))
            args.append(arr)
    in_specs.append(pl.BlockSpec((None, 1, V_HEAD_DIM), lambda b, h, t: (l, 0, 0)))
    args.append(sub_norm.reshape(DEPTH, 1, V_HEAD_DIM))
    return pl.pallas_call(
        functools.partial(_attn_body, n_seg=len(segs), post_scale=1.0 - lam_init),
        grid=grid,
        in_specs=in_specs,
        out_specs=pl.BlockSpec((tq, V_HEAD_DIM), o_map),
        out_shape=jax.ShapeDtypeStruct((R_CTX if ctx_only else R_LAT, ATT_W), BF16),
        compiler_params=_cparams(3),
        name="attn_ctx" if ctx_only else "attn",
    )(*args)


def _dft_tables():
    n, n1, n2 = FFT_N, FFT_N1, FFT_N2
    k1 = np.arange(n1)[:, None]
    tabs_d, tabs_s, tabs_i = [], [], []
    for b in range(n2):
        cols = n2 * np.arange(n1)[None, :] + b
        g = np.exp(-2j * np.pi * ((k1 * cols) % n) / n)
        gd = g[:, :n1 // 2]
        tabs_d.append(np.block([[gd.real, -gd.imag], [gd.imag, gd.real]]))
        tabs_s.append(np.concatenate([g.real, g.imag], axis=0))
        gi = np.conj(gd).T / n1
        tabs_i.append(np.block([[gi.real, -gi.imag], [gi.imag, gi.real]]))
    kk = np.arange(n2)
    f2 = np.exp(-2j * np.pi * ((kk[:, None] * kk[None, :]) % n2) / n2)
    g2 = np.block([[f2.real, -f2.imag], [f2.imag, f2.real]])
    f2i = np.conj(f2) / n2
    g2i = np.block([[f2i.real, -f2i.imag], [f2i.imag, f2i.real]])
    as_bf = lambda a: jnp.asarray(np.asarray(a, np.float32)).astype(BF16)
    return (as_bf(np.stack(tabs_d)), as_bf(np.stack(tabs_s)), as_bf(np.stack(tabs_i)),
            as_bf(g2), as_bf(g2i))


FFT_UNROLL = 16


def _pack_pair(a, b):
    ab = lax.bitcast_convert_type(a.astype(BF16).astype(F32), jnp.uint32)
    bb = lax.bitcast_convert_type(b.astype(BF16).astype(F32), jnp.uint32)
    return lax.bitcast_convert_type(ab | (bb >> 16), F32)


def _unpack_pair(w):
    bits = lax.bitcast_convert_type(w, jnp.uint32)
    hi = lax.bitcast_convert_type(bits & jnp.uint32(0xFFFF0000), F32)
    lo = lax.bitcast_convert_type(bits << 16, F32)
    return hi, lo


def _spectrum_body(hf_ref, hr_ref, wf_ref, wb_ref, dl_ref, g1_ref, g2_ref, or_ref, oi_ref,
                   k_ref, a_ref):
    n1, n2 = FFT_N1, FFT_N2
    delta = dl_ref[...]
    step = 1.0 / (SEQ - 1)
    rows = 512
    row_iota = lax.broadcasted_iota(jnp.int32, (rows, 1), 0)

    def taps(c, carry):
        r0 = pl.multiple_of(c * rows, rows)
        pos = (row_iota + r0).astype(F32)
        top = jnp.dot(hf_ref[pl.ds(r0, rows), :].astype(BF16), wf_ref[...].astype(BF16),
                      preferred_element_type=F32)
        bot = jnp.dot(hr_ref[pl.ds(r0, rows), :].astype(BF16), wb_ref[...].astype(BF16),
                      preferred_element_type=F32)
        k_ref[pl.ds(r0, rows), :] = top * jnp.exp(-(pos * step) * delta)
        k_ref[pl.ds(SEQ + r0, rows), :] = bot * jnp.exp(-((SEQ - pos) * step) * delta)
        return carry

    lax.fori_loop(0, SEQ // rows, taps, 0)

    def stage1(b, carry):
        x = k_ref[pl.ds(b, n1, stride=n2), :].astype(BF16)
        a = jnp.dot(g1_ref[b], x, preferred_element_type=F32)
        row = pl.multiple_of(b * n1, n1)
        a_ref[pl.ds(row, n1), :] = _pack_pair(a[:n1], a[n1:])
        return carry

    lax.fori_loop(0, n2, stage1, 0, unroll=FFT_UNROLL)

    def stage2(k1, carry):
        a = jnp.concatenate(_unpack_pair(a_ref[pl.ds(k1, n2, stride=n1), :]), axis=0).astype(BF16)
        x = jnp.dot(g2_ref[...], a, preferred_element_type=F32)
        row = pl.multiple_of(k1 * n2, n2)
        or_ref[pl.ds(row, n2), :] = x[:n2].astype(BF16)
        oi_ref[pl.ds(row, n2), :] = x[n2:].astype(BF16)
        return carry

    lax.fori_loop(0, n1, stage2, 0, unroll=FFT_UNROLL)


def _const_spec(shape, n_grid):
    nd = len(shape)
    imap = (lambda a, b: (0,) * nd) if n_grid == 2 else (lambda a: (0,) * nd)
    return pl.BlockSpec(shape, imap, pipeline_mode=pl.Buffered(1))


def _spectrum(hid_f, hid_r, w4p, deltas, l, g1s, g2):
    tc = HY_TC
    nct = HY_W // tc
    hspec = pl.BlockSpec((SEQ, LANES), lambda f, ct: (0, 0))
    wspec = lambda d: pl.BlockSpec(
        (None, LANES, tc), functools.partial(lambda f, ct, d: (l, 0, (2 * f + d) * nct + ct), d=d))
    ospec = pl.BlockSpec((None, FFT_N, tc), lambda f, ct: (f, 0, ct))
    return pl.pallas_call(
        _spectrum_body,
        grid=(HY_N_FILT, nct),
        in_specs=[hspec, hspec, wspec(0), wspec(1),
                  pl.BlockSpec((1, tc), lambda f, ct: (0, ct)),
                  _const_spec((FFT_N2, 2 * FFT_N1, FFT_N1), 2),
                  _const_spec((2 * FFT_N2, 2 * FFT_N2), 2)],
        out_specs=[ospec, ospec],
        out_shape=[jax.ShapeDtypeStruct((HY_N_FILT, FFT_N, HY_W), BF16)] * 2,
        scratch_shapes=[pltpu.VMEM((FFT_N, tc), F32)] * 2,
        compiler_params=_cparams(2),
        name="hy_spectrum",
    )(hid_f, hid_r, w4p, w4p, deltas, g1s, g2)


def _conv3(u, w_ref, b_ref):
    n = u.shape[0]
    row = lax.broadcasted_iota(jnp.int32, u.shape, 0)
    prev = jnp.where(row == 0, 0.0, pltpu.roll(u, 1, 0))
    nxt = jnp.where(row == n - 1, 0.0, pltpu.roll(u, n - 1, 0))
    return prev * w_ref[0:1, :] + u * w_ref[1:2, :] + nxt * w_ref[2:3, :] + b_ref[...]


def _hyconv_body(s0_ref, s1_ref, m0_ref, m1_ref, cws_ref, cbs_ref, cwm_ref, cbm_ref, bias_ref,
                 kr_ref, ki_ref, g1_ref, g1i_ref, g2_ref, g2i_ref, o_ref,
                 vr_ref, vi_ref, vp_ref, a_ref, b_ref, *, conv_signal):
    n1, n2 = FFT_N1, FFT_N2
    h1 = n1 // 2
    for s_ref, v_ref in ((s0_ref, vr_ref), (s1_ref, vi_ref)):
        u = s_ref[...].astype(F32)
        v_ref[...] = _conv3(u, cws_ref, cbs_ref) if conv_signal else u
    vp_ref[...] = _pack_pair(vr_ref[...], vi_ref[...])

    def stage1(b, carry):
        x = jnp.concatenate(_unpack_pair(vp_ref[pl.ds(b, h1, stride=n2), :]), axis=0).astype(BF16)
        a = jnp.dot(g1_ref[b], x, preferred_element_type=F32)
        row = pl.multiple_of(b * n1, n1)
        a_ref[pl.ds(row, n1), :] = _pack_pair(a[:n1], a[n1:])
        return carry

    lax.fori_loop(0, n2, stage1, 0, unroll=FFT_UNROLL)

    def stage2(k1, carry):
        a = jnp.concatenate(_unpack_pair(a_ref[pl.ds(k1, n2, stride=n1), :]), axis=0).astype(BF16)
        x = jnp.dot(g2_ref[...], a, preferred_element_type=F32)
        row = pl.multiple_of(k1 * n2, n2)
        fr = kr_ref[pl.ds(row, n2), :].astype(F32)
        fi = ki_ref[pl.ds(row, n2), :].astype(F32)
        xr, xi = x[:n2], x[n2:]
        y = jnp.concatenate([xr * fr - xi * fi, xr * fi + xi * fr], axis=0).astype(BF16)
        bq = jnp.dot(g2i_ref[...], y, preferred_element_type=F32)
        b_ref[pl.ds(row, n2), :] = _pack_pair(bq[:n2], bq[n2:])
        return carry

    lax.fori_loop(0, n1, stage2, 0, unroll=FFT_UNROLL)

    def stage1_inv(b, carry):
        bq = jnp.concatenate(_unpack_pair(b_ref[pl.ds(b, n1, stride=n2), :]), axis=0).astype(BF16)
        y = jnp.dot(g1i_ref[b], bq, preferred_element_type=F32)
        row = pl.multiple_of(b * h1, h1)
        a_ref[pl.ds(row, h1), :] = _pack_pair(y[:h1], y[h1:])
        return carry

    lax.fori_loop(0, n2, stage1_inv, 0, unroll=FFT_UNROLL)

    bias = bias_ref[...]

    def unpermute(i, carry):
        row = pl.multiple_of(i * n2, n2)
        yr, yi = _unpack_pair(a_ref[pl.ds(i, n2, stride=h1), :])
        vr_ref[pl.ds(row, n2), :] = yr + bias * vr_ref[pl.ds(row, n2), :]
        vi_ref[pl.ds(row, n2), :] = yi + bias * vi_ref[pl.ds(row, n2), :]
        return carry

    lax.fori_loop(0, h1, unpermute, 0, unroll=FFT_UNROLL)

    for half, (m_ref, v_ref) in enumerate(((m0_ref, vr_ref), (m1_ref, vi_ref))):
        mult = _conv3(m_ref[...].astype(F32), cwm_ref, cbm_ref)
        o_ref[half * SEQ:(half + 1) * SEQ, :] = (mult * v_ref[...]).astype(o_ref.dtype)


def _hy_param_specs(conv_w, conv_b, bias, l, filt, sig_ch, mul_ch, tc):
    cw3 = conv_w.reshape(DEPTH, 3, 3 * HY_W)
    cb3 = conv_b.reshape(DEPTH, 1, 3 * HY_W)
    chan = lambda ch0, rows: pl.BlockSpec(
        (None, rows, tc), functools.partial(lambda ct, pr, c0: (l, 0, c0 + ct), c0=ch0))
    specs = [chan(sig_ch, 3), chan(sig_ch, 1), chan(mul_ch, 3), chan(mul_ch, 1),
             pl.BlockSpec((None, None, 1, tc), lambda ct, pr: (l, filt, 0, ct))]
    args = [cw3, cb3, cw3, cb3, bias.reshape(DEPTH, HY_N_FILT, 1, HY_W)]
    return specs, args


def _hyconv(sig, sig_cb, mul, mul_cb, conv_w, conv_b, bias, l, filt, sig_ch, mul_ch,
            kr, ki, tabs, conv_signal):
    g1d, g1i, g2, g2i = tabs
    tc = HY_TC
    blk = lambda arr_cb, odd: pl.BlockSpec(
        (SEQ, tc), functools.partial(lambda ct, pr, cb, o: (2 * pr + o, cb + ct), cb=arr_cb, o=odd))
    pspecs, pargs = _hy_param_specs(conv_w, conv_b, bias, l, filt, sig_ch, mul_ch, tc)
    kspec = pl.BlockSpec((None, FFT_N, tc), lambda ct, pr: (filt, 0, ct))
    in_specs = [blk(sig_cb, 0), blk(sig_cb, 1), blk(mul_cb, 0), blk(mul_cb, 1)] + pspecs + [
        kspec, kspec,
        _const_spec((FFT_N2, 2 * FFT_N1, FFT_N1), 2),
        _const_spec((FFT_N2, FFT_N1, 2 * FFT_N1), 2),
        _const_spec((2 * FFT_N2, 2 * FFT_N2), 2),
        _const_spec((2 * FFT_N2, 2 * FFT_N2), 2),
    ]
    return pl.pallas_call(
        functools.partial(_hyconv_body, conv_signal=conv_signal),
        grid=(HY_W // tc, BATCH // 2),
        in_specs=in_specs,
        out_specs=pl.BlockSpec((2 * SEQ, tc), lambda ct, pr: (pr, ct)),
        out_shape=jax.ShapeDtypeStruct((R_LAT, HY_W), BF16),
        scratch_shapes=[pltpu.VMEM((SEQ, tc), F32)] * 3 + [pltpu.VMEM((FFT_N, tc), F32)] * 2,
        compiler_params=_cparams(2),
        name="hyconv",
    )(sig, sig, mul, mul, *pargs, kr, ki, g1d, g1i, g2, g2i)


def _ctx_dft_tables():
    n = 2 * CTX_LEN
    kk = np.arange(n)
    f = np.exp(-2j * np.pi * ((kk[:, None] * kk[None, :]) % n) / n)
    fd = f[:, :CTX_LEN]
    fwd = np.block([[fd.real, -fd.imag], [fd.imag, fd.real]])
    spec = np.concatenate([f.real, f.imag], axis=0)
    fi = np.conj(f)[:CTX_LEN, :] / n
    inv = np.block([[fi.real, -fi.imag], [fi.imag, fi.real]])
    as_bf = lambda a: jnp.asarray(np.asarray(a, np.float32)).astype(BF16)
    return as_bf(fwd), as_bf(spec), as_bf(inv)


def _hyconv_ctx_body(s0_ref, s1_ref, m0_ref, m1_ref, cws_ref, cbs_ref, cwm_ref, cbm_ref, bias_ref,
                     k_ref, fwd_ref, spec_ref, inv_ref, o_ref, *, conv_signal):
    n = 2 * CTX_LEN
    vs = []
    for s_ref in (s0_ref, s1_ref):
        u = s_ref[...].astype(F32)
        vs.append(_conv3(u, cws_ref, cbs_ref) if conv_signal else u)
    x = jnp.dot(fwd_ref[...], jnp.concatenate(vs, axis=0).astype(BF16), preferred_element_type=F32)
    kf = jnp.dot(spec_ref[...], k_ref[...].astype(BF16), preferred_element_type=F32)
    kf = kf.astype(BF16).astype(F32)
    xr, xi, fr, fi = x[:n], x[n:], kf[:n], kf[n:]
    y = jnp.concatenate([xr * fr - xi * fi, xr * fi + xi * fr], axis=0).astype(BF16)
    conv = jnp.dot(inv_ref[...], y, preferred_element_type=F32)
    bias = bias_ref[...]
    for half, m_ref in enumerate((m0_ref, m1_ref)):
        sl = slice(half * CTX_LEN, (half + 1) * CTX_LEN)
        mult = _conv3(m_ref[...].astype(F32), cwm_ref, cbm_ref)
        o_ref[sl, :] = (mult * (conv[sl] + bias * vs[half])).astype(o_ref.dtype)


def _hyconv_ctx(sig, sig_rb, sig_cb, mul, mul_rb, mul_cb, conv_w, conv_b, bias, l, filt,
                sig_ch, mul_ch, kctx, tabs, conv_signal):
    fwd, spec, inv = tabs
    tc = HY_TC
    blk = lambda rb, arr_cb, odd: pl.BlockSpec(
        (CTX_LEN, tc),
        functools.partial(lambda ct, pr, rb, cb, o: (rb + 2 * pr + o, cb + ct), rb=rb, cb=arr_cb, o=odd))
    pspecs, pargs = _hy_param_specs(conv_w, conv_b, bias, l, filt, sig_ch, mul_ch, tc)
    n = 2 * CTX_LEN
    in_specs = [blk(sig_rb, sig_cb, 0), blk(sig_rb, sig_cb, 1),
                blk(mul_rb, mul_cb, 0), blk(mul_rb, mul_cb, 1)] + pspecs + [
        pl.BlockSpec((None, n, tc), lambda ct, pr: (filt, 0, ct)),
        _const_spec((2 * n, n), 2),
        _const_spec((2 * n, n), 2),
        _const_spec((n, 2 * n), 2),
    ]
    return pl.pallas_call(
        functools.partial(_hyconv_ctx_body, conv_signal=conv_signal),
        grid=(HY_W // tc, BATCH // 2),
        in_specs=in_specs,
        out_specs=pl.BlockSpec((2 * CTX_LEN, tc), lambda ct, pr: (pr, ct)),
        out_shape=jax.ShapeDtypeStruct((R_CTX, HY_W), BF16),
        compiler_params=_cparams(2),
        name="hyconv_ctx",
    )(sig, sig, mul, mul, *pargs, kctx, fwd, spec, inv)


def _rope_tables():
    rows = SEQ // GRID_W
    row = jnp.repeat(jnp.arange(rows), GRID_W).astype(F32)
    col = jnp.tile(jnp.arange(GRID_W), rows).astype(F32)
    inv = ROPE_BASE ** (-jnp.arange(ROT_FREQS, dtype=F32) / ROT_FREQS)
    ar, ac = row[:, None] * inv, col[:, None] * inv
    cos = jnp.concatenate([jnp.cos(ar), jnp.cos(ar), jnp.cos(ac), jnp.cos(ac)], axis=1)
    sin = jnp.concatenate([-jnp.sin(ar), jnp.sin(ar), -jnp.sin(ac), jnp.sin(ac)], axis=1)
    cos = jnp.concatenate([cos, jnp.ones((TM, HEAD_DIM), F32)], axis=0)
    sin = jnp.concatenate([sin, jnp.zeros((TM, HEAD_DIM), F32)], axis=0)
    return cos, sin


def _hyena_positions(L, reverse):
    idx = jnp.arange(L, dtype=F32)
    return (L - idx) if reverse else idx


def _hyena_hidden(L, w1, b1, w2, b2, w3, b3, freq, reverse=False):
    pos = _hyena_positions(L, reverse)[:, None]
    t = pos / (L - 1)
    w = 2.0 * math.pi * pos / L
    f = jnp.linspace(1e-4, HY_BANDS - 1, HY_BANDS, dtype=F32)
    z = jnp.concatenate([t, jnp.cos(w * f), -jnp.sin(w * f)], axis=-1)
    dot = functools.partial(jnp.dot, precision=HIGHEST)
    hid = jnp.sin(freq * (dot(z, w1) + b1))
    hid = jnp.sin(freq * (dot(hid, w2) + b2))
    hid = jnp.sin(freq * (dot(hid, w3) + b3))
    if reverse:
        hid = jnp.where(jnp.arange(L)[:, None] > 0, hid, 0.0)
    return hid


def _hyena_deltas():
    return jnp.abs(jnp.linspace(math.log(HY_TARGET) / HY_SLOW_PCT,
                                math.log(HY_TARGET) / HY_FAST_PCT, HY_W, dtype=F32))


def _hyena_filters_ctx(hid, hid_rev, w4):
    L = hid.shape[0]
    w4 = w4.reshape(HY_FH, HY_N_FILT, 2, HY_W)
    halves = []
    for d, feats in enumerate((hid, hid_rev)):
        t = _hyena_positions(L, bool(d))[:, None] / (L - 1)
        taps = jnp.einsum('lh,hfc->flc', feats, w4[:, :, d], precision=HIGHEST)
        halves.append(taps * jnp.exp(-t * _hyena_deltas())[None])
    return jnp.concatenate(halves, axis=1)


def kernel(x, c, ctx, c_ctx, w_ada, b_ada, norm1, norm2, w_in, b_gate, q_norm, k_norm, lam_q1, lam_k1, lam_q2, lam_k2, sub_norm, hy_conv_w, hy_conv_b, hy_w1, hy_b1, hy_w2, hy_b2, hy_w3, hy_b3, hy_w4, hy_freq, hy_bias, w_br_a, w_br_b, w_out, ffn_w1, ffn_w3, ffn_w2, router, moe_w1, moe_w3, moe_w2):
    xs = jnp.concatenate([x.reshape(R_LAT, D_MODEL), ctx.reshape(R_CTX, D_MODEL)], axis=0)
    cs = jnp.concatenate([c, c_ctx[None], jnp.zeros((8 - BATCH - 1, D_MODEL), F32)], axis=0)
    mods = _adaln(cs, w_ada, b_ada)
    cos_t, sin_t = _rope_tables()
    g1d, g1s, g1i, g2, g2i = _dft_tables()
    ctx_tabs = _ctx_dft_tables()
    gains = jnp.zeros((DEPTH, 8, HEAD_DIM), F32)
    gains = gains.at[:, 0].set(q_norm * (math.log2(math.e) / math.sqrt(HEAD_DIM))).at[:, 1].set(k_norm)
    router_pad = jnp.zeros((router.shape[0], D_MODEL, LANES), F32).at[:, :, :N_EXPERTS].set(router)
    hy_cb = ATT_W // HY_TC
    ch = HY_W // HY_TC
    w4p = jnp.pad(hy_w4, ((0, 0), (0, LANES - HY_FH), (0, 0)))
    deltas = _hyena_deltas()[None]

    for l in range(DEPTH):
        lam_init = 0.8 - 0.6 * math.exp(-0.3 * l)
        lam = (jnp.exp(jnp.sum(lam_q1[l] * lam_k1[l])) - jnp.exp(jnp.sum(lam_q2[l] * lam_k2[l]))
               + lam_init)
        h = _normmod(xs, norm1, mods, l, 0)
        qk, vu, gates = _inproj(h, w_in, l, gains, cos_t, sin_t, b_gate)

        ya = (_attention(qk, vu, lam, sub_norm, l, lam_init),
              _attention(qk, vu, lam, sub_norm, l, lam_init, ctx_only=True))

        hp = (hy_w1[l], hy_b1[l], hy_w2[l], hy_b2[l], hy_w3[l], hy_b3[l], hy_freq[l])
        lane_pad = ((0, 0), (0, LANES - HY_FH))
        hid = jnp.pad(_hyena_hidden(SEQ, *hp), lane_pad)
        hid_rev = jnp.pad(_hyena_hidden(SEQ, *hp, reverse=True), lane_pad)
        kr, ki = _spectrum(hid, hid_rev, w4p, deltas, l, g1s, g2)
        kctx = _hyena_filters_ctx(_hyena_hidden(CTX_LEN, *hp),
                                  _hyena_hidden(CTX_LEN, *hp, reverse=True), hy_w4[l])
        common = (hy_conv_w, hy_conv_b, hy_bias, l)
        tabs = (g1d, g1i, g2, g2i)
        crb = R_LAT // CTX_LEN
        z = _hyconv(vu, hy_cb, vu, hy_cb + ch, *common, 0, 0, ch, kr, ki, tabs, True)
        zc = _hyconv_ctx(vu, crb, hy_cb, vu, crb, hy_cb + ch, *common, 0, 0, ch, kctx, ctx_tabs, True)
        yb = (_hyconv(z, 0, vu, hy_cb + 2 * ch, *common, 1, 0, 2 * ch, kr, ki, tabs, False),
              _hyconv_ctx(zc, 0, 0, vu, crb, hy_cb + 2 * ch, *common, 1, 0, 2 * ch, kctx, ctx_tabs, False))

        m = _merge(ya, yb, gates, w_br_a, w_br_b, l)
        xs = _resid_proj(m, w_out, (l,), xs, mods, l, 2, name="out_proj")

        i = l // 2
        if l % 2 == 0:
            h2 = _normmod(xs, norm2, mods, l, 3)
            g = _swiglu_up(h2, ffn_w1, ffn_w3, (i,))
            xs = _resid_proj(g, ffn_w2, (i,), xs, mods, l, 5, tm=FFD_TM, name="ffn_down")
        else:
            h32, logits = _normmod(xs, norm2, mods, l, 3, router=router_pad[i])
            xs = _moe(h32, logits, xs, mods, l, i, moe_w1, moe_w3, moe_w2)
    return xs[:R_LAT].reshape(BATCH, SEQ, D_MODEL)
```

```python
import functools
import math

import numpy as np
import jax
import jax.numpy as jnp
from jax import lax
from jax.experimental import pallas as pl
from jax.experimental.pallas import tpu as pltpu

D_MODEL = 2048
BATCH = 4
SEQ = 4096
DEPTH = 4
CTX_LEN = 256
GRID_W = 64
HEAD_DIM = 128
N_DIFF_HEADS = D_MODEL // (2 * HEAD_DIM)
V_HEAD_DIM = 2 * HEAD_DIM
ATT_W = N_DIFF_HEADS * V_HEAD_DIM
HY_W = D_MODEL
OFF_K = ATT_W
OFF_V = 2 * ATT_W
OFF_HY = 3 * ATT_W
OFF_G = 3 * ATT_W + 3 * HY_W
IN_COLS = OFF_G + 2 * D_MODEL
ROPE_BASE = 10000.0
ROT_FREQS = HEAD_DIM // 4
HY_EMB = 33
HY_BANDS = (HY_EMB - 1) // 2
HY_FH = 64
HY_N_FILT = 2
HY_TARGET = 1e-2
HY_FAST_PCT = 0.3
HY_SLOW_PCT = 1.5
D_FF = 5632
N_EXPERTS = 8
TOP_K = 2
EPS = 1e-6

R_LAT = BATCH * SEQ
R_CTX = BATCH * CTX_LEN
R_ALL = R_LAT + R_CTX
TM = 1024
CTX_TILE = R_LAT // TM
CTX_SEG = BATCH
TR = 256
LANES = 128
VMEM_LIMIT = 56 * 1024 * 1024

FFT_N = 2 * SEQ
FFT_N1 = 128
FFT_N2 = 64
HY_TC = 128

F32 = jnp.float32
BF16 = jnp.bfloat16
HIGHEST = lax.Precision.HIGHEST


def _cparams(n_axes):
    return pltpu.CompilerParams(
        dimension_semantics=("arbitrary",) * n_axes, vmem_limit_bytes=VMEM_LIMIT)


def _row_seg(i, tm=TM):
    return jnp.where(i >= R_LAT // tm, CTX_SEG, i // (SEQ // tm))


def _adaln_body(c_ref, w_ref, b_ref, o_ref):
    s = c_ref[...]
    s = s * jax.nn.sigmoid(s)
    o_ref[...] = jnp.dot(s, w_ref[...], preferred_element_type=F32, precision=HIGHEST) + b_ref[...]


def _adaln(cs, w_ada, b_ada):
    tn = 1536
    n = 6 * D_MODEL
    return pl.pallas_call(
        _adaln_body,
        grid=(DEPTH, n // tn),
        in_specs=[
            pl.BlockSpec((8, D_MODEL), lambda l, j: (0, 0)),
            pl.BlockSpec((None, D_MODEL, tn), lambda l, j: (l, 0, j)),
            pl.BlockSpec((None, 1, tn), lambda l, j: (l, 0, j)),
        ],
        out_specs=pl.BlockSpec((None, 8, tn), lambda l, j: (l, 0, j)),
        out_shape=jax.ShapeDtypeStruct((DEPTH, 8, n), F32),
        compiler_params=_cparams(2),
        name="adaln",
    )(cs, w_ada, b_ada.reshape(DEPTH, 1, n))


def _normmod_body(x_ref, g_ref, sh_ref, sc_ref, *rest, with_router):
    i = pl.program_id(0)
    seg = jnp.where(i >= R_LAT // TR, CTX_SEG, i // (SEQ // TR))
    x = x_ref[...]
    ms = jnp.mean(x * x, axis=-1, keepdims=True)
    y = x * lax.rsqrt(ms + EPS) * g_ref[...]
    h = y * (1.0 + sc_ref[pl.ds(seg, 1), :]) + sh_ref[pl.ds(seg, 1), :]
    if with_router:
        r_ref, o_ref, lg_ref = rest
        lg_ref[...] = jnp.dot(h, r_ref[...], preferred_element_type=F32, precision=HIGHEST)
    else:
        (o_ref,) = rest
    o_ref[...] = h.astype(o_ref.dtype)


def _normmod(x, gain, mods, l, which, router=None):
    in_specs = [
        pl.BlockSpec((TR, D_MODEL), lambda i: (i, 0)),
        pl.BlockSpec((None, 1, D_MODEL), lambda i: (l, 0, 0)),
        pl.BlockSpec((None, 8, D_MODEL), lambda i: (l, 0, which)),
        pl.BlockSpec((None, 8, D_MODEL), lambda i: (l, 0, which + 1)),
    ]
    args = [x, gain.reshape(DEPTH, 1, D_MODEL), mods, mods]
    out_specs = pl.BlockSpec((TR, D_MODEL), lambda i: (i, 0))
    out_shape = jax.ShapeDtypeStruct((R_ALL, D_MODEL), BF16 if router is None else F32)
    if router is not None:
        in_specs.append(pl.BlockSpec((D_MODEL, LANES), lambda i: (0, 0)))
        args.append(router)
        out_specs = [out_specs, pl.BlockSpec((TR, LANES), lambda i: (i, 0))]
        out_shape = [out_shape, jax.ShapeDtypeStruct((R_ALL, LANES), F32)]
    return pl.pallas_call(
        functools.partial(_normmod_body, with_router=router is not None),
        grid=(R_ALL // TR,),
        in_specs=in_specs,
        out_specs=out_specs,
        out_shape=out_shape,
        compiler_params=_cparams(1),
        name="normmod",
    )(*args)


def _mm_body(*refs, n_pref, n_a, n_w, n_aux, epilogue, a_of_w, split_ctx):
    pref = refs[:n_pref]
    refs = refs[n_pref:]
    if split_ctx:
        a_refs, a_ctx_refs = refs[:2 * n_a:2], refs[1:2 * n_a:2]
        refs = refs[n_a:]
    else:
        a_refs = a_ctx_refs = refs[:n_a]
    w_refs = refs[n_a:n_a + n_w]
    aux_refs = refs[n_a + n_w:n_a + n_w + n_aux]
    o_ref = refs[n_a + n_w + n_aux]
    wbf_refs = refs[n_a + n_w + n_aux + 1:n_a + 2 * n_w + n_aux + 1]
    aux_refs = tuple(aux_refs) + tuple(refs[n_a + 2 * n_w + n_aux + 1:])
    j = pl.program_id(0)
    i = pl.program_id(1)
    if n_pref:
        te_ref, nu_ref = pref
        new_weights = (i == 0) | (te_ref[i] != te_ref[jnp.maximum(i - 1, 0)])
        valid = i < nu_ref[0]
    else:
        new_weights = i == 0
        valid = None

    @pl.when(new_weights)
    def _():
        for w_ref, wbf_ref in zip(w_refs, wbf_refs):
            wbf_ref[...] = w_ref[...].astype(BF16)

    def compute(srcs):
        accs = [jnp.dot(srcs[a_of_w[k]][...], wbf_refs[k][...], preferred_element_type=F32)
                for k in range(n_w)]
        epilogue(accs, aux_refs, o_ref, j, i)

    if split_ctx:
        pl.when(i < CTX_TILE)(functools.partial(compute, a_refs))
        pl.when(i >= CTX_TILE)(functools.partial(compute, a_ctx_refs))
    elif valid is None:
        compute(a_refs)
    else:
        pl.when(valid)(functools.partial(compute, a_refs))

        @pl.when(jnp.logical_not(valid))
        def _():
            o_ref[...] = jnp.zeros(o_ref.shape, o_ref.dtype)


def _mm(a_list, w_list, aux_list, epilogue, *, n_out, out_dtype, tn, a_of_w=None,
        n_rows=R_ALL, tm=TM, alias_aux=None, group=None, split_ctx=False, w_col0=0,
        epilogue_scratch=(), name="mm"):
    n_a, n_w, n_aux = len(a_list), len(w_list), len(aux_list)
    n_pref = 0 if group is None else 2
    a_of_w = tuple(a_of_w) if a_of_w is not None else tuple(range(n_w))
    in_specs, args, scratch = [], [], []
    for a in a_list:
        if split_ctx:
            lat, ctx = a
            assert tm == R_CTX and group is None
            in_specs.append(pl.BlockSpec((tm, lat.shape[1]),
                                         lambda j, i: (jnp.minimum(i, CTX_TILE - 1), 0)))
            in_specs.append(pl.BlockSpec((tm, ctx.shape[1]), lambda j, i: (0, 0),
                                         pipeline_mode=pl.Buffered(1)))
            args += [lat, ctx]
        else:
            in_specs.append(pl.BlockSpec((tm, a.shape[1]), lambda j, i, *_: (i, 0)))
            args.append(a)
    for w, prefix in w_list:
        k = w.shape[-2]
        if group is None:
            wmap = functools.partial(lambda j, i, p: p + (0, w_col0 + j), p=tuple(prefix))
        else:
            wmap = functools.partial(lambda j, i, te, nu, p: p[:-1] + (te[i], 0, j), p=tuple(prefix))
        in_specs.append(pl.BlockSpec((None,) * len(prefix) + (k, tn), wmap))
        args.append(w)
        scratch.append(pltpu.VMEM((k, tn), BF16))
    for arr, block, imap in aux_list:
        in_specs.append(pl.BlockSpec(block, functools.partial(lambda j, i, *_, f: f(j, i), f=imap)))
        args.append(arr)
    aliases = {}
    if alias_aux is not None:
        aliases = {n_pref + (2 * n_a if split_ctx else n_a) + n_w + alias_aux: 0}
    grid_spec = pltpu.PrefetchScalarGridSpec(
        num_scalar_prefetch=n_pref,
        grid=(n_out // tn, n_rows // tm),
        in_specs=in_specs,
        out_specs=pl.BlockSpec((tm, tn), lambda j, i, *_: (i, j)),
        scratch_shapes=scratch + list(epilogue_scratch),
    )
    return pl.pallas_call(
        functools.partial(_mm_body, n_pref=n_pref, n_a=n_a, n_w=n_w, n_aux=n_aux,
                          epilogue=epilogue, a_of_w=a_of_w, split_ctx=split_ctx),
        grid_spec=grid_spec,
        out_shape=jax.ShapeDtypeStruct((n_rows, n_out), out_dtype),
        input_output_aliases=aliases,
        compiler_params=_cparams(2),
        name=name,
    )(*(list(group) if group is not None else []), *args)


IN_TN = 512
IN_TN_WIDE = 1024

QK_ROWS = 512


def _qk_epilogue(accs, aux, o_ref, j, i):
    gains_ref, cos_ref, sin_ref, acc_ref = aux
    acc_ref[...] = accs[0]
    gain = gains_ref[pl.ds(j // (OFF_K // IN_TN), 1), :]
    lane = lax.broadcasted_iota(jnp.int32, (1, HEAD_DIM), 1)
    first_half = (lane % (2 * ROT_FREQS)) < ROT_FREQS

    def rows(r, carry):
        r0 = pl.multiple_of(r * QK_ROWS, QK_ROWS)
        cos = cos_ref[pl.ds(r0, QK_ROWS), :]
        sin = sin_ref[pl.ds(r0, QK_ROWS), :]
        for h in range(IN_TN // HEAD_DIM):
            xh = acc_ref[pl.ds(r0, QK_ROWS), h * HEAD_DIM:(h + 1) * HEAD_DIM]
            ms = jnp.mean(xh * xh, axis=-1, keepdims=True)
            y = xh * lax.rsqrt(ms + EPS) * gain
            partner = jnp.where(first_half,
                                pltpu.roll(y, HEAD_DIM - ROT_FREQS, 1),
                                pltpu.roll(y, ROT_FREQS, 1))
            o_ref[pl.ds(r0, QK_ROWS), h * HEAD_DIM:(h + 1) * HEAD_DIM] = (
                y * cos + partner * sin).astype(BF16)
        return carry

    lax.fori_loop(0, TM // QK_ROWS, rows, 0)


def _cast_epilogue(accs, aux, o_ref, j, i):
    o_ref[...] = accs[0].astype(BF16)


def _gate_epilogue(accs, aux, o_ref, j, i):
    o_ref[...] = jax.nn.sigmoid(accs[0] + aux[0][...]).astype(BF16)


def _inproj(h, w_in, l, gains, cos_t, sin_t, b_gate):
    def rope_map(j, i):
        return (jnp.where(i >= CTX_TILE, SEQ // TM, i % (SEQ // TM)), 0)

    w = [(w_in, (l,))]
    qk = _mm([h], w, [(gains, (None, 8, HEAD_DIM), lambda j, i: (l, 0, 0)),
                      (cos_t, (TM, HEAD_DIM), rope_map), (sin_t, (TM, HEAD_DIM), rope_map)],
             _qk_epilogue, n_out=OFF_V, out_dtype=BF16, tn=IN_TN,
             epilogue_scratch=[pltpu.VMEM((TM, IN_TN), F32)], name="inproj_qk")
    vu = _mm([h], w, [], _cast_epilogue, n_out=OFF_G - OFF_V, out_dtype=BF16, tn=IN_TN_WIDE,
             w_col0=OFF_V // IN_TN_WIDE, name="inproj_vu")
    gates = _mm([h], w, [(b_gate.reshape(DEPTH, 1, 2 * D_MODEL), (None, 1, IN_TN_WIDE),
                          lambda j, i: (l, 0, j))],
                _gate_epilogue, n_out=2 * D_MODEL, out_dtype=BF16, tn=IN_TN_WIDE,
                w_col0=OFF_G // IN_TN_WIDE, name="inproj_gate")
    return qk, vu, gates


MG_TN = 512


def _merge_epilogue(accs, aux, o_ref, j, i):
    ga_ref, gb_ref = aux
    o_ref[...] = (ga_ref[...].astype(F32) * accs[0] + gb_ref[...].astype(F32) * accs[1]).astype(BF16)


def _merge(ya, yb, gates, w_br_a, w_br_b, l):
    aux = [
        (gates, (TM, MG_TN), lambda j, i: (i, j)),
        (gates, (TM, MG_TN), lambda j, i: (i, D_MODEL // MG_TN + j)),
    ]
    return _mm([ya, yb], [(w_br_a, (l,)), (w_br_b, (l,))], aux, _merge_epilogue,
               n_out=D_MODEL, out_dtype=BF16, tn=MG_TN, split_ctx=True, name="merge")


RS_TN = 512
FFD_TM = 512


def _resid_epilogue(accs, aux, o_ref, j, i, tm):
    x_ref, g_ref = aux
    o_ref[...] = x_ref[...] + g_ref[pl.ds(_row_seg(i, tm), 1), :] * accs[0]


def _resid_proj(a, w, prefix, x, mods, l, which, tm=TM, name="resid"):
    nb = D_MODEL // RS_TN
    aux = [
        (x, (tm, RS_TN), lambda j, i: (i, j)),
        (mods, (None, 8, RS_TN), lambda j, i: (l, 0, which * nb + j)),
    ]
    return _mm([a], [(w, prefix)], aux, functools.partial(_resid_epilogue, tm=tm),
               n_out=D_MODEL, out_dtype=F32, tm=tm, tn=RS_TN, alias_aux=0, name=name)


FF_TN = 512


def _swiglu_epilogue(accs, aux, o_ref, j, i):
    a = accs[0]
    o_ref[...] = (a * jax.nn.sigmoid(a) * accs[1]).astype(BF16)


def _swiglu_up(h, w1, w3, prefix, group=None):
    kw = dict(n_rows=MOE_P, tm=MOE_TM) if group is not None else {}
    return _mm([h], [(w1, prefix), (w3, prefix)], [], _swiglu_epilogue, n_out=D_FF,
               out_dtype=BF16, tn=FF_TN, a_of_w=(0, 0), group=group, name="swiglu_up", **kw)


MOE_TM = 512
MOE_A = TOP_K * R_ALL
MOE_NT = MOE_A // MOE_TM + N_EXPERTS
MOE_P = MOE_NT * MOE_TM
GATHER_ROWS = 256


def _route(logits):
    top_v, top_i = lax.top_k(logits[:, :N_EXPERTS], TOP_K)
    wts = jax.nn.softmax(top_v, axis=-1).reshape(-1)
    e_flat = top_i.reshape(-1).astype(jnp.int32)
    onehot = (e_flat[:, None] == jnp.arange(N_EXPERTS, dtype=jnp.int32)[None, :]).astype(jnp.int32)
    before = jnp.cumsum(onehot, axis=0) - onehot
    rank = jnp.sum(before * onehot, axis=1)
    counts = jnp.sum(onehot, axis=0)
    tiles = (counts + MOE_TM - 1) // MOE_TM
    tile_end = jnp.cumsum(tiles)
    pos = ((tile_end - tiles)[e_flat] * MOE_TM + rank).astype(jnp.int32)
    n_used = tile_end[-1:].astype(jnp.int32)
    tile_ids = jnp.minimum(jnp.arange(MOE_NT, dtype=jnp.int32), n_used[0] - 1)
    tile_expert = jnp.sum((tile_ids[:, None] >= tile_end[None, :]).astype(jnp.int32), axis=1)
    token = jnp.arange(MOE_A, dtype=jnp.int32) // TOP_K
    src_token = jnp.zeros((MOE_P,), jnp.int32).at[pos].set(token)
    return pos, src_token, wts, tile_expert.astype(jnp.int32), n_used


def _gather_body(src_ref, nu_ref, h_ref, o_ref, buf_ref, sem):
    t = pl.program_id(0)
    n_rows = nu_ref[0] * MOE_TM
    slot = t % 2

    def row_copy(slot_, r, src_row):
        return pltpu.make_async_copy(h_ref.at[pl.ds(src_row, 1), :],
                                     buf_ref.at[slot_, pl.ds(r, 1), :], sem.at[slot_])

    def issue(step, slot_):
        def body(r, carry):
            row_copy(slot_, r, src_ref[step * GATHER_ROWS + r]).start()
            return carry

        lax.fori_loop(0, GATHER_ROWS, body, 0, unroll=8)

    @pl.when((t == 0) & (n_rows > 0))
    def _():
        issue(0, 0)

    @pl.when((t + 1 < pl.num_programs(0)) & ((t + 1) * GATHER_ROWS < n_rows))
    def _():
        issue(t + 1, 1 - slot)

    @pl.when(t * GATHER_ROWS < n_rows)
    def _():
        def drain(r, carry):
            row_copy(slot, r, 0).wait()
            return carry

        lax.fori_loop(0, GATHER_ROWS, drain, 0, unroll=8)
        o_ref[...] = buf_ref[slot].astype(BF16)

    @pl.when(t * GATHER_ROWS >= n_rows)
    def _():
        o_ref[...] = jnp.zeros(o_ref.shape, o_ref.dtype)


def _moe_gather(h32, src_token, n_used):
    grid_spec = pltpu.PrefetchScalarGridSpec(
        num_scalar_prefetch=2,
        grid=(MOE_P // GATHER_ROWS,),
        in_specs=[pl.BlockSpec(memory_space=pl.ANY)],
        out_specs=pl.BlockSpec((GATHER_ROWS, D_MODEL), lambda t, *_: (t, 0)),
        scratch_shapes=[pltpu.VMEM((2, GATHER_ROWS, D_MODEL), F32), pltpu.SemaphoreType.DMA((2,))],
    )
    return pl.pallas_call(
        _gather_body,
        grid_spec=grid_spec,
        out_shape=jax.ShapeDtypeStruct((MOE_P, D_MODEL), BF16),
        compiler_params=_cparams(1),
        name="moe_gather",
    )(src_token, n_used, h32)


def _moe_down_epilogue(accs, aux, o_ref, j, i):
    o_ref[...] = accs[0]


def _combine_body(pos_ref, y_ref, w_ref, x_ref, g_ref, o_ref, buf_ref, sem):
    t = pl.program_id(0)
    slot = t % 2

    def row_copy(slot_, k, r, src_row):
        return pltpu.make_async_copy(y_ref.at[pl.ds(src_row, 1), :],
                                     buf_ref.at[slot_, k, pl.ds(r, 1), :], sem.at[slot_])

    def issue(step, slot_):
        def body(r, carry):
            for k in range(TOP_K):
                row_copy(slot_, k, r, pos_ref[TOP_K * (step * TR + r) + k]).start()
            return carry

        lax.fori_loop(0, TR, body, 0, unroll=4)

    @pl.when(t == 0)
    def _():
        issue(0, 0)

    @pl.when(t + 1 < pl.num_programs(0))
    def _():
        issue(t + 1, 1 - slot)

    def drain(r, carry):
        for k in range(TOP_K):
            row_copy(slot, k, r, 0).wait()
        return carry

    lax.fori_loop(0, TR, drain, 0, unroll=4)
    g = g_ref[pl.ds(_row_seg(t, TR), 1), :]
    y = w_ref[0] * buf_ref[slot, 0]
    for k in range(1, TOP_K):
        y = y + w_ref[k] * buf_ref[slot, k]
    o_ref[...] = x_ref[...] + g * y


def _moe_combine(ys, pos, wts, x, mods, l, which):
    w_cols = jnp.broadcast_to(wts.reshape(R_ALL, TOP_K).T[:, :, None], (TOP_K, R_ALL, 1))
    grid_spec = pltpu.PrefetchScalarGridSpec(
        num_scalar_prefetch=1,
        grid=(R_ALL // TR,),
        in_specs=[pl.BlockSpec(memory_space=pl.ANY),
                  pl.BlockSpec((TOP_K, TR, 1), lambda t, *_: (0, t, 0)),
                  pl.BlockSpec((TR, D_MODEL), lambda t, *_: (t, 0)),
                  pl.BlockSpec((None, 8, D_MODEL), lambda t, *_: (l, 0, which))],
        out_specs=pl.BlockSpec((TR, D_MODEL), lambda t, *_: (t, 0)),
        scratch_shapes=[pltpu.VMEM((2, TOP_K, TR, D_MODEL), F32), pltpu.SemaphoreType.DMA((2,))],
    )
    return pl.pallas_call(
        _combine_body,
        grid_spec=grid_spec,
        out_shape=jax.ShapeDtypeStruct((R_ALL, D_MODEL), F32),
        input_output_aliases={3: 0},
        compiler_params=_cparams(1),
        name="moe_combine",
    )(pos, ys, w_cols, x, mods)


def _moe(h32, logits, xs, mods, l, i, moe_w1, moe_w3, moe_w2):
    pos, src_token, wts, tile_expert, n_used = _route(logits)
    group = (tile_expert, n_used)
    hs = _moe_gather(h32, src_token, n_used)
    g = _swiglu_up(hs, moe_w1, moe_w3, (i, 0), group=group)
    ys = _mm([g], [(moe_w2, (i, 0))], [], _moe_down_epilogue, n_out=D_MODEL, out_dtype=F32,
             tn=RS_TN, n_rows=MOE_P, tm=MOE_TM, group=group, name="moe_down")
    return _moe_combine(ys, pos, wts, xs, mods, l, 5)


AT_TQ = 512
AT_TK = 1024


ATT_SAFE_BOUND = 60.0


def _attn_scores(qs, k_ref, sl, c0, c1):
    return lax.dot_general(qs, k_ref[c0:c1, sl], (((1,), (1,)), ((), ())),
                           preferred_element_type=F32)


def _attn_body(lam_ref, q_ref, *rest, n_seg, post_scale):
    k_refs = rest[:n_seg]
    v_refs = rest[n_seg:2 * n_seg]
    sub_ref = rest[2 * n_seg]
    o_ref = rest[2 * n_seg + 1]
    kmax_ref = rest[2 * n_seg + 2]
    num_ref = rest[2 * n_seg + 3]
    lam = lam_ref[0]
    subs = [slice(s * HEAD_DIM, (s + 1) * HEAD_DIM) for s in range(2)]

    @pl.when(pl.program_id(2) == 0)
    def _():
        for s, sl in enumerate(subs):
            best = None
            for k_ref in k_refs:
                kf = k_ref[:, sl].astype(F32)
                n2 = jnp.max(jnp.sum(kf * kf, axis=-1, keepdims=True))
                best = n2 if best is None else jnp.maximum(best, n2)
            kmax_ref[s] = best

    qs = [q_ref[:, sl] for sl in subs]
    bounds = []
    for s in range(2):
        qf = qs[s].astype(F32)
        qn2 = jnp.sum(qf * qf, axis=-1, keepdims=True)
        bounds.append(jnp.sqrt(qn2 * kmax_ref[s]) * 1.001 + 1e-6)
    worst = jnp.maximum(jnp.max(bounds[0]), jnp.max(bounds[1]))

    @pl.when(worst <= ATT_SAFE_BOUND)
    def _():
        for s, sl in enumerate(subs):
            den = num = None
            for k_ref, v_ref in zip(k_refs, v_refs):
                p = jnp.exp2(_attn_scores(qs[s], k_ref, sl, 0, k_ref.shape[0]) - bounds[s])
                ps = p.sum(axis=-1, keepdims=True)
                pv = jnp.dot(p.astype(BF16), v_ref[...], preferred_element_type=F32)
                den = ps if den is None else den + ps
                num = pv if num is None else num + pv
            num_ref[s] = num / den

    @pl.when(worst > ATT_SAFE_BOUND)
    def _():
        for s, sl in enumerate(subs):
            m = den = num = None
            for k_ref, v_ref in zip(k_refs, v_refs):
                seg_len = k_ref.shape[0]
                for c0 in range(0, seg_len, AT_TK):
                    c1 = min(c0 + AT_TK, seg_len)
                    sc = _attn_scores(qs[s], k_ref, sl, c0, c1)
                    mc = sc.max(axis=-1, keepdims=True)
                    m_new = mc if m is None else jnp.maximum(m, mc)
                    p = jnp.exp2(sc - m_new)
                    ps = p.sum(axis=-1, keepdims=True)
                    pv = jnp.dot(p.astype(BF16), v_ref[c0:c1, :], preferred_element_type=F32)
                    if m is None:
                        den, num = ps, pv
                    else:
                        alpha = jnp.exp2(m - m_new)
                        den = alpha * den + ps
                        num = alpha * num + pv
                    m = m_new
            num_ref[s] = num / den

    d = num_ref[0] - lam * num_ref[1]
    ms = jnp.mean(d * d, axis=-1, keepdims=True)
    o_ref[...] = (d * lax.rsqrt(ms + EPS) * sub_ref[...] * post_scale).astype(BF16)


def _attention(qk, vu, lam, sub_norm, l, lam_init, ctx_only=False):
    kb = OFF_K // V_HEAD_DIM
    ctx_blk0 = R_LAT // CTX_LEN
    if ctx_only:
        tq = CTX_LEN
        grid = (BATCH, N_DIFF_HEADS, 1)
        q_map = lambda b, h, t: (ctx_blk0 + b, h)
        o_map = lambda b, h, t: (b, h)
        segs = [CTX_LEN]
    else:
        tq = AT_TQ
        grid = (BATCH, N_DIFF_HEADS, SEQ // tq)
        q_map = o_map = lambda b, h, t: (b * (SEQ // tq) + t, h)
        segs = [CTX_LEN, SEQ]
    in_specs = [pl.BlockSpec(memory_space=pltpu.SMEM),
                pl.BlockSpec((tq, V_HEAD_DIM), q_map)]
    args = [lam.reshape(1), qk]
    for arr, off in ((qk, kb), (vu, 0)):
        for seg_len in segs:
            if seg_len == CTX_LEN:
                imap = functools.partial(lambda b, h, t, o: (ctx_blk0 + b, o + h), o=off)
            else:
                imap = functools.partial(lambda b, h, t, o: (b, o + h), o=off)
            in_specs.append(pl.BlockSpec((seg_len, V_HEAD_DIM), imap))
            args.append(arr)
    in_specs.append(pl.BlockSpec((None, 1, V_HEAD_DIM), lambda b, h, t: (l, 0, 0)))
    args.append(sub_norm.reshape(DEPTH, 1, V_HEAD_DIM))
    return pl.pallas_call(
        functools.partial(_attn_body, n_seg=len(segs), post_scale=1.0 - lam_init),
        grid=grid,
        in_specs=in_specs,
        out_specs=pl.BlockSpec((tq, V_HEAD_DIM), o_map),
        out_shape=jax.ShapeDtypeStruct((R_CTX if ctx_only else R_LAT, ATT_W), BF16),
        scratch_shapes=[pltpu.SMEM((2,), F32), pltpu.VMEM((2, tq, V_HEAD_DIM), F32)],
        compiler_params=_cparams(3),
        name="attn_ctx" if ctx_only else "attn",
    )(*args)


def _dft_tables():
    n, n1, n2 = FFT_N, FFT_N1, FFT_N2
    k1 = np.arange(n1)[:, None]
    tabs_d, tabs_s, tabs_i = [], [], []
    for b in range(n2):
        cols = n2 * np.arange(n1)[None, :] + b
        g = np.exp(-2j * np.pi * ((k1 * cols) % n) / n)
        gd = g[:, :n1 // 2]
        tabs_d.append(np.block([[gd.real, -gd.imag], [gd.imag, gd.real]]))
        tabs_s.append(np.concatenate([g.real, g.imag], axis=0))
        gi = np.conj(gd).T / n1
        tabs_i.append(np.block([[gi.real, -gi.imag], [gi.imag, gi.real]]))
    kk = np.arange(n2)
    f2 = np.exp(-2j * np.pi * ((kk[:, None] * kk[None, :]) % n2) / n2)
    g2 = np.block([[f2.real, -f2.imag], [f2.imag, f2.real]])
    f2i = np.conj(f2) / n2
    g2i = np.block([[f2i.real, -f2i.imag], [f2i.imag, f2i.real]])
    as_bf = lambda a: jnp.asarray(np.asarray(a, np.float32)).astype(BF16)
    return (as_bf(np.stack(tabs_d)), as_bf(np.stack(tabs_s)), as_bf(np.stack(tabs_i)),
            as_bf(g2), as_bf(g2i))


FFT_UNROLL = 32


def _pack_pair(a, b):
    ab = lax.bitcast_convert_type(a.astype(BF16).astype(F32), jnp.uint32)
    bb = lax.bitcast_convert_type(b.astype(BF16).astype(F32), jnp.uint32)
    return lax.bitcast_convert_type(ab | (bb >> 16), F32)


def _unpack_pair(w):
    bits = lax.bitcast_convert_type(w, jnp.uint32)
    hi = lax.bitcast_convert_type(bits & jnp.uint32(0xFFFF0000), F32)
    lo = lax.bitcast_convert_type(bits << 16, F32)
    return hi, lo


def _spectrum_body(hf_ref, hr_ref, wf_ref, wb_ref, dl_ref, g1_ref, g2_ref, or_ref, oi_ref,
                   k_ref, a_ref):
    n1, n2 = FFT_N1, FFT_N2
    delta = dl_ref[...]
    step = 1.0 / (SEQ - 1)
    rows = 512
    row_iota = lax.broadcasted_iota(jnp.int32, (rows, 1), 0)

    def taps(c, carry):
        r0 = pl.multiple_of(c * rows, rows)
        pos = (row_iota + r0).astype(F32)
        top = jnp.dot(hf_ref[pl.ds(r0, rows), :].astype(BF16), wf_ref[...].astype(BF16),
                      preferred_element_type=F32)
        bot = jnp.dot(hr_ref[pl.ds(r0, rows), :].astype(BF16), wb_ref[...].astype(BF16),
                      preferred_element_type=F32)
        k_ref[pl.ds(r0, rows), :] = top * jnp.exp(-(pos * step) * delta)
        k_ref[pl.ds(SEQ + r0, rows), :] = bot * jnp.exp(-((SEQ - pos) * step) * delta)
        return carry

    lax.fori_loop(0, SEQ // rows, taps, 0)

    def stage1(b, carry):
        x = k_ref[pl.ds(b, n1, stride=n2), :].astype(BF16)
        a = jnp.dot(g1_ref[b], x, preferred_element_type=F32)
        row = pl.multiple_of(b * n1, n1)
        a_ref[pl.ds(row, n1), :] = _pack_pair(a[:n1], a[n1:])
        return carry

    lax.fori_loop(0, n2, stage1, 0, unroll=FFT_UNROLL)

    def stage2(k1, carry):
        a = jnp.concatenate(_unpack_pair(a_ref[pl.ds(k1, n2, stride=n1), :]), axis=0).astype(BF16)
        x = jnp.dot(g2_ref[...], a, preferred_element_type=F32)
        row = pl.multiple_of(k1 * n2, n2)
        or_ref[pl.ds(row, n2), :] = x[:n2].astype(BF16)
        oi_ref[pl.ds(row, n2), :] = x[n2:].astype(BF16)
        return carry

    lax.fori_loop(0, n1, stage2, 0, unroll=FFT_UNROLL)


def _const_spec(shape, n_grid):
    nd = len(shape)
    imap = (lambda a, b: (0,) * nd) if n_grid == 2 else (lambda a: (0,) * nd)
    return pl.BlockSpec(shape, imap, pipeline_mode=pl.Buffered(1))


def _spectrum(hid_f, hid_r, w4p, deltas, l, g1s, g2):
    tc = HY_TC
    nct = HY_W // tc
    hspec = pl.BlockSpec((SEQ, LANES), lambda f, ct: (0, 0))
    wspec = lambda d: pl.BlockSpec(
        (None, LANES, tc), functools.partial(lambda f, ct, d: (l, 0, (2 * f + d) * nct + ct), d=d))
    ospec = pl.BlockSpec((None, FFT_N, tc), lambda f, ct: (f, 0, ct))
    return pl.pallas_call(
        _spectrum_body,
        grid=(HY_N_FILT, nct),
        in_specs=[hspec, hspec, wspec(0), wspec(1),
                  pl.BlockSpec((1, tc), lambda f, ct: (0, ct)),
                  _const_spec((FFT_N2, 2 * FFT_N1, FFT_N1), 2),
                  _const_spec((2 * FFT_N2, 2 * FFT_N2), 2)],
        out_specs=[ospec, ospec],
        out_shape=[jax.ShapeDtypeStruct((HY_N_FILT, FFT_N, HY_W), BF16)] * 2,
        scratch_shapes=[pltpu.VMEM((FFT_N, tc), F32)] * 2,
        compiler_params=_cparams(2),
        name="hy_spectrum",
    )(hid_f, hid_r, w4p, w4p, deltas, g1s, g2)


def _conv3(u, w_ref, b_ref):
    n = u.shape[0]
    row = lax.broadcasted_iota(jnp.int32, u.shape, 0)
    prev = jnp.where(row == 0, 0.0, pltpu.roll(u, 1, 0))
    nxt = jnp.where(row == n - 1, 0.0, pltpu.roll(u, n - 1, 0))
    return prev * w_ref[0:1, :] + u * w_ref[1:2, :] + nxt * w_ref[2:3, :] + b_ref[...]


def _hyconv_body(s0_ref, s1_ref, m0_ref, m1_ref, cws_ref, cbs_ref, cwm_ref, cbm_ref, bias_ref,
                 kr_ref, ki_ref, g1_ref, g1i_ref, g2_ref, g2i_ref, o_ref,
                 vr_ref, vi_ref, vp_ref, a_ref, b_ref, *, conv_signal):
    n1, n2 = FFT_N1, FFT_N2
    h1 = n1 // 2
    for s_ref, v_ref in ((s0_ref, vr_ref), (s1_ref, vi_ref)):
        u = s_ref[...].astype(F32)
        v_ref[...] = _conv3(u, cws_ref, cbs_ref) if conv_signal else u
    vp_ref[...] = _pack_pair(vr_ref[...], vi_ref[...])

    def stage1(b, carry):
        x = jnp.concatenate(_unpack_pair(vp_ref[pl.ds(b, h1, stride=n2), :]), axis=0).astype(BF16)
        a = jnp.dot(g1_ref[b], x, preferred_element_type=F32)
        row = pl.multiple_of(b * n1, n1)
        a_ref[pl.ds(row, n1), :] = _pack_pair(a[:n1], a[n1:])
        return carry

    lax.fori_loop(0, n2, stage1, 0, unroll=FFT_UNROLL)

    def stage2(k1, carry):
        a = jnp.concatenate(_unpack_pair(a_ref[pl.ds(k1, n2, stride=n1), :]), axis=0).astype(BF16)
        x = jnp.dot(g2_ref[...], a, preferred_element_type=F32)
        row = pl.multiple_of(k1 * n2, n2)
        fr = kr_ref[pl.ds(row, n2), :].astype(F32)
        fi = ki_ref[pl.ds(row, n2), :].astype(F32)
        xr, xi = x[:n2], x[n2:]
        y = jnp.concatenate([xr * fr - xi * fi, xr * fi + xi * fr], axis=0).astype(BF16)
        bq = jnp.dot(g2i_ref[...], y, preferred_element_type=F32)
        b_ref[pl.ds(row, n2), :] = _pack_pair(bq[:n2], bq[n2:])
        return carry

    lax.fori_loop(0, n1, stage2, 0, unroll=FFT_UNROLL)

    def stage1_inv(b, carry):
        bq = jnp.concatenate(_unpack_pair(b_ref[pl.ds(b, n1, stride=n2), :]), axis=0).astype(BF16)
        y = jnp.dot(g1i_ref[b], bq, preferred_element_type=F32)
        row = pl.multiple_of(b * h1, h1)
        a_ref[pl.ds(row, h1), :] = _pack_pair(y[:h1], y[h1:])
        return carry

    lax.fori_loop(0, n2, stage1_inv, 0, unroll=FFT_UNROLL)

    bias = bias_ref[...]

    def unpermute(i, carry):
        row = pl.multiple_of(i * n2, n2)
        yr, yi = _unpack_pair(a_ref[pl.ds(i, n2, stride=h1), :])
        vr_ref[pl.ds(row, n2), :] = yr + bias * vr_ref[pl.ds(row, n2), :]
        vi_ref[pl.ds(row, n2), :] = yi + bias * vi_ref[pl.ds(row, n2), :]
        return carry

    lax.fori_loop(0, h1, unpermute, 0, unroll=FFT_UNROLL)

    for half, (m_ref, v_ref) in enumerate(((m0_ref, vr_ref), (m1_ref, vi_ref))):
        mult = _conv3(m_ref[...].astype(F32), cwm_ref, cbm_ref)
        o_ref[half * SEQ:(half + 1) * SEQ, :] = (mult * v_ref[...]).astype(o_ref.dtype)


def _hy_param_specs(conv_w, conv_b, bias, l, filt, sig_ch, mul_ch, tc):
    cw3 = conv_w.reshape(DEPTH, 3, 3 * HY_W)
    cb3 = conv_b.reshape(DEPTH, 1, 3 * HY_W)
    chan = lambda ch0, rows: pl.BlockSpec(
        (None, rows, tc), functools.partial(lambda ct, pr, c0: (l, 0, c0 + ct), c0=ch0))
    specs = [chan(sig_ch, 3), chan(sig_ch, 1), chan(mul_ch, 3), chan(mul_ch, 1),
             pl.BlockSpec((None, None, 1, tc), lambda ct, pr: (l, filt, 0, ct))]
    args = [cw3, cb3, cw3, cb3, bias.reshape(DEPTH, HY_N_FILT, 1, HY_W)]
    return specs, args


def _hyconv(sig, sig_cb, mul, mul_cb, conv_w, conv_b, bias, l, filt, sig_ch, mul_ch,
            kr, ki, tabs, conv_signal):
    g1d, g1i, g2, g2i = tabs
    tc = HY_TC
    blk = lambda arr_cb, odd: pl.BlockSpec(
        (SEQ, tc), functools.partial(lambda ct, pr, cb, o: (2 * pr + o, cb + ct), cb=arr_cb, o=odd))
    pspecs, pargs = _hy_param_specs(conv_w, conv_b, bias, l, filt, sig_ch, mul_ch, tc)
    kspec = pl.BlockSpec((None, FFT_N, tc), lambda ct, pr: (filt, 0, ct))
    in_specs = [blk(sig_cb, 0), blk(sig_cb, 1), blk(mul_cb, 0), blk(mul_cb, 1)] + pspecs + [
        kspec, kspec,
        _const_spec((FFT_N2, 2 * FFT_N1, FFT_N1), 2),
        _const_spec((FFT_N2, FFT_N1, 2 * FFT_N1), 2),
        _const_spec((2 * FFT_N2, 2 * FFT_N2), 2),
        _const_spec((2 * FFT_N2, 2 * FFT_N2), 2),
    ]
    return pl.pallas_call(
        functools.partial(_hyconv_body, conv_signal=conv_signal),
        grid=(HY_W // tc, BATCH // 2),
        in_specs=in_specs,
        out_specs=pl.BlockSpec((2 * SEQ, tc), lambda ct, pr: (pr, ct)),
        out_shape=jax.ShapeDtypeStruct((R_LAT, HY_W), BF16),
        scratch_shapes=[pltpu.VMEM((SEQ, tc), F32)] * 3 + [pltpu.VMEM((FFT_N, tc), F32)] * 2,
        compiler_params=_cparams(2),
        name="hyconv",
    )(sig, sig, mul, mul, *pargs, kr, ki, g1d, g1i, g2, g2i)


def _ctx_dft_tables():
    n = 2 * CTX_LEN
    kk = np.arange(n)
    f = np.exp(-2j * np.pi * ((kk[:, None] * kk[None, :]) % n) / n)
    fd = f[:, :CTX_LEN]
    fwd = np.block([[fd.real, -fd.imag], [fd.imag, fd.real]])
    spec = np.concatenate([f.real, f.imag], axis=0)
    fi = np.conj(f)[:CTX_LEN, :] / n
    inv = np.block([[fi.real, -fi.imag], [fi.imag, fi.real]])
    as_bf = lambda a: jnp.asarray(np.asarray(a, np.float32)).astype(BF16)
    return as_bf(fwd), as_bf(spec), as_bf(inv)


def _hyconv_ctx_body(s0_ref, s1_ref, m0_ref, m1_ref, cws_ref, cbs_ref, cwm_ref, cbm_ref, bias_ref,
                     k_ref, fwd_ref, spec_ref, inv_ref, o_ref, *, conv_signal):
    n = 2 * CTX_LEN
    vs = []
    for s_ref in (s0_ref, s1_ref):
        u = s_ref[...].astype(F32)
        vs.append(_conv3(u, cws_ref, cbs_ref) if conv_signal else u)
    x = jnp.dot(fwd_ref[...], jnp.concatenate(vs, axis=0).astype(BF16), preferred_element_type=F32)
    kf = jnp.dot(spec_ref[...], k_ref[...].astype(BF16), preferred_element_type=F32)
    kf = kf.astype(BF16).astype(F32)
    xr, xi, fr, fi = x[:n], x[n:], kf[:n], kf[n:]
    y = jnp.concatenate([xr * fr - xi * fi, xr * fi + xi * fr], axis=0).astype(BF16)
    conv = jnp.dot(inv_ref[...], y, preferred_element_type=F32)
    bias = bias_ref[...]
    for half, m_ref in enumerate((m0_ref, m1_ref)):
        sl = slice(half * CTX_LEN, (half + 1) * CTX_LEN)
        mult = _conv3(m_ref[...].astype(F32), cwm_ref, cbm_ref)
        o_ref[sl, :] = (mult * (conv[sl] + bias * vs[half])).astype(o_ref.dtype)


def _hyconv_ctx(sig, sig_rb, sig_cb, mul, mul_rb, mul_cb, conv_w, conv_b, bias, l, filt,
                sig_ch, mul_ch, kctx, tabs, conv_signal):
    fwd, spec, inv = tabs
    tc = HY_TC
    blk = lambda rb, arr_cb, odd: pl.BlockSpec(
        (CTX_LEN, tc),
        functools.partial(lambda ct, pr, rb, cb, o: (rb + 2 * pr + o, cb + ct), rb=rb, cb=arr_cb, o=odd))
    pspecs, pargs = _hy_param_specs(conv_w, conv_b, bias, l, filt, sig_ch, mul_ch, tc)
    n = 2 * CTX_LEN
    in_specs = [blk(sig_rb, sig_cb, 0), blk(sig_rb, sig_cb, 1),
                blk(mul_rb, mul_cb, 0), blk(mul_rb, mul_cb, 1)] + pspecs + [
        pl.BlockSpec((None, n, tc), lambda ct, pr: (filt, 0, ct)),
        _const_spec((2 * n, n), 2),
        _const_spec((2 * n, n), 2),
        _const_spec((n, 2 * n), 2),
    ]
    return pl.pallas_call(
        functools.partial(_hyconv_ctx_body, conv_signal=conv_signal),
        grid=(HY_W // tc, BATCH // 2),
        in_specs=in_specs,
        out_specs=pl.BlockSpec((2 * CTX_LEN, tc), lambda ct, pr: (pr, ct)),
        out_shape=jax.ShapeDtypeStruct((R_CTX, HY_W), BF16),
        compiler_params=_cparams(2),
        name="hyconv_ctx",
    )(sig, sig, mul, mul, *pargs, kctx, fwd, spec, inv)


def _rope_tables():
    rows = SEQ // GRID_W
    row = jnp.repeat(jnp.arange(rows), GRID_W).astype(F32)
    col = jnp.tile(jnp.arange(GRID_W), rows).astype(F32)
    inv = ROPE_BASE ** (-jnp.arange(ROT_FREQS, dtype=F32) / ROT_FREQS)
    ar, ac = row[:, None] * inv, col[:, None] * inv
    cos = jnp.concatenate([jnp.cos(ar), jnp.cos(ar), jnp.cos(ac), jnp.cos(ac)], axis=1)
    sin = jnp.concatenate([-jnp.sin(ar), jnp.sin(ar), -jnp.sin(ac), jnp.sin(ac)], axis=1)
    cos = jnp.concatenate([cos, jnp.ones((TM, HEAD_DIM), F32)], axis=0)
    sin = jnp.concatenate([sin, jnp.zeros((TM, HEAD_DIM), F32)], axis=0)
    return cos, sin


def _hyena_positions(L, reverse):
    idx = jnp.arange(L, dtype=F32)
    return (L - idx) if reverse else idx


def _hyena_hidden(L, w1, b1, w2, b2, w3, b3, freq, reverse=False):
    pos = _hyena_positions(L, reverse)[:, None]
    t = pos / (L - 1)
    w = 2.0 * math.pi * pos / L
    f = jnp.linspace(1e-4, HY_BANDS - 1, HY_BANDS, dtype=F32)
    z = jnp.concatenate([t, jnp.cos(w * f), -jnp.sin(w * f)], axis=-1)
    dot = functools.partial(jnp.dot, precision=HIGHEST)
    hid = jnp.sin(freq * (dot(z, w1) + b1))
    hid = jnp.sin(freq * (dot(hid, w2) + b2))
    hid = jnp.sin(freq * (dot(hid, w3) + b3))
    if reverse:
        hid = jnp.where(jnp.arange(L)[:, None] > 0, hid, 0.0)
    return hid


def _hyena_deltas():
    return jnp.abs(jnp.linspace(math.log(HY_TARGET) / HY_SLOW_PCT,
                                math.log(HY_TARGET) / HY_FAST_PCT, HY_W, dtype=F32))


def _hyena_filters_ctx(hid, hid_rev, w4):
    L = hid.shape[0]
    w4 = w4.reshape(HY_FH, HY_N_FILT, 2, HY_W)
    halves = []
    for d, feats in enumerate((hid, hid_rev)):
        t = _hyena_positions(L, bool(d))[:, None] / (L - 1)
        taps = jnp.einsum('lh,hfc->flc', feats, w4[:, :, d], precision=HIGHEST)
        halves.append(taps * jnp.exp(-t * _hyena_deltas())[None])
    return jnp.concatenate(halves, axis=1)


def kernel(x, c, ctx, c_ctx, w_ada, b_ada, norm1, norm2, w_in, b_gate, q_norm, k_norm, lam_q1, lam_k1, lam_q2, lam_k2, sub_norm, hy_conv_w, hy_conv_b, hy_w1, hy_b1, hy_w2, hy_b2, hy_w3, hy_b3, hy_w4, hy_freq, hy_bias, w_br_a, w_br_b, w_out, ffn_w1, ffn_w3, ffn_w2, router, moe_w1, moe_w3, moe_w2):
    xs = jnp.concatenate([x.reshape(R_LAT, D_MODEL), ctx.reshape(R_CTX, D_MODEL)], axis=0)
    cs = jnp.concatenate([c, c_ctx[None], jnp.zeros((8 - BATCH - 1, D_MODEL), F32)], axis=0)
    mods = _adaln(cs, w_ada, b_ada)
    cos_t, sin_t = _rope_tables()
    g1d, g1s, g1i, g2, g2i = _dft_tables()
    ctx_tabs = _ctx_dft_tables()
    gains = jnp.zeros((DEPTH, 8, HEAD_DIM), F32)
    gains = gains.at[:, 0].set(q_norm * (math.log2(math.e) / math.sqrt(HEAD_DIM))).at[:, 1].set(k_norm)
    router_pad = jnp.zeros((router.shape[0], D_MODEL, LANES), F32).at[:, :, :N_EXPERTS].set(router)
    hy_cb = ATT_W // HY_TC
    ch = HY_W // HY_TC
    w4p = jnp.pad(hy_w4, ((0, 0), (0, LANES - HY_FH), (0, 0)))
    deltas = _hyena_deltas()[None]

    for l in range(DEPTH):
        lam_init = 0.8 - 0.6 * math.exp(-0.3 * l)
        lam = (jnp.exp(jnp.sum(lam_q1[l] * lam_k1[l])) - jnp.exp(jnp.sum(lam_q2[l] * lam_k2[l]))
               + lam_init)
        h = _normmod(xs, norm1, mods, l, 0)
        qk, vu, gates = _inproj(h, w_in, l, gains, cos_t, sin_t, b_gate)

        ya = (_attention(qk, vu, lam, sub_norm, l, lam_init),
              _attention(qk, vu, lam, sub_norm, l, lam_init, ctx_only=True))

        hp = (hy_w1[l], hy_b1[l], hy_w2[l], hy_b2[l], hy_w3[l], hy_b3[l], hy_freq[l])
        lane_pad = ((0, 0), (0, LANES - HY_FH))
        hid = jnp.pad(_hyena_hidden(SEQ, *hp), lane_pad)
        hid_rev = jnp.pad(_hyena_hidden(SEQ, *hp, reverse=True), lane_pad)
        kr, ki = _spectrum(hid, hid_rev, w4p, deltas, l, g1s, g2)
        kctx = _hyena_filters_ctx(_hyena_hidden(CTX_LEN, *hp),
                                  _hyena_hidden(CTX_LEN, *hp, reverse=True), hy_w4[l])
        common = (hy_conv_w, hy_conv_b, hy_bias, l)
        tabs = (g1d, g1i, g2, g2i)
        crb = R_LAT // CTX_LEN
        z = _hyconv(vu, hy_cb, vu, hy_cb + ch, *common, 0, 0, ch, kr, ki, tabs, True)
        zc = _hyconv_ctx(vu, crb, hy_cb, vu, crb, hy_cb + ch, *common, 0, 0, ch, kctx, ctx_tabs, True)
        yb = (_hyconv(z, 0, vu, hy_cb + 2 * ch, *common, 1, 0, 2 * ch, kr, ki, tabs, False),
              _hyconv_ctx(zc, 0, 0, vu, crb, hy_cb + 2 * ch, *common, 1, 0, 2 * ch, kctx, ctx_tabs, False))

        m = _merge(ya, yb, gates, w_br_a, w_br_b, l)
        xs = _resid_proj(m, w_out, (l,), xs, mods, l, 2, name="out_proj")

        i = l // 2
        if l % 2 == 0:
            h2 = _normmod(xs, norm2, mods, l, 3)
            g = _swiglu_up(h2, ffn_w1, ffn_w3, (i,))
            xs = _resid_proj(g, ffn_w2, (i,), xs, mods, l, 5, tm=FFD_TM, name="ffn_down")
        else:
            h32, logits = _normmod(xs, norm2, mods, l, 3, router=router_pad[i])
            xs = _moe(h32, logits, xs, mods, l, i, moe_w1, moe_w3, moe_w2)
    return xs[:R_LAT].reshape(BATCH, SEQ, D_MODEL)
```

```python
import functools
import math

import numpy as np
import jax
import jax.numpy as jnp
from jax import lax
from jax.experimental import pallas as pl
from jax.experimental.pallas import tpu as pltpu

D_MODEL = 2048
BATCH = 4
SEQ = 4096
DEPTH = 4
CTX_LEN = 256
GRID_W = 64
HEAD_DIM = 128
N_DIFF_HEADS = D_MODEL // (2 * HEAD_DIM)
V_HEAD_DIM = 2 * HEAD_DIM
ATT_W = N_DIFF_HEADS * V_HEAD_DIM
HY_W = D_MODEL
OFF_K = ATT_W
OFF_V = 2 * ATT_W
OFF_HY = 3 * ATT_W
OFF_G = 3 * ATT_W + 3 * HY_W
IN_COLS = OFF_G + 2 * D_MODEL
ROPE_BASE = 10000.0
ROT_FREQS = HEAD_DIM // 4
HY_EMB = 33
HY_BANDS = (HY_EMB - 1) // 2
HY_FH = 64
HY_N_FILT = 2
HY_TARGET = 1e-2
HY_FAST_PCT = 0.3
HY_SLOW_PCT = 1.5
D_FF = 5632
N_EXPERTS = 8
TOP_K = 2
EPS = 1e-6

R_LAT = BATCH * SEQ
R_CTX = BATCH * CTX_LEN
R_ALL = R_LAT + R_CTX
TM = 1024
CTX_TILE = R_LAT // TM
CTX_SEG = BATCH
TR = 512
LANES = 128
VMEM_LIMIT = 56 * 1024 * 1024

FFT_N = 2 * SEQ
FFT_N1 = 128
FFT_N2 = 64
HY_TC = 128

F32 = jnp.float32
BF16 = jnp.bfloat16
HIGHEST = lax.Precision.HIGHEST


def _cparams(n_axes):
    return pltpu.CompilerParams(
        dimension_semantics=("arbitrary",) * n_axes, vmem_limit_bytes=VMEM_LIMIT)


def _row_seg(i, tm=TM):
    return jnp.where(i >= R_LAT // tm, CTX_SEG, i // (SEQ // tm))


def _adaln_body(c_ref, w_ref, b_ref, o_ref):
    s = c_ref[...]
    s = s * jax.nn.sigmoid(s)
    o_ref[...] = jnp.dot(s, w_ref[...], preferred_element_type=F32, precision=HIGHEST) + b_ref[...]


def _adaln(cs, w_ada, b_ada):
    tn = 1536
    n = 6 * D_MODEL
    return pl.pallas_call(
        _adaln_body,
        grid=(DEPTH, n // tn),
        in_specs=[
            pl.BlockSpec((8, D_MODEL), lambda l, j: (0, 0)),
            pl.BlockSpec((None, D_MODEL, tn), lambda l, j: (l, 0, j)),
            pl.BlockSpec((None, 1, tn), lambda l, j: (l, 0, j)),
        ],
        out_specs=pl.BlockSpec((None, 8, tn), lambda l, j: (l, 0, j)),
        out_shape=jax.ShapeDtypeStruct((DEPTH, 8, n), F32),
        compiler_params=_cparams(2),
        name="adaln",
    )(cs, w_ada, b_ada.reshape(DEPTH, 1, n))


def _normmod_body(x_ref, g_ref, sh_ref, sc_ref, *rest, with_router):
    i = pl.program_id(0)
    seg = jnp.where(i >= R_LAT // TR, CTX_SEG, i // (SEQ // TR))
    x = x_ref[...]
    ms = jnp.mean(x * x, axis=-1, keepdims=True)
    y = x * lax.rsqrt(ms + EPS) * g_ref[...]
    h = y * (1.0 + sc_ref[pl.ds(seg, 1), :]) + sh_ref[pl.ds(seg, 1), :]
    if with_router:
        r_ref, o_ref, lg_ref = rest
        lg_ref[...] = jnp.dot(h, r_ref[...], preferred_element_type=F32, precision=HIGHEST)
    else:
        (o_ref,) = rest
    o_ref[...] = h.astype(o_ref.dtype)


def _normmod(x, gain, mods, l, which, router=None):
    in_specs = [
        pl.BlockSpec((TR, D_MODEL), lambda i: (i, 0)),
        pl.BlockSpec((None, 1, D_MODEL), lambda i: (l, 0, 0)),
        pl.BlockSpec((None, 8, D_MODEL), lambda i: (l, 0, which)),
        pl.BlockSpec((None, 8, D_MODEL), lambda i: (l, 0, which + 1)),
    ]
    args = [x, gain.reshape(DEPTH, 1, D_MODEL), mods, mods]
    out_specs = pl.BlockSpec((TR, D_MODEL), lambda i: (i, 0))
    out_shape = jax.ShapeDtypeStruct((R_ALL, D_MODEL), BF16 if router is None else F32)
    if router is not None:
        in_specs.append(pl.BlockSpec((D_MODEL, LANES), lambda i: (0, 0)))
        args.append(router)
        out_specs = [out_specs, pl.BlockSpec((TR, LANES), lambda i: (i, 0))]
        out_shape = [out_shape, jax.ShapeDtypeStruct((R_ALL, LANES), F32)]
    return pl.pallas_call(
        functools.partial(_normmod_body, with_router=router is not None),
        grid=(R_ALL // TR,),
        in_specs=in_specs,
        out_specs=out_specs,
        out_shape=out_shape,
        compiler_params=_cparams(1),
        name="normmod",
    )(*args)


def _mm_body(*refs, n_pref, n_a, n_w, n_aux, epilogue, a_of_w, split_ctx):
    pref = refs[:n_pref]
    refs = refs[n_pref:]
    if split_ctx:
        a_refs, a_ctx_refs = refs[:2 * n_a:2], refs[1:2 * n_a:2]
        refs = refs[n_a:]
    else:
        a_refs = a_ctx_refs = refs[:n_a]
    w_refs = refs[n_a:n_a + n_w]
    aux_refs = refs[n_a + n_w:n_a + n_w + n_aux]
    o_ref = refs[n_a + n_w + n_aux]
    wbf_refs = refs[n_a + n_w + n_aux + 1:n_a + 2 * n_w + n_aux + 1]
    aux_refs = tuple(aux_refs) + tuple(refs[n_a + 2 * n_w + n_aux + 1:])
    j = pl.program_id(0)
    i = pl.program_id(1)
    if n_pref:
        te_ref, nu_ref = pref
        new_weights = (i == 0) | (te_ref[i] != te_ref[jnp.maximum(i - 1, 0)])
        valid = i < nu_ref[0]
    else:
        new_weights = i == 0
        valid = None

    @pl.when(new_weights)
    def _():
        for w_ref, wbf_ref in zip(w_refs, wbf_refs):
            wbf_ref[...] = w_ref[...].astype(BF16)

    def compute(srcs):
        accs = [jnp.dot(srcs[a_of_w[k]][...], wbf_refs[k][...], preferred_element_type=F32)
                for k in range(n_w)]
        epilogue(accs, aux_refs, o_ref, j, i)

    if split_ctx:
        pl.when(i < CTX_TILE)(functools.partial(compute, a_refs))
        pl.when(i >= CTX_TILE)(functools.partial(compute, a_ctx_refs))
    elif valid is None:
        compute(a_refs)
    else:
        pl.when(valid)(functools.partial(compute, a_refs))

        @pl.when(jnp.logical_not(valid))
        def _():
            o_ref[...] = jnp.zeros(o_ref.shape, o_ref.dtype)


def _mm(a_list, w_list, aux_list, epilogue, *, n_out, out_dtype, tn, a_of_w=None,
        n_rows=R_ALL, tm=TM, alias_aux=None, group=None, split_ctx=False, w_col0=0,
        epilogue_scratch=(), name="mm"):
    n_a, n_w, n_aux = len(a_list), len(w_list), len(aux_list)
    n_pref = 0 if group is None else 2
    a_of_w = tuple(a_of_w) if a_of_w is not None else tuple(range(n_w))
    in_specs, args, scratch = [], [], []
    for a in a_list:
        if split_ctx:
            lat, ctx = a
            assert tm == R_CTX and group is None
            in_specs.append(pl.BlockSpec((tm, lat.shape[1]),
                                         lambda j, i: (jnp.minimum(i, CTX_TILE - 1), 0)))
            in_specs.append(pl.BlockSpec((tm, ctx.shape[1]), lambda j, i: (0, 0),
                                         pipeline_mode=pl.Buffered(1)))
            args += [lat, ctx]
        else:
            in_specs.append(pl.BlockSpec((tm, a.shape[1]), lambda j, i, *_: (i, 0)))
            args.append(a)
    for w, prefix in w_list:
        k = w.shape[-2]
        if group is None:
            wmap = functools.partial(lambda j, i, p: p + (0, w_col0 + j), p=tuple(prefix))
        else:
            wmap = functools.partial(lambda j, i, te, nu, p: p[:-1] + (te[i], 0, j), p=tuple(prefix))
        in_specs.append(pl.BlockSpec((None,) * len(prefix) + (k, tn), wmap))
        args.append(w)
        scratch.append(pltpu.VMEM((k, tn), BF16))
    for arr, block, imap in aux_list:
        in_specs.append(pl.BlockSpec(block, functools.partial(lambda j, i, *_, f: f(j, i), f=imap)))
        args.append(arr)
    aliases = {}
    if alias_aux is not None:
        aliases = {n_pref + (2 * n_a if split_ctx else n_a) + n_w + alias_aux: 0}
    grid_spec = pltpu.PrefetchScalarGridSpec(
        num_scalar_prefetch=n_pref,
        grid=(n_out // tn, n_rows // tm),
        in_specs=in_specs,
        out_specs=pl.BlockSpec((tm, tn), lambda j, i, *_: (i, j)),
        scratch_shapes=scratch + list(epilogue_scratch),
    )
    return pl.pallas_call(
        functools.partial(_mm_body, n_pref=n_pref, n_a=n_a, n_w=n_w, n_aux=n_aux,
                          epilogue=epilogue, a_of_w=a_of_w, split_ctx=split_ctx),
        grid_spec=grid_spec,
        out_shape=jax.ShapeDtypeStruct((n_rows, n_out), out_dtype),
        input_output_aliases=aliases,
        compiler_params=_cparams(2),
        name=name,
    )(*(list(group) if group is not None else []), *args)


IN_TN = 512
IN_TN_WIDE = 1024

QK_ROWS = 512


def _qk_epilogue(accs, aux, o_ref, j, i):
    gains_ref, cos_ref, sin_ref, acc_ref = aux
    acc_ref[...] = accs[0]
    gain = gains_ref[pl.ds(j // (OFF_K // IN_TN), 1), :]
    lane = lax.broadcasted_iota(jnp.int32, (1, HEAD_DIM), 1)
    first_half = (lane % (2 * ROT_FREQS)) < ROT_FREQS
    ones = jnp.ones((HEAD_DIM, HEAD_DIM), BF16)

    def rows(r, carry):
        r0 = pl.multiple_of(r * QK_ROWS, QK_ROWS)
        cos = cos_ref[pl.ds(r0, QK_ROWS), :]
        sin = sin_ref[pl.ds(r0, QK_ROWS), :]
        for h in range(IN_TN // HEAD_DIM):
            xh = acc_ref[pl.ds(r0, QK_ROWS), h * HEAD_DIM:(h + 1) * HEAD_DIM]
            ss = jnp.dot((xh * xh).astype(BF16), ones, preferred_element_type=F32)
            y = xh * lax.rsqrt(ss * (1.0 / HEAD_DIM) + EPS) * gain
            partner = jnp.where(first_half,
                                pltpu.roll(y, HEAD_DIM - ROT_FREQS, 1),
                                pltpu.roll(y, ROT_FREQS, 1))
            o_ref[pl.ds(r0, QK_ROWS), h * HEAD_DIM:(h + 1) * HEAD_DIM] = (
                y * cos + partner * sin).astype(BF16)
        return carry

    lax.fori_loop(0, TM // QK_ROWS, rows, 0)


def _cast_epilogue(accs, aux, o_ref, j, i):
    o_ref[...] = accs[0].astype(BF16)


def _gate_epilogue(accs, aux, o_ref, j, i):
    o_ref[...] = jax.nn.sigmoid(accs[0] + aux[0][...]).astype(BF16)


def _inproj(h, w_in, l, gains, cos_t, sin_t, b_gate):
    def rope_map(j, i):
        return (jnp.where(i >= CTX_TILE, SEQ // TM, i % (SEQ // TM)), 0)

    w = [(w_in, (l,))]
    qk = _mm([h], w, [(gains, (None, 8, HEAD_DIM), lambda j, i: (l, 0, 0)),
                      (cos_t, (TM, HEAD_DIM), rope_map), (sin_t, (TM, HEAD_DIM), rope_map)],
             _qk_epilogue, n_out=OFF_V, out_dtype=BF16, tn=IN_TN,
             epilogue_scratch=[pltpu.VMEM((TM, IN_TN), F32)], name="inproj_qk")
    vu = _mm([h], w, [], _cast_epilogue, n_out=OFF_G - OFF_V, out_dtype=BF16, tn=IN_TN_WIDE,
             w_col0=OFF_V // IN_TN_WIDE, name="inproj_vu")
    gates = _mm([h], w, [(b_gate.reshape(DEPTH, 1, 2 * D_MODEL), (None, 1, IN_TN_WIDE),
                          lambda j, i: (l, 0, j))],
                _gate_epilogue, n_out=2 * D_MODEL, out_dtype=BF16, tn=IN_TN_WIDE,
                w_col0=OFF_G // IN_TN_WIDE, name="inproj_gate")
    return qk, vu, gates


MG_TN = 512


def _merge_epilogue(accs, aux, o_ref, j, i):
    ga_ref, gb_ref = aux
    o_ref[...] = (ga_ref[...].astype(F32) * accs[0] + gb_ref[...].astype(F32) * accs[1]).astype(BF16)


def _merge(ya, yb, gates, w_br_a, w_br_b, l):
    aux = [
        (gates, (TM, MG_TN), lambda j, i: (i, j)),
        (gates, (TM, MG_TN), lambda j, i: (i, D_MODEL // MG_TN + j)),
    ]
    return _mm([ya, yb], [(w_br_a, (l,)), (w_br_b, (l,))], aux, _merge_epilogue,
               n_out=D_MODEL, out_dtype=BF16, tn=MG_TN, split_ctx=True, name="merge")


RS_TN = 512
FFD_TM = 512


def _resid_epilogue(accs, aux, o_ref, j, i, tm):
    x_ref, g_ref = aux
    o_ref[...] = x_ref[...] + g_ref[pl.ds(_row_seg(i, tm), 1), :] * accs[0]


def _resid_proj(a, w, prefix, x, mods, l, which, tm=TM, name="resid"):
    nb = D_MODEL // RS_TN
    aux = [
        (x, (tm, RS_TN), lambda j, i: (i, j)),
        (mods, (None, 8, RS_TN), lambda j, i: (l, 0, which * nb + j)),
    ]
    return _mm([a], [(w, prefix)], aux, functools.partial(_resid_epilogue, tm=tm),
               n_out=D_MODEL, out_dtype=F32, tm=tm, tn=RS_TN, alias_aux=0, name=name)


FF_TN = 512


def _swiglu_epilogue(accs, aux, o_ref, j, i):
    a = accs[0]
    o_ref[...] = (a * jax.nn.sigmoid(a) * accs[1]).astype(BF16)


def _swiglu_up(h, w1, w3, prefix, group=None):
    kw = dict(n_rows=MOE_P, tm=MOE_TM) if group is not None else {}
    return _mm([h], [(w1, prefix), (w3, prefix)], [], _swiglu_epilogue, n_out=D_FF,
               out_dtype=BF16, tn=FF_TN, a_of_w=(0, 0), group=group, name="swiglu_up", **kw)


MOE_TM = 512
MOE_A = TOP_K * R_ALL
MOE_NT = MOE_A // MOE_TM + N_EXPERTS
MOE_P = MOE_NT * MOE_TM
GATHER_ROWS = 256


def _route(logits):
    top_v, top_i = lax.top_k(logits[:, :N_EXPERTS], TOP_K)
    wts = jax.nn.softmax(top_v, axis=-1).reshape(-1)
    e_flat = top_i.reshape(-1).astype(jnp.int32)
    onehot = (e_flat[:, None] == jnp.arange(N_EXPERTS, dtype=jnp.int32)[None, :]).astype(jnp.int32)
    before = jnp.cumsum(onehot, axis=0) - onehot
    rank = jnp.sum(before * onehot, axis=1)
    counts = jnp.sum(onehot, axis=0)
    tiles = (counts + MOE_TM - 1) // MOE_TM
    tile_end = jnp.cumsum(tiles)
    pos = ((tile_end - tiles)[e_flat] * MOE_TM + rank).astype(jnp.int32)
    n_used = tile_end[-1:].astype(jnp.int32)
    tile_ids = jnp.minimum(jnp.arange(MOE_NT, dtype=jnp.int32), n_used[0] - 1)
    tile_expert = jnp.sum((tile_ids[:, None] >= tile_end[None, :]).astype(jnp.int32), axis=1)
    token = jnp.arange(MOE_A, dtype=jnp.int32) // TOP_K
    src_token = jnp.zeros((MOE_P,), jnp.int32).at[pos].set(token)
    return pos, src_token, wts, tile_expert.astype(jnp.int32), n_used


def _gather_body(src_ref, nu_ref, h_ref, o_ref, buf_ref, sem):
    t = pl.program_id(0)
    n_rows = nu_ref[0] * MOE_TM
    slot = t % 2

    def row_copy(slot_, r, src_row):
        return pltpu.make_async_copy(h_ref.at[pl.ds(src_row, 1), :],
                                     buf_ref.at[slot_, pl.ds(r, 1), :], sem.at[slot_])

    def issue(step, slot_):
        def body(r, carry):
            row_copy(slot_, r, src_ref[step * GATHER_ROWS + r]).start()
            return carry

        lax.fori_loop(0, GATHER_ROWS, body, 0, unroll=8)

    @pl.when((t == 0) & (n_rows > 0))
    def _():
        issue(0, 0)

    @pl.when((t + 1 < pl.num_programs(0)) & ((t + 1) * GATHER_ROWS < n_rows))
    def _():
        issue(t + 1, 1 - slot)

    @pl.when(t * GATHER_ROWS < n_rows)
    def _():
        def drain(r, carry):
            row_copy(slot, r, 0).wait()
            return carry

        lax.fori_loop(0, GATHER_ROWS, drain, 0, unroll=8)
        o_ref[...] = buf_ref[slot].astype(BF16)

    @pl.when(t * GATHER_ROWS >= n_rows)
    def _():
        o_ref[...] = jnp.zeros(o_ref.shape, o_ref.dtype)


def _moe_gather(h32, src_token, n_used):
    grid_spec = pltpu.PrefetchScalarGridSpec(
        num_scalar_prefetch=2,
        grid=(MOE_P // GATHER_ROWS,),
        in_specs=[pl.BlockSpec(memory_space=pl.ANY)],
        out_specs=pl.BlockSpec((GATHER_ROWS, D_MODEL), lambda t, *_: (t, 0)),
        scratch_shapes=[pltpu.VMEM((2, GATHER_ROWS, D_MODEL), F32), pltpu.SemaphoreType.DMA((2,))],
    )
    return pl.pallas_call(
        _gather_body,
        grid_spec=grid_spec,
        out_shape=jax.ShapeDtypeStruct((MOE_P, D_MODEL), BF16),
        compiler_params=_cparams(1),
        name="moe_gather",
    )(src_token, n_used, h32)


def _moe_down_epilogue(accs, aux, o_ref, j, i):
    o_ref[...] = accs[0]


def _combine_body(pos_ref, y_ref, w_ref, x_ref, g_ref, o_ref, buf_ref, sem):
    t = pl.program_id(0)
    slot = t % 2

    def row_copy(slot_, k, r, src_row):
        return pltpu.make_async_copy(y_ref.at[pl.ds(src_row, 1), :],
                                     buf_ref.at[slot_, k, pl.ds(r, 1), :], sem.at[slot_])

    def issue(step, slot_):
        def body(r, carry):
            for k in range(TOP_K):
                row_copy(slot_, k, r, pos_ref[TOP_K * (step * TR + r) + k]).start()
            return carry

        lax.fori_loop(0, TR, body, 0, unroll=4)

    @pl.when(t == 0)
    def _():
        issue(0, 0)

    @pl.when(t + 1 < pl.num_programs(0))
    def _():
        issue(t + 1, 1 - slot)

    def drain(r, carry):
        for k in range(TOP_K):
            row_copy(slot, k, r, 0).wait()
        return carry

    lax.fori_loop(0, TR, drain, 0, unroll=4)
    g = g_ref[pl.ds(_row_seg(t, TR), 1), :]
    y = w_ref[0] * buf_ref[slot, 0]
    for k in range(1, TOP_K):
        y = y + w_ref[k] * buf_ref[slot, k]
    o_ref[...] = x_ref[...] + g * y


def _moe_combine(ys, pos, wts, x, mods, l, which):
    w_cols = jnp.broadcast_to(wts.reshape(R_ALL, TOP_K).T[:, :, None], (TOP_K, R_ALL, 1))
    grid_spec = pltpu.PrefetchScalarGridSpec(
        num_scalar_prefetch=1,
        grid=(R_ALL // TR,),
        in_specs=[pl.BlockSpec(memory_space=pl.ANY),
                  pl.BlockSpec((TOP_K, TR, 1), lambda t, *_: (0, t, 0)),
                  pl.BlockSpec((TR, D_MODEL), lambda t, *_: (t, 0)),
                  pl.BlockSpec((None, 8, D_MODEL), lambda t, *_: (l, 0, which))],
        out_specs=pl.BlockSpec((TR, D_MODEL), lambda t, *_: (t, 0)),
        scratch_shapes=[pltpu.VMEM((2, TOP_K, TR, D_MODEL), F32), pltpu.SemaphoreType.DMA((2,))],
    )
    return pl.pallas_call(
        _combine_body,
        grid_spec=grid_spec,
        out_shape=jax.ShapeDtypeStruct((R_ALL, D_MODEL), F32),
        input_output_aliases={3: 0},
        compiler_params=_cparams(1),
        name="moe_combine",
    )(pos, ys, w_cols, x, mods)


def _moe(h32, logits, xs, mods, l, i, moe_w1, moe_w3, moe_w2):
    pos, src_token, wts, tile_expert, n_used = _route(logits)
    group = (tile_expert, n_used)
    hs = _moe_gather(h32, src_token, n_used)
    g = _swiglu_up(hs, moe_w1, moe_w3, (i, 0), group=group)
    ys = _mm([g], [(moe_w2, (i, 0))], [], _moe_down_epilogue, n_out=D_MODEL, out_dtype=F32,
             tn=RS_TN, n_rows=MOE_P, tm=MOE_TM, group=group, name="moe_down")
    return _moe_combine(ys, pos, wts, xs, mods, l, 5)


AT_TQ = 512
AT_TK = 1024


ATT_SAFE_BOUND = 60.0


def _attn_scores(qs, k_ref, sl, c0, c1):
    return lax.dot_general(qs, k_ref[c0:c1, sl], (((1,), (1,)), ((), ())),
                           preferred_element_type=F32)


def _attn_body(lam_ref, q_ref, *rest, n_seg, post_scale):
    k_refs = rest[:n_seg]
    v_refs = rest[n_seg:2 * n_seg]
    sub_ref = rest[2 * n_seg]
    o_ref = rest[2 * n_seg + 1]
    kmax_ref = rest[2 * n_seg + 2]
    num_ref = rest[2 * n_seg + 3]
    lam = lam_ref[0]
    subs = [slice(s * HEAD_DIM, (s + 1) * HEAD_DIM) for s in range(2)]

    @pl.when(pl.program_id(2) == 0)
    def _():
        for s, sl in enumerate(subs):
            best = None
            for k_ref in k_refs:
                kf = k_ref[:, sl].astype(F32)
                n2 = jnp.max(jnp.sum(kf * kf, axis=-1, keepdims=True))
                best = n2 if best is None else jnp.maximum(best, n2)
            kmax_ref[s] = best

    qs = [q_ref[:, sl] for sl in subs]
    bounds = []
    for s in range(2):
        qf = qs[s].astype(F32)
        qn2 = jnp.sum(qf * qf, axis=-1, keepdims=True)
        bounds.append(jnp.sqrt(qn2 * kmax_ref[s]) * 1.001 + 1e-6)
    worst = jnp.maximum(jnp.max(bounds[0]), jnp.max(bounds[1]))

    @pl.when(worst <= ATT_SAFE_BOUND)
    def _():
        for s, sl in enumerate(subs):
            den = num = None
            for k_ref, v_ref in zip(k_refs, v_refs):
                p = jnp.exp2(_attn_scores(qs[s], k_ref, sl, 0, k_ref.shape[0]) - bounds[s])
                ps = p.sum(axis=-1, keepdims=True)
                pv = jnp.dot(p.astype(BF16), v_ref[...], preferred_element_type=F32)
                den = ps if den is None else den + ps
                num = pv if num is None else num + pv
            num_ref[s] = num / den

    @pl.when(worst > ATT_SAFE_BOUND)
    def _():
        for s, sl in enumerate(subs):
            m = den = num = None
            for k_ref, v_ref in zip(k_refs, v_refs):
                seg_len = k_ref.shape[0]
                for c0 in range(0, seg_len, AT_TK):
                    c1 = min(c0 + AT_TK, seg_len)
                    sc = _attn_scores(qs[s], k_ref, sl, c0, c1)
                    mc = sc.max(axis=-1, keepdims=True)
                    m_new = mc if m is None else jnp.maximum(m, mc)
                    p = jnp.exp2(sc - m_new)
                    ps = p.sum(axis=-1, keepdims=True)
                    pv = jnp.dot(p.astype(BF16), v_ref[c0:c1, :], preferred_element_type=F32)
                    if m is None:
                        den, num = ps, pv
                    else:
                        alpha = jnp.exp2(m - m_new)
                        den = alpha * den + ps
                        num = alpha * num + pv
                    m = m_new
            num_ref[s] = num / den

    d = num_ref[0] - lam * num_ref[1]
    ms = jnp.mean(d * d, axis=-1, keepdims=True)
    o_ref[...] = (d * lax.rsqrt(ms + EPS) * sub_ref[...] * post_scale).astype(BF16)


def _attention(qk, vu, lam, sub_norm, l, lam_init, ctx_only=False):
    kb = OFF_K // V_HEAD_DIM
    ctx_blk0 = R_LAT // CTX_LEN
    if ctx_only:
        tq = CTX_LEN
        grid = (BATCH, N_DIFF_HEADS, 1)
        q_map = lambda b, h, t: (ctx_blk0 + b, h)
        o_map = lambda b, h, t: (b, h)
        segs = [CTX_LEN]
    else:
        tq = AT_TQ
        grid = (BATCH, N_DIFF_HEADS, SEQ // tq)
        q_map = o_map = lambda b, h, t: (b * (SEQ // tq) + t, h)
        segs = [CTX_LEN, SEQ]
    in_specs = [pl.BlockSpec(memory_space=pltpu.SMEM),
                pl.BlockSpec((tq, V_HEAD_DIM), q_map)]
    args = [lam.reshape(1), qk]
    for arr, off in ((qk, kb), (vu, 0)):
        for seg_len in segs:
            if seg_len == CTX_LEN:
                imap = functools.partial(lambda b, h, t, o: (ctx_blk0 + b, o + h), o=off)
            else:
                imap = functools.partial(lambda b, h, t, o: (b, o + h), o=off)
            in_specs.append(pl.BlockSpec((seg_len, V_HEAD_DIM), imap))
            args.append(arr)
    in_specs.append(pl.BlockSpec((None, 1, V_HEAD_DIM), lambda b, h, t: (l, 0, 0)))
    args.append(sub_norm.reshape(DEPTH, 1, V_HEAD_DIM))
    return pl.pallas_call(
        functools.partial(_attn_body, n_seg=len(segs), post_scale=1.0 - lam_init),
        grid=grid,
        in_specs=in_specs,
        out_specs=pl.BlockSpec((tq, V_HEAD_DIM), o_map),
        out_shape=jax.ShapeDtypeStruct((R_CTX if ctx_only else R_LAT, ATT_W), BF16),
        scratch_shapes=[pltpu.SMEM((2,), F32), pltpu.VMEM((2, tq, V_HEAD_DIM), F32)],
        compiler_params=_cparams(3),
        name="attn_ctx" if ctx_only else "attn",
    )(*args)


def _dft_tables():
    n, n1, n2 = FFT_N, FFT_N1, FFT_N2
    k1 = np.arange(n1)[:, None]
    tabs_d, tabs_s, tabs_i = [], [], []
    for b in range(n2):
        cols = n2 * np.arange(n1)[None, :] + b
        g = np.exp(-2j * np.pi * ((k1 * cols) % n) / n)
        gd = g[:, :n1 // 2]
        tabs_d.append(np.block([[gd.real, -gd.imag], [gd.imag, gd.real]]))
        tabs_s.append(np.concatenate([g.real, g.imag], axis=0))
        gi = np.conj(gd).T / n1
        tabs_i.append(np.block([[gi.real, -gi.imag], [gi.imag, gi.real]]))
    kk = np.arange(n2)
    f2 = np.exp(-2j * np.pi * ((kk[:, None] * kk[None, :]) % n2) / n2)
    g2 = np.block([[f2.real, -f2.imag], [f2.imag, f2.real]])
    f2i = np.conj(f2) / n2
    g2i = np.block([[f2i.real, -f2i.imag], [f2i.imag, f2i.real]])
    as_bf = lambda a: jnp.asarray(np.asarray(a, np.float32)).astype(BF16)
    return (as_bf(np.stack(tabs_d)), as_bf(np.stack(tabs_s)), as_bf(np.stack(tabs_i)),
            as_bf(g2), as_bf(g2i))


FFT_UNROLL = 32


def _pack_pair(a, b):
    ab = lax.bitcast_convert_type(a.astype(BF16).astype(F32), jnp.uint32)
    bb = lax.bitcast_convert_type(b.astype(BF16).astype(F32), jnp.uint32)
    return lax.bitcast_convert_type(ab | (bb >> 16), F32)


def _unpack_pair(w):
    bits = lax.bitcast_convert_type(w, jnp.uint32)
    hi = lax.bitcast_convert_type(bits & jnp.uint32(0xFFFF0000), F32)
    lo = lax.bitcast_convert_type(bits << 16, F32)
    return hi, lo


def _spectrum_body(hf_ref, hr_ref, wf_ref, wb_ref, dl_ref, g1_ref, g2_ref, or_ref, oi_ref,
                   k_ref, a_ref):
    n1, n2 = FFT_N1, FFT_N2
    delta = dl_ref[...]
    step = 1.0 / (SEQ - 1)
    rows = 512
    row_iota = lax.broadcasted_iota(jnp.int32, (rows, 1), 0)

    def taps(c, carry):
        r0 = pl.multiple_of(c * rows, rows)
        pos = (row_iota + r0).astype(F32)
        top = jnp.dot(hf_ref[pl.ds(r0, rows), :].astype(BF16), wf_ref[...].astype(BF16),
                      preferred_element_type=F32)
        bot = jnp.dot(hr_ref[pl.ds(r0, rows), :].astype(BF16), wb_ref[...].astype(BF16),
                      preferred_element_type=F32)
        k_ref[pl.ds(r0, rows), :] = top * jnp.exp(-(pos * step) * delta)
        k_ref[pl.ds(SEQ + r0, rows), :] = bot * jnp.exp(-((SEQ - pos) * step) * delta)
        return carry

    lax.fori_loop(0, SEQ // rows, taps, 0)

    def stage1(b, carry):
        x = k_ref[pl.ds(b, n1, stride=n2), :].astype(BF16)
        a = jnp.dot(g1_ref[b], x, preferred_element_type=F32)
        row = pl.multiple_of(b * n1, n1)
        a_ref[pl.ds(row, n1), :] = _pack_pair(a[:n1], a[n1:])
        return carry

    lax.fori_loop(0, n2, stage1, 0, unroll=FFT_UNROLL)

    def stage2(k1, carry):
        a = jnp.concatenate(_unpack_pair(a_ref[pl.ds(k1, n2, stride=n1), :]), axis=0).astype(BF16)
        x = jnp.dot(g2_ref[...], a, preferred_element_type=F32)
        row = pl.multiple_of(k1 * n2, n2)
        or_ref[pl.ds(row, n2), :] = x[:n2].astype(BF16)
        oi_ref[pl.ds(row, n2), :] = x[n2:].astype(BF16)
        return carry

    lax.fori_loop(0, n1, stage2, 0, unroll=FFT_UNROLL)


def _const_spec(shape, n_grid):
    nd = len(shape)
    imap = (lambda a, b: (0,) * nd) if n_grid == 2 else (lambda a: (0,) * nd)
    return pl.BlockSpec(shape, imap, pipeline_mode=pl.Buffered(1))


def _spectrum(hid_f, hid_r, w4p, deltas, l, g1s, g2):
    tc = HY_TC
    nct = HY_W // tc
    hspec = pl.BlockSpec((SEQ, LANES), lambda f, ct: (0, 0))
    wspec = lambda d: pl.BlockSpec(
        (None, LANES, tc), functools.partial(lambda f, ct, d: (l, 0, (2 * f + d) * nct + ct), d=d))
    ospec = pl.BlockSpec((None, FFT_N, tc), lambda f, ct: (f, 0, ct))
    return pl.pallas_call(
        _spectrum_body,
        grid=(HY_N_FILT, nct),
        in_specs=[hspec, hspec, wspec(0), wspec(1),
                  pl.BlockSpec((1, tc), lambda f, ct: (0, ct)),
                  _const_spec((FFT_N2, 2 * FFT_N1, FFT_N1), 2),
                  _const_spec((2 * FFT_N2, 2 * FFT_N2), 2)],
        out_specs=[ospec, ospec],
        out_shape=[jax.ShapeDtypeStruct((HY_N_FILT, FFT_N, HY_W), BF16)] * 2,
        scratch_shapes=[pltpu.VMEM((FFT_N, tc), F32)] * 2,
        compiler_params=_cparams(2),
        name="hy_spectrum",
    )(hid_f, hid_r, w4p, w4p, deltas, g1s, g2)


def _conv3(u, w_ref, b_ref):
    n = u.shape[0]
    row = lax.broadcasted_iota(jnp.int32, u.shape, 0)
    prev = jnp.where(row == 0, 0.0, pltpu.roll(u, 1, 0))
    nxt = jnp.where(row == n - 1, 0.0, pltpu.roll(u, n - 1, 0))
    return prev * w_ref[0:1, :] + u * w_ref[1:2, :] + nxt * w_ref[2:3, :] + b_ref[...]


def _hyconv_body(s0_ref, s1_ref, m0_ref, m1_ref, cws_ref, cbs_ref, cwm_ref, cbm_ref, bias_ref,
                 kr_ref, ki_ref, g1_ref, g1i_ref, g2_ref, g2i_ref, o_ref,
                 vr_ref, vi_ref, vp_ref, a_ref, b_ref, *, conv_signal):
    n1, n2 = FFT_N1, FFT_N2
    h1 = n1 // 2
    for s_ref, v_ref in ((s0_ref, vr_ref), (s1_ref, vi_ref)):
        u = s_ref[...].astype(F32)
        v_ref[...] = _conv3(u, cws_ref, cbs_ref) if conv_signal else u
    vp_ref[...] = _pack_pair(vr_ref[...], vi_ref[...])

    def stage1(b, carry):
        x = jnp.concatenate(_unpack_pair(vp_ref[pl.ds(b, h1, stride=n2), :]), axis=0).astype(BF16)
        a = jnp.dot(g1_ref[b], x, preferred_element_type=F32)
        row = pl.multiple_of(b * n1, n1)
        a_ref[pl.ds(row, n1), :] = _pack_pair(a[:n1], a[n1:])
        return carry

    lax.fori_loop(0, n2, stage1, 0, unroll=FFT_UNROLL)

    def stage2(k1, carry):
        a = jnp.concatenate(_unpack_pair(a_ref[pl.ds(k1, n2, stride=n1), :]), axis=0).astype(BF16)
        x = jnp.dot(g2_ref[...], a, preferred_element_type=F32)
        row = pl.multiple_of(k1 * n2, n2)
        fr = kr_ref[pl.ds(row, n2), :].astype(F32)
        fi = ki_ref[pl.ds(row, n2), :].astype(F32)
        xr, xi = x[:n2], x[n2:]
        y = jnp.concatenate([xr * fr - xi * fi, xr * fi + xi * fr], axis=0).astype(BF16)
        bq = jnp.dot(g2i_ref[...], y, preferred_element_type=F32)
        b_ref[pl.ds(row, n2), :] = _pack_pair(bq[:n2], bq[n2:])
        return carry

    lax.fori_loop(0, n1, stage2, 0, unroll=FFT_UNROLL)

    def stage1_inv(b, carry):
        bq = jnp.concatenate(_unpack_pair(b_ref[pl.ds(b, n1, stride=n2), :]), axis=0).astype(BF16)
        y = jnp.dot(g1i_ref[b], bq, preferred_element_type=F32)
        row = pl.multiple_of(b * h1, h1)
        a_ref[pl.ds(row, h1), :] = _pack_pair(y[:h1], y[h1:])
        return carry

    lax.fori_loop(0, n2, stage1_inv, 0, unroll=FFT_UNROLL)

    bias = bias_ref[...]

    def unpermute(i, carry):
        row = pl.multiple_of(i * n2, n2)
        yr, yi = _unpack_pair(a_ref[pl.ds(i, n2, stride=h1), :])
        vr_ref[pl.ds(row, n2), :] = yr + bias * vr_ref[pl.ds(row, n2), :]
        vi_ref[pl.ds(row, n2), :] = yi + bias * vi_ref[pl.ds(row, n2), :]
        return carry

    lax.fori_loop(0, h1, unpermute, 0, unroll=FFT_UNROLL)

    for half, (m_ref, v_ref) in enumerate(((m0_ref, vr_ref), (m1_ref, vi_ref))):
        mult = _conv3(m_ref[...].astype(F32), cwm_ref, cbm_ref)
        o_ref[half * SEQ:(half + 1) * SEQ, :] = (mult * v_ref[...]).astype(o_ref.dtype)


def _hy_param_specs(conv_w, conv_b, bias, l, filt, sig_ch, mul_ch, tc):
    cw3 = conv_w.reshape(DEPTH, 3, 3 * HY_W)
    cb3 = conv_b.reshape(DEPTH, 1, 3 * HY_W)
    chan = lambda ch0, rows: pl.BlockSpec(
        (None, rows, tc), functools.partial(lambda ct, pr, c0: (l, 0, c0 + ct), c0=ch0))
    specs = [chan(sig_ch, 3), chan(sig_ch, 1), chan(mul_ch, 3), chan(mul_ch, 1),
             pl.BlockSpec((None, None, 1, tc), lambda ct, pr: (l, filt, 0, ct))]
    args = [cw3, cb3, cw3, cb3, bias.reshape(DEPTH, HY_N_FILT, 1, HY_W)]
    return specs, args


def _hyconv(sig, sig_cb, mul, mul_cb, conv_w, conv_b, bias, l, filt, sig_ch, mul_ch,
            kr, ki, tabs, conv_signal):
    g1d, g1i, g2, g2i = tabs
    tc = HY_TC
    blk = lambda arr_cb, odd: pl.BlockSpec(
        (SEQ, tc), functools.partial(lambda ct, pr, cb, o: (2 * pr + o, cb + ct), cb=arr_cb, o=odd))
    pspecs, pargs = _hy_param_specs(conv_w, conv_b, bias, l, filt, sig_ch, mul_ch, tc)
    kspec = pl.BlockSpec((None, FFT_N, tc), lambda ct, pr: (filt, 0, ct))
    in_specs = [blk(sig_cb, 0), blk(sig_cb, 1), blk(mul_cb, 0), blk(mul_cb, 1)] + pspecs + [
        kspec, kspec,
        _const_spec((FFT_N2, 2 * FFT_N1, FFT_N1), 2),
        _const_spec((FFT_N2, FFT_N1, 2 * FFT_N1), 2),
        _const_spec((2 * FFT_N2, 2 * FFT_N2), 2),
        _const_spec((2 * FFT_N2, 2 * FFT_N2), 2),
    ]
    return pl.pallas_call(
        functools.partial(_hyconv_body, conv_signal=conv_signal),
        grid=(HY_W // tc, BATCH // 2),
        in_specs=in_specs,
        out_specs=pl.BlockSpec((2 * SEQ, tc), lambda ct, pr: (pr, ct)),
        out_shape=jax.ShapeDtypeStruct((R_LAT, HY_W), BF16),
        scratch_shapes=[pltpu.VMEM((SEQ, tc), F32)] * 3 + [pltpu.VMEM((FFT_N, tc), F32)] * 2,
        compiler_params=_cparams(2),
        name="hyconv",
    )(sig, sig, mul, mul, *pargs, kr, ki, g1d, g1i, g2, g2i)


def _ctx_dft_tables():
    n = 2 * CTX_LEN
    kk = np.arange(n)
    f = np.exp(-2j * np.pi * ((kk[:, None] * kk[None, :]) % n) / n)
    fd = f[:, :CTX_LEN]
    fwd = np.block([[fd.real, -fd.imag], [fd.imag, fd.real]])
    spec = np.concatenate([f.real, f.imag], axis=0)
    fi = np.conj(f)[:CTX_LEN, :] / n
    inv = np.block([[fi.real, -fi.imag], [fi.imag, fi.real]])
    as_bf = lambda a: jnp.asarray(np.asarray(a, np.float32)).astype(BF16)
    return as_bf(fwd), as_bf(spec), as_bf(inv)


def _hyconv_ctx_body(s0_ref, s1_ref, m0_ref, m1_ref, cws_ref, cbs_ref, cwm_ref, cbm_ref, bias_ref,
                     k_ref, fwd_ref, spec_ref, inv_ref, o_ref, *, conv_signal):
    n = 2 * CTX_LEN
    vs = []
    for s_ref in (s0_ref, s1_ref):
        u = s_ref[...].astype(F32)
        vs.append(_conv3(u, cws_ref, cbs_ref) if conv_signal else u)
    x = jnp.dot(fwd_ref[...], jnp.concatenate(vs, axis=0).astype(BF16), preferred_element_type=F32)
    kf = jnp.dot(spec_ref[...], k_ref[...].astype(BF16), preferred_element_type=F32)
    kf = kf.astype(BF16).astype(F32)
    xr, xi, fr, fi = x[:n], x[n:], kf[:n], kf[n:]
    y = jnp.concatenate([xr * fr - xi * fi, xr * fi + xi * fr], axis=0).astype(BF16)
    conv = jnp.dot(inv_ref[...], y, preferred_element_type=F32)
    bias = bias_ref[...]
    for half, m_ref in enumerate((m0_ref, m1_ref)):
        sl = slice(half * CTX_LEN, (half + 1) * CTX_LEN)
        mult = _conv3(m_ref[...].astype(F32), cwm_ref, cbm_ref)
        o_ref[sl, :] = (mult * (conv[sl] + bias * vs[half])).astype(o_ref.dtype)


def _hyconv_ctx(sig, sig_rb, sig_cb, mul, mul_rb, mul_cb, conv_w, conv_b, bias, l, filt,
                sig_ch, mul_ch, kctx, tabs, conv_signal):
    fwd, spec, inv = tabs
    tc = HY_TC
    blk = lambda rb, arr_cb, odd: pl.BlockSpec(
        (CTX_LEN, tc),
        functools.partial(lambda ct, pr, rb, cb, o: (rb + 2 * pr + o, cb + ct), rb=rb, cb=arr_cb, o=odd))
    pspecs, pargs = _hy_param_specs(conv_w, conv_b, bias, l, filt, sig_ch, mul_ch, tc)
    n = 2 * CTX_LEN
    in_specs = [blk(sig_rb, sig_cb, 0), blk(sig_rb, sig_cb, 1),
                blk(mul_rb, mul_cb, 0), blk(mul_rb, mul_cb, 1)] + pspecs + [
        pl.BlockSpec((None, n, tc), lambda ct, pr: (filt, 0, ct)),
        _const_spec((2 * n, n), 2),
        _const_spec((2 * n, n), 2),
        _const_spec((n, 2 * n), 2),
    ]
    return pl.pallas_call(
        functools.partial(_hyconv_ctx_body, conv_signal=conv_signal),
        grid=(HY_W // tc, BATCH // 2),
        in_specs=in_specs,
        out_specs=pl.BlockSpec((2 * CTX_LEN, tc), lambda ct, pr: (pr, ct)),
        out_shape=jax.ShapeDtypeStruct((R_CTX, HY_W), BF16),
        compiler_params=_cparams(2),
        name="hyconv_ctx",
    )(sig, sig, mul, mul, *pargs, kctx, fwd, spec, inv)


def _rope_tables():
    rows = SEQ // GRID_W
    row = jnp.repeat(jnp.arange(rows), GRID_W).astype(F32)
    col = jnp.tile(jnp.arange(GRID_W), rows).astype(F32)
    inv = ROPE_BASE ** (-jnp.arange(ROT_FREQS, dtype=F32) / ROT_FREQS)
    ar, ac = row[:, None] * inv, col[:, None] * inv
    cos = jnp.concatenate([jnp.cos(ar), jnp.cos(ar), jnp.cos(ac), jnp.cos(ac)], axis=1)
    sin = jnp.concatenate([-jnp.sin(ar), jnp.sin(ar), -jnp.sin(ac), jnp.sin(ac)], axis=1)
    cos = jnp.concatenate([cos, jnp.ones((TM, HEAD_DIM), F32)], axis=0)
    sin = jnp.concatenate([sin, jnp.zeros((TM, HEAD_DIM), F32)], axis=0)
    return cos, sin


def _hyena_positions(L, reverse):
    idx = jnp.arange(L, dtype=F32)
    return (L - idx) if reverse else idx


def _hyena_hidden(L, w1, b1, w2, b2, w3, b3, freq, reverse=False):
    pos = _hyena_positions(L, reverse)[:, None]
    t = pos / (L - 1)
    w = 2.0 * math.pi * pos / L
    f = jnp.linspace(1e-4, HY_BANDS - 1, HY_BANDS, dtype=F32)
    z = jnp.concatenate([t, jnp.cos(w * f), -jnp.sin(w * f)], axis=-1)
    dot = functools.partial(jnp.dot, precision=HIGHEST)
    hid = jnp.sin(freq * (dot(z, w1) + b1))
    hid = jnp.sin(freq * (dot(hid, w2) + b2))
    hid = jnp.sin(freq * (dot(hid, w3) + b3))
    if reverse:
        hid = jnp.where(jnp.arange(L)[:, None] > 0, hid, 0.0)
    return hid


def _hyena_deltas():
    return jnp.abs(jnp.linspace(math.log(HY_TARGET) / HY_SLOW_PCT,
                                math.log(HY_TARGET) / HY_FAST_PCT, HY_W, dtype=F32))


def _hyena_filters_ctx(hid, hid_rev, w4):
    L = hid.shape[0]
    w4 = w4.reshape(HY_FH, HY_N_FILT, 2, HY_W)
    halves = []
    for d, feats in enumerate((hid, hid_rev)):
        t = _hyena_positions(L, bool(d))[:, None] / (L - 1)
        taps = jnp.einsum('lh,hfc->flc', feats, w4[:, :, d], precision=HIGHEST)
        halves.append(taps * jnp.exp(-t * _hyena_deltas())[None])
    return jnp.concatenate(halves, axis=1)


def kernel(x, c, ctx, c_ctx, w_ada, b_ada, norm1, norm2, w_in, b_gate, q_norm, k_norm, lam_q1, lam_k1, lam_q2, lam_k2, sub_norm, hy_conv_w, hy_conv_b, hy_w1, hy_b1, hy_w2, hy_b2, hy_w3, hy_b3, hy_w4, hy_freq, hy_bias, w_br_a, w_br_b, w_out, ffn_w1, ffn_w3, ffn_w2, router, moe_w1, moe_w3, moe_w2):
    xs = jnp.concatenate([x.reshape(R_LAT, D_MODEL), ctx.reshape(R_CTX, D_MODEL)], axis=0)
    cs = jnp.concatenate([c, c_ctx[None], jnp.zeros((8 - BATCH - 1, D_MODEL), F32)], axis=0)
    mods = _adaln(cs, w_ada, b_ada)
    cos_t, sin_t = _rope_tables()
    g1d, g1s, g1i, g2, g2i = _dft_tables()
    ctx_tabs = _ctx_dft_tables()
    gains = jnp.zeros((DEPTH, 8, HEAD_DIM), F32)
    gains = gains.at[:, 0].set(q_norm * (math.log2(math.e) / math.sqrt(HEAD_DIM))).at[:, 1].set(k_norm)
    router_pad = jnp.zeros((router.shape[0], D_MODEL, LANES), F32).at[:, :, :N_EXPERTS].set(router)
    hy_cb = ATT_W // HY_TC
    ch = HY_W // HY_TC
    w4p = jnp.pad(hy_w4, ((0, 0), (0, LANES - HY_FH), (0, 0)))
    deltas = _hyena_deltas()[None]

    for l in range(DEPTH):
        lam_init = 0.8 - 0.6 * math.exp(-0.3 * l)
        lam = (jnp.exp(jnp.sum(lam_q1[l] * lam_k1[l])) - jnp.exp(jnp.sum(lam_q2[l] * lam_k2[l]))
               + lam_init)
        h = _normmod(xs, norm1, mods, l, 0)
        qk, vu, gates = _inproj(h, w_in, l, gains, cos_t, sin_t, b_gate)

        ya = (_attention(qk, vu, lam, sub_norm, l, lam_init),
              _attention(qk, vu, lam, sub_norm, l, lam_init, ctx_only=True))

        hp = (hy_w1[l], hy_b1[l], hy_w2[l], hy_b2[l], hy_w3[l], hy_b3[l], hy_freq[l])
        lane_pad = ((0, 0), (0, LANES - HY_FH))
        hid = jnp.pad(_hyena_hidden(SEQ, *hp), lane_pad)
        hid_rev = jnp.pad(_hyena_hidden(SEQ, *hp, reverse=True), lane_pad)
        kr, ki = _spectrum(hid, hid_rev, w4p, deltas, l, g1s, g2)
        kctx = _hyena_filters_ctx(_hyena_hidden(CTX_LEN, *hp),
                                  _hyena_hidden(CTX_LEN, *hp, reverse=True), hy_w4[l])
        common = (hy_conv_w, hy_conv_b, hy_bias, l)
        tabs = (g1d, g1i, g2, g2i)
        crb = R_LAT // CTX_LEN
        z = _hyconv(vu, hy_cb, vu, hy_cb + ch, *common, 0, 0, ch, kr, ki, tabs, True)
        zc = _hyconv_ctx(vu, crb, hy_cb, vu, crb, hy_cb + ch, *common, 0, 0, ch, kctx, ctx_tabs, True)
        yb = (_hyconv(z, 0, vu, hy_cb + 2 * ch, *common, 1, 0, 2 * ch, kr, ki, tabs, False),
              _hyconv_ctx(zc, 0, 0, vu, crb, hy_cb + 2 * ch, *common, 1, 0, 2 * ch, kctx, ctx_tabs, False))

        m = _merge(ya, yb, gates, w_br_a, w_br_b, l)
        xs = _resid_proj(m, w_out, (l,), xs, mods, l, 2, name="out_proj")

        i = l // 2
        if l % 2 == 0:
            h2 = _normmod(xs, norm2, mods, l, 3)
            g = _swiglu_up(h2, ffn_w1, ffn_w3, (i,))
            xs = _resid_proj(g, ffn_w2, (i,), xs, mods, l, 5, tm=FFD_TM, name="ffn_down")
        else:
            h32, logits = _normmod(xs, norm2, mods, l, 3, router=router_pad[i])
            xs = _moe(h32, logits, xs, mods, l, i, moe_w1, moe_w3, moe_w2)
    return xs[:R_LAT].reshape(BATCH, SEQ, D_MODEL)
```

```python
import functools
import math

import numpy as np
import jax
import jax.numpy as jnp
from jax import lax
from jax.experimental import pallas as pl
from jax.experimental.pallas import tpu as pltpu

D_MODEL = 2048
BATCH = 4
SEQ = 4096
DEPTH = 4
CTX_LEN = 256
GRID_W = 64
HEAD_DIM = 128
N_DIFF_HEADS = D_MODEL // (2 * HEAD_DIM)
V_HEAD_DIM = 2 * HEAD_DIM
ATT_W = N_DIFF_HEADS * V_HEAD_DIM
HY_W = D_MODEL
OFF_K = ATT_W
OFF_V = 2 * ATT_W
OFF_HY = 3 * ATT_W
OFF_G = 3 * ATT_W + 3 * HY_W
IN_COLS = OFF_G + 2 * D_MODEL
ROPE_BASE = 10000.0
ROT_FREQS = HEAD_DIM // 4
HY_EMB = 33
HY_BANDS = (HY_EMB - 1) // 2
HY_FH = 64
HY_N_FILT = 2
HY_TARGET = 1e-2
HY_FAST_PCT = 0.3
HY_SLOW_PCT = 1.5
D_FF = 5632
N_EXPERTS = 8
TOP_K = 2
EPS = 1e-6

R_LAT = BATCH * SEQ
R_CTX = BATCH * CTX_LEN
R_ALL = R_LAT + R_CTX
TM = 1024
CTX_TILE = R_LAT // TM
CTX_SEG = BATCH
TR = 512
LANES = 128
VMEM_LIMIT = 56 * 1024 * 1024

FFT_N = 2 * SEQ
FFT_N1 = 128
FFT_N2 = 64
HY_TC = 128

F32 = jnp.float32
BF16 = jnp.bfloat16
HIGHEST = lax.Precision.HIGHEST


def _cparams(n_axes):
    return pltpu.CompilerParams(
        dimension_semantics=("arbitrary",) * n_axes, vmem_limit_bytes=VMEM_LIMIT)


def _row_seg(i, tm=TM):
    return jnp.where(i >= R_LAT // tm, CTX_SEG, i // (SEQ // tm))


def _adaln_body(c_ref, w_ref, b_ref, o_ref):
    s = c_ref[...]
    s = s * jax.nn.sigmoid(s)
    o_ref[...] = jnp.dot(s, w_ref[...], preferred_element_type=F32, precision=HIGHEST) + b_ref[...]


def _adaln(cs, w_ada, b_ada):
    tn = 1536
    n = 6 * D_MODEL
    return pl.pallas_call(
        _adaln_body,
        grid=(DEPTH, n // tn),
        in_specs=[
            pl.BlockSpec((8, D_MODEL), lambda l, j: (0, 0)),
            pl.BlockSpec((None, D_MODEL, tn), lambda l, j: (l, 0, j)),
            pl.BlockSpec((None, 1, tn), lambda l, j: (l, 0, j)),
        ],
        out_specs=pl.BlockSpec((None, 8, tn), lambda l, j: (l, 0, j)),
        out_shape=jax.ShapeDtypeStruct((DEPTH, 8, n), F32),
        compiler_params=_cparams(2),
        name="adaln",
    )(cs, w_ada, b_ada.reshape(DEPTH, 1, n))


def _normmod_body(x_ref, g_ref, sh_ref, sc_ref, *rest, with_router):
    i = pl.program_id(0)
    seg = jnp.where(i >= R_LAT // TR, CTX_SEG, i // (SEQ // TR))
    x = x_ref[...]
    ms = jnp.mean(x * x, axis=-1, keepdims=True)
    y = x * lax.rsqrt(ms + EPS) * g_ref[...]
    h = y * (1.0 + sc_ref[pl.ds(seg, 1), :]) + sh_ref[pl.ds(seg, 1), :]
    if with_router:
        r_ref, o_ref, lg_ref = rest
        lg_ref[...] = jnp.dot(h, r_ref[...], preferred_element_type=F32, precision=HIGHEST)
    else:
        (o_ref,) = rest
    o_ref[...] = h.astype(o_ref.dtype)


def _normmod(x, gain, mods, l, which, router=None):
    in_specs = [
        pl.BlockSpec((TR, D_MODEL), lambda i: (i, 0)),
        pl.BlockSpec((None, 1, D_MODEL), lambda i: (l, 0, 0)),
        pl.BlockSpec((None, 8, D_MODEL), lambda i: (l, 0, which)),
        pl.BlockSpec((None, 8, D_MODEL), lambda i: (l, 0, which + 1)),
    ]
    args = [x, gain.reshape(DEPTH, 1, D_MODEL), mods, mods]
    out_specs = pl.BlockSpec((TR, D_MODEL), lambda i: (i, 0))
    out_shape = jax.ShapeDtypeStruct((R_ALL, D_MODEL), BF16 if router is None else F32)
    if router is not None:
        in_specs.append(pl.BlockSpec((D_MODEL, LANES), lambda i: (0, 0)))
        args.append(router)
        out_specs = [out_specs, pl.BlockSpec((TR, LANES), lambda i: (i, 0))]
        out_shape = [out_shape, jax.ShapeDtypeStruct((R_ALL, LANES), F32)]
    return pl.pallas_call(
        functools.partial(_normmod_body, with_router=router is not None),
        grid=(R_ALL // TR,),
        in_specs=in_specs,
        out_specs=out_specs,
        out_shape=out_shape,
        compiler_params=_cparams(1),
        name="normmod",
    )(*args)


def _mm_body(*refs, n_pref, n_a, n_w, n_aux, epilogue, a_of_w, split_ctx):
    pref = refs[:n_pref]
    refs = refs[n_pref:]
    if split_ctx:
        a_refs, a_ctx_refs = refs[:2 * n_a:2], refs[1:2 * n_a:2]
        refs = refs[n_a:]
    else:
        a_refs = a_ctx_refs = refs[:n_a]
    w_refs = refs[n_a:n_a + n_w]
    aux_refs = refs[n_a + n_w:n_a + n_w + n_aux]
    o_ref = refs[n_a + n_w + n_aux]
    wbf_refs = refs[n_a + n_w + n_aux + 1:n_a + 2 * n_w + n_aux + 1]
    aux_refs = tuple(aux_refs) + tuple(refs[n_a + 2 * n_w + n_aux + 1:])
    j = pl.program_id(0)
    i = pl.program_id(1)
    if n_pref:
        te_ref, nu_ref = pref
        new_weights = (i == 0) | (te_ref[i] != te_ref[jnp.maximum(i - 1, 0)])
        valid = i < nu_ref[0]
    else:
        new_weights = i == 0
        valid = None

    @pl.when(new_weights)
    def _():
        for w_ref, wbf_ref in zip(w_refs, wbf_refs):
            wbf_ref[...] = w_ref[...].astype(BF16)

    def compute(srcs):
        accs = [jnp.dot(srcs[a_of_w[k]][...], wbf_refs[k][...], preferred_element_type=F32)
                for k in range(n_w)]
        epilogue(accs, aux_refs, o_ref, j, i)

    if split_ctx:
        pl.when(i < CTX_TILE)(functools.partial(compute, a_refs))
        pl.when(i >= CTX_TILE)(functools.partial(compute, a_ctx_refs))
    elif valid is None:
        compute(a_refs)
    else:
        pl.when(valid)(functools.partial(compute, a_refs))

        @pl.when(jnp.logical_not(valid))
        def _():
            o_ref[...] = jnp.zeros(o_ref.shape, o_ref.dtype)


def _mm(a_list, w_list, aux_list, epilogue, *, n_out, out_dtype, tn, a_of_w=None,
        n_rows=R_ALL, tm=TM, alias_aux=None, group=None, split_ctx=False, w_col0=0,
        epilogue_scratch=(), name="mm"):
    n_a, n_w, n_aux = len(a_list), len(w_list), len(aux_list)
    n_pref = 0 if group is None else 2
    a_of_w = tuple(a_of_w) if a_of_w is not None else tuple(range(n_w))
    in_specs, args, scratch = [], [], []
    for a in a_list:
        if split_ctx:
            lat, ctx = a
            assert tm == R_CTX and group is None
            in_specs.append(pl.BlockSpec((tm, lat.shape[1]),
                                         lambda j, i: (jnp.minimum(i, CTX_TILE - 1), 0)))
            in_specs.append(pl.BlockSpec((tm, ctx.shape[1]), lambda j, i: (0, 0),
                                         pipeline_mode=pl.Buffered(1)))
            args += [lat, ctx]
        else:
            in_specs.append(pl.BlockSpec((tm, a.shape[1]), lambda j, i, *_: (i, 0)))
            args.append(a)
    for w, prefix in w_list:
        k = w.shape[-2]
        if group is None:
            wmap = functools.partial(lambda j, i, p: p + (0, w_col0 + j), p=tuple(prefix))
        else:
            wmap = functools.partial(lambda j, i, te, nu, p: p[:-1] + (te[i], 0, j), p=tuple(prefix))
        in_specs.append(pl.BlockSpec((None,) * len(prefix) + (k, tn), wmap))
        args.append(w)
        scratch.append(pltpu.VMEM((k, tn), BF16))
    for arr, block, imap in aux_list:
        in_specs.append(pl.BlockSpec(block, functools.partial(lambda j, i, *_, f: f(j, i), f=imap)))
        args.append(arr)
    aliases = {}
    if alias_aux is not None:
        aliases = {n_pref + (2 * n_a if split_ctx else n_a) + n_w + alias_aux: 0}
    grid_spec = pltpu.PrefetchScalarGridSpec(
        num_scalar_prefetch=n_pref,
        grid=(n_out // tn, n_rows // tm),
        in_specs=in_specs,
        out_specs=pl.BlockSpec((tm, tn), lambda j, i, *_: (i, j)),
        scratch_shapes=scratch + list(epilogue_scratch),
    )
    return pl.pallas_call(
        functools.partial(_mm_body, n_pref=n_pref, n_a=n_a, n_w=n_w, n_aux=n_aux,
                          epilogue=epilogue, a_of_w=a_of_w, split_ctx=split_ctx),
        grid_spec=grid_spec,
        out_shape=jax.ShapeDtypeStruct((n_rows, n_out), out_dtype),
        input_output_aliases=aliases,
        compiler_params=_cparams(2),
        name=name,
    )(*(list(group) if group is not None else []), *args)


IN_TN = 512
IN_TN_WIDE = 1024

QK_ROWS = 512


def _qk_epilogue(accs, aux, o_ref, j, i):
    gains_ref, cos_ref, sin_ref, acc_ref = aux
    acc_ref[...] = accs[0]
    gain = gains_ref[pl.ds(j // (OFF_K // IN_TN), 1), :]
    ones = jnp.ones((HEAD_DIM, HEAD_DIM), BF16)

    def rows(r, carry):
        r0 = pl.multiple_of(r * QK_ROWS, QK_ROWS)
        cos = cos_ref[pl.ds(r0, QK_ROWS), :]
        sin = sin_ref[pl.ds(r0, QK_ROWS), :]
        for h in range(IN_TN // HEAD_DIM):
            xh = acc_ref[pl.ds(r0, QK_ROWS), h * HEAD_DIM:(h + 1) * HEAD_DIM]
            ss = jnp.dot((xh * xh).astype(BF16), ones, preferred_element_type=F32)
            y = xh * lax.rsqrt(ss * (1.0 / HEAD_DIM) + EPS) * gain
            partner = pltpu.roll(y, HEAD_DIM // 2, 1)
            o_ref[pl.ds(r0, QK_ROWS), h * HEAD_DIM:(h + 1) * HEAD_DIM] = (
                y * cos + partner * sin).astype(BF16)
        return carry

    lax.fori_loop(0, TM // QK_ROWS, rows, 0)


def _cast_epilogue(accs, aux, o_ref, j, i):
    o_ref[...] = accs[0].astype(BF16)


def _gate_epilogue(accs, aux, o_ref, j, i):
    o_ref[...] = jax.nn.sigmoid(accs[0] + aux[0][...]).astype(BF16)


def _pair_major(a, axis=-1):
    a = jnp.moveaxis(a, axis, -1)
    lead = a.shape[:-1]
    a = a.reshape(lead + (a.shape[-1] // HEAD_DIM, 2, 2, ROT_FREQS))
    a = jnp.swapaxes(a, -3, -2).reshape(lead + (-1,))
    return jnp.moveaxis(a, -1, axis)


def _inproj(h, w_in, w_qk, l, gains, cos_t, sin_t, b_gate):
    def rope_map(j, i):
        return (jnp.where(i >= CTX_TILE, SEQ // TM, i % (SEQ // TM)), 0)

    w = [(w_in, (l,))]
    qk = _mm([h], [(w_qk, (l,))], [(gains, (None, 8, HEAD_DIM), lambda j, i: (l, 0, 0)),
                      (cos_t, (TM, HEAD_DIM), rope_map), (sin_t, (TM, HEAD_DIM), rope_map)],
             _qk_epilogue, n_out=OFF_V, out_dtype=BF16, tn=IN_TN,
             epilogue_scratch=[pltpu.VMEM((TM, IN_TN), F32)], name="inproj_qk")
    vu = _mm([h], w, [], _cast_epilogue, n_out=OFF_G - OFF_V, out_dtype=BF16, tn=IN_TN_WIDE,
             w_col0=OFF_V // IN_TN_WIDE, name="inproj_vu")
    gates = _mm([h], w, [(b_gate.reshape(DEPTH, 1, 2 * D_MODEL), (None, 1, IN_TN_WIDE),
                          lambda j, i: (l, 0, j))],
                _gate_epilogue, n_out=2 * D_MODEL, out_dtype=BF16, tn=IN_TN_WIDE,
                w_col0=OFF_G // IN_TN_WIDE, name="inproj_gate")
    return qk, vu, gates


MG_TN = 512


def _merge_epilogue(accs, aux, o_ref, j, i):
    ga_ref, gb_ref = aux
    o_ref[...] = (ga_ref[...].astype(F32) * accs[0] + gb_ref[...].astype(F32) * accs[1]).astype(BF16)


def _merge(ya, yb, gates, w_br_a, w_br_b, l):
    aux = [
        (gates, (TM, MG_TN), lambda j, i: (i, j)),
        (gates, (TM, MG_TN), lambda j, i: (i, D_MODEL // MG_TN + j)),
    ]
    return _mm([ya, yb], [(w_br_a, (l,)), (w_br_b, (l,))], aux, _merge_epilogue,
               n_out=D_MODEL, out_dtype=BF16, tn=MG_TN, split_ctx=True, name="merge")


RS_TN = 512
FFD_TM = 512


def _resid_epilogue(accs, aux, o_ref, j, i, tm):
    x_ref, g_ref = aux
    o_ref[...] = x_ref[...] + g_ref[pl.ds(_row_seg(i, tm), 1), :] * accs[0]


def _resid_proj(a, w, prefix, x, mods, l, which, tm=TM, name="resid"):
    nb = D_MODEL // RS_TN
    aux = [
        (x, (tm, RS_TN), lambda j, i: (i, j)),
        (mods, (None, 8, RS_TN), lambda j, i: (l, 0, which * nb + j)),
    ]
    return _mm([a], [(w, prefix)], aux, functools.partial(_resid_epilogue, tm=tm),
               n_out=D_MODEL, out_dtype=F32, tm=tm, tn=RS_TN, alias_aux=0, name=name)


FF_TN = 512


def _swiglu_epilogue(accs, aux, o_ref, j, i):
    a = accs[0]
    o_ref[...] = (a * jax.nn.sigmoid(a) * accs[1]).astype(BF16)


def _swiglu_up(h, w1, w3, prefix, group=None):
    kw = dict(n_rows=MOE_P, tm=MOE_TM) if group is not None else {}
    return _mm([h], [(w1, prefix), (w3, prefix)], [], _swiglu_epilogue, n_out=D_FF,
               out_dtype=BF16, tn=FF_TN, a_of_w=(0, 0), group=group, name="swiglu_up", **kw)


MOE_TM = 512
MOE_A = TOP_K * R_ALL
MOE_NT = MOE_A // MOE_TM + N_EXPERTS
MOE_P = MOE_NT * MOE_TM
GATHER_ROWS = 256


def _route(logits):
    top_v, top_i = lax.top_k(logits[:, :N_EXPERTS], TOP_K)
    wts = jax.nn.softmax(top_v, axis=-1).reshape(-1)
    e_flat = top_i.reshape(-1).astype(jnp.int32)
    onehot = (e_flat[:, None] == jnp.arange(N_EXPERTS, dtype=jnp.int32)[None, :]).astype(jnp.int32)
    before = jnp.cumsum(onehot, axis=0) - onehot
    rank = jnp.sum(before * onehot, axis=1)
    counts = jnp.sum(onehot, axis=0)
    tiles = (counts + MOE_TM - 1) // MOE_TM
    tile_end = jnp.cumsum(tiles)
    pos = ((tile_end - tiles)[e_flat] * MOE_TM + rank).astype(jnp.int32)
    n_used = tile_end[-1:].astype(jnp.int32)
    tile_ids = jnp.minimum(jnp.arange(MOE_NT, dtype=jnp.int32), n_used[0] - 1)
    tile_expert = jnp.sum((tile_ids[:, None] >= tile_end[None, :]).astype(jnp.int32), axis=1)
    token = jnp.arange(MOE_A, dtype=jnp.int32) // TOP_K
    src_token = jnp.zeros((MOE_P,), jnp.int32).at[pos].set(token)
    return pos, src_token, wts, tile_expert.astype(jnp.int32), n_used


def _gather_body(src_ref, nu_ref, h_ref, o_ref, buf_ref, sem):
    t = pl.program_id(0)
    n_rows = nu_ref[0] * MOE_TM
    slot = t % 2

    def row_copy(slot_, r, src_row):
        return pltpu.make_async_copy(h_ref.at[pl.ds(src_row, 1), :],
                                     buf_ref.at[slot_, pl.ds(r, 1), :], sem.at[slot_])

    def issue(step, slot_):
        def body(r, carry):
            row_copy(slot_, r, src_ref[step * GATHER_ROWS + r]).start()
            return carry

        lax.fori_loop(0, GATHER_ROWS, body, 0, unroll=8)

    @pl.when((t == 0) & (n_rows > 0))
    def _():
        issue(0, 0)

    @pl.when((t + 1 < pl.num_programs(0)) & ((t + 1) * GATHER_ROWS < n_rows))
    def _():
        issue(t + 1, 1 - slot)

    @pl.when(t * GATHER_ROWS < n_rows)
    def _():
        def drain(r, carry):
            row_copy(slot, r, 0).wait()
            return carry

        lax.fori_loop(0, GATHER_ROWS, drain, 0, unroll=8)
        o_ref[...] = buf_ref[slot].astype(BF16)

    @pl.when(t * GATHER_ROWS >= n_rows)
    def _():
        o_ref[...] = jnp.zeros(o_ref.shape, o_ref.dtype)


def _moe_gather(h32, src_token, n_used):
    grid_spec = pltpu.PrefetchScalarGridSpec(
        num_scalar_prefetch=2,
        grid=(MOE_P // GATHER_ROWS,),
        in_specs=[pl.BlockSpec(memory_space=pl.ANY)],
        out_specs=pl.BlockSpec((GATHER_ROWS, D_MODEL), lambda t, *_: (t, 0)),
        scratch_shapes=[pltpu.VMEM((2, GATHER_ROWS, D_MODEL), F32), pltpu.SemaphoreType.DMA((2,))],
    )
    return pl.pallas_call(
        _gather_body,
        grid_spec=grid_spec,
        out_shape=jax.ShapeDtypeStruct((MOE_P, D_MODEL), BF16),
        compiler_params=_cparams(1),
        name="moe_gather",
    )(src_token, n_used, h32)


def _moe_down_epilogue(accs, aux, o_ref, j, i):
    o_ref[...] = accs[0]


def _combine_body(pos_ref, y_ref, w_ref, x_ref, g_ref, o_ref, buf_ref, sem):
    t = pl.program_id(0)
    slot = t % 2

    def row_copy(slot_, k, r, src_row):
        return pltpu.make_async_copy(y_ref.at[pl.ds(src_row, 1), :],
                                     buf_ref.at[slot_, k, pl.ds(r, 1), :], sem.at[slot_])

    def issue(step, slot_):
        def body(r, carry):
            for k in range(TOP_K):
                row_copy(slot_, k, r, pos_ref[TOP_K * (step * TR + r) + k]).start()
            return carry

        lax.fori_loop(0, TR, body, 0, unroll=4)

    @pl.when(t == 0)
    def _():
        issue(0, 0)

    @pl.when(t + 1 < pl.num_programs(0))
    def _():
        issue(t + 1, 1 - slot)

    def drain(r, carry):
        for k in range(TOP_K):
            row_copy(slot, k, r, 0).wait()
        return carry

    lax.fori_loop(0, TR, drain, 0, unroll=4)
    g = g_ref[pl.ds(_row_seg(t, TR), 1), :]
    y = w_ref[0] * buf_ref[slot, 0]
    for k in range(1, TOP_K):
        y = y + w_ref[k] * buf_ref[slot, k]
    o_ref[...] = x_ref[...] + g * y


def _moe_combine(ys, pos, wts, x, mods, l, which):
    w_cols = jnp.broadcast_to(wts.reshape(R_ALL, TOP_K).T[:, :, None], (TOP_K, R_ALL, 1))
    grid_spec = pltpu.PrefetchScalarGridSpec(
        num_scalar_prefetch=1,
        grid=(R_ALL // TR,),
        in_specs=[pl.BlockSpec(memory_space=pl.ANY),
                  pl.BlockSpec((TOP_K, TR, 1), lambda t, *_: (0, t, 0)),
                  pl.BlockSpec((TR, D_MODEL), lambda t, *_: (t, 0)),
                  pl.BlockSpec((None, 8, D_MODEL), lambda t, *_: (l, 0, which))],
        out_specs=pl.BlockSpec((TR, D_MODEL), lambda t, *_: (t, 0)),
        scratch_shapes=[pltpu.VMEM((2, TOP_K, TR, D_MODEL), F32), pltpu.SemaphoreType.DMA((2,))],
    )
    return pl.pallas_call(
        _combine_body,
        grid_spec=grid_spec,
        out_shape=jax.ShapeDtypeStruct((R_ALL, D_MODEL), F32),
        input_output_aliases={3: 0},
        compiler_params=_cparams(1),
        name="moe_combine",
    )(pos, ys, w_cols, x, mods)


def _moe(h32, logits, xs, mods, l, i, moe_w1, moe_w3, moe_w2):
    pos, src_token, wts, tile_expert, n_used = _route(logits)
    group = (tile_expert, n_used)
    hs = _moe_gather(h32, src_token, n_used)
    g = _swiglu_up(hs, moe_w1, moe_w3, (i, 0), group=group)
    ys = _mm([g], [(moe_w2, (i, 0))], [], _moe_down_epilogue, n_out=D_MODEL, out_dtype=F32,
             tn=RS_TN, n_rows=MOE_P, tm=MOE_TM, group=group, name="moe_down")
    return _moe_combine(ys, pos, wts, xs, mods, l, 5)


AT_TQ = 512
AT_TK = 1024


ATT_SAFE_BOUND = 60.0


def _attn_scores(qs, k_ref, sl, c0, c1):
    return lax.dot_general(qs, k_ref[c0:c1, sl], (((1,), (1,)), ((), ())),
                           preferred_element_type=F32)


def _attn_body(lam_ref, q_ref, *rest, n_seg, post_scale):
    k_refs = rest[:n_seg]
    v_refs = rest[n_seg:2 * n_seg]
    sub_ref = rest[2 * n_seg]
    o_ref = rest[2 * n_seg + 1]
    kmax_ref = rest[2 * n_seg + 2]
    num_ref = rest[2 * n_seg + 3]
    lam = lam_ref[0]
    subs = [slice(s * HEAD_DIM, (s + 1) * HEAD_DIM) for s in range(2)]

    @pl.when(pl.program_id(2) == 0)
    def _():
        for s, sl in enumerate(subs):
            best = None
            for k_ref in k_refs:
                kf = k_ref[:, sl].astype(F32)
                n2 = jnp.max(jnp.sum(kf * kf, axis=-1, keepdims=True))
                best = n2 if best is None else jnp.maximum(best, n2)
            kmax_ref[s] = best

    qs = [q_ref[:, sl] for sl in subs]
    bounds = []
    for s in range(2):
        qf = qs[s].astype(F32)
        qn2 = jnp.sum(qf * qf, axis=-1, keepdims=True)
        bounds.append(jnp.sqrt(qn2 * kmax_ref[s]) * 1.001 + 1e-6)
    worst = jnp.maximum(jnp.max(bounds[0]), jnp.max(bounds[1]))

    @pl.when(worst <= ATT_SAFE_BOUND)
    def _():
        for s, sl in enumerate(subs):
            den = num = None
            for k_ref, v_ref in zip(k_refs, v_refs):
                p = jnp.exp2(_attn_scores(qs[s], k_ref, sl, 0, k_ref.shape[0]) - bounds[s])
                ps = p.sum(axis=-1, keepdims=True)
                pv = jnp.dot(p.astype(BF16), v_ref[...], preferred_element_type=F32)
                den = ps if den is None else den + ps
                num = pv if num is None else num + pv
            num_ref[s] = num / den

    @pl.when(worst > ATT_SAFE_BOUND)
    def _():
        for s, sl in enumerate(subs):
            m = den = num = None
            for k_ref, v_ref in zip(k_refs, v_refs):
                seg_len = k_ref.shape[0]
                for c0 in range(0, seg_len, AT_TK):
                    c1 = min(c0 + AT_TK, seg_len)
                    sc = _attn_scores(qs[s], k_ref, sl, c0, c1)
                    mc = sc.max(axis=-1, keepdims=True)
                    m_new = mc if m is None else jnp.maximum(m, mc)
                    p = jnp.exp2(sc - m_new)
                    ps = p.sum(axis=-1, keepdims=True)
                    pv = jnp.dot(p.astype(BF16), v_ref[c0:c1, :], preferred_element_type=F32)
                    if m is None:
                        den, num = ps, pv
                    else:
                        alpha = jnp.exp2(m - m_new)
                        den = alpha * den + ps
                        num = alpha * num + pv
                    m = m_new
            num_ref[s] = num / den

    d = num_ref[0] - lam * num_ref[1]
    ms = jnp.mean(d * d, axis=-1, keepdims=True)
    o_ref[...] = (d * lax.rsqrt(ms + EPS) * sub_ref[...] * post_scale).astype(BF16)


def _attention(qk, vu, lam, sub_norm, l, lam_init, ctx_only=False):
    kb = OFF_K // V_HEAD_DIM
    ctx_blk0 = R_LAT // CTX_LEN
    if ctx_only:
        tq = CTX_LEN
        grid = (BATCH, N_DIFF_HEADS, 1)
        q_map = lambda b, h, t: (ctx_blk0 + b, h)
        o_map = lambda b, h, t: (b, h)
        segs = [CTX_LEN]
    else:
        tq = AT_TQ
        grid = (BATCH, N_DIFF_HEADS, SEQ // tq)
        q_map = o_map = lambda b, h, t: (b * (SEQ // tq) + t, h)
        segs = [CTX_LEN, SEQ]
    in_specs = [pl.BlockSpec(memory_space=pltpu.SMEM),
                pl.BlockSpec((tq, V_HEAD_DIM), q_map)]
    args = [lam.reshape(1), qk]
    for arr, off in ((qk, kb), (vu, 0)):
        for seg_len in segs:
            if seg_len == CTX_LEN:
                imap = functools.partial(lambda b, h, t, o: (ctx_blk0 + b, o + h), o=off)
            else:
                imap = functools.partial(lambda b, h, t, o: (b, o + h), o=off)
            in_specs.append(pl.BlockSpec((seg_len, V_HEAD_DIM), imap))
            args.append(arr)
    in_specs.append(pl.BlockSpec((None, 1, V_HEAD_DIM), lambda b, h, t: (l, 0, 0)))
    args.append(sub_norm.reshape(DEPTH, 1, V_HEAD_DIM))
    return pl.pallas_call(
        functools.partial(_attn_body, n_seg=len(segs), post_scale=1.0 - lam_init),
        grid=grid,
        in_specs=in_specs,
        out_specs=pl.BlockSpec((tq, V_HEAD_DIM), o_map),
        out_shape=jax.ShapeDtypeStruct((R_CTX if ctx_only else R_LAT, ATT_W), BF16),
        scratch_shapes=[pltpu.SMEM((2,), F32), pltpu.VMEM((2, tq, V_HEAD_DIM), F32)],
        compiler_params=_cparams(3),
        name="attn_ctx" if ctx_only else "attn",
    )(*args)


def _dft_tables():
    n, n1, n2 = FFT_N, FFT_N1, FFT_N2
    k1 = np.arange(n1)[:, None]
    tabs_d, tabs_s, tabs_i = [], [], []
    for b in range(n2):
        cols = n2 * np.arange(n1)[None, :] + b
        g = np.exp(-2j * np.pi * ((k1 * cols) % n) / n)
        gd = g[:, :n1 // 2]
        tabs_d.append(np.block([[gd.real, -gd.imag], [gd.imag, gd.real]]))
        tabs_s.append(np.concatenate([g.real, g.imag], axis=0))
        gi = np.conj(gd).T / n1
        tabs_i.append(np.block([[gi.real, -gi.imag], [gi.imag, gi.real]]))
    kk = np.arange(n2)
    f2 = np.exp(-2j * np.pi * ((kk[:, None] * kk[None, :]) % n2) / n2)
    g2 = np.block([[f2.real, -f2.imag], [f2.imag, f2.real]])
    f2i = np.conj(f2) / n2
    g2i = np.block([[f2i.real, -f2i.imag], [f2i.imag, f2i.real]])
    as_bf = lambda a: jnp.asarray(np.asarray(a, np.float32)).astype(BF16)
    return (as_bf(np.stack(tabs_d)), as_bf(np.stack(tabs_s)), as_bf(np.stack(tabs_i)),
            as_bf(g2), as_bf(g2i))


FFT_UNROLL = 32


def _pack_pair(a, b):
    ab = lax.bitcast_convert_type(a.astype(BF16).astype(F32), jnp.uint32)
    bb = lax.bitcast_convert_type(b.astype(BF16).astype(F32), jnp.uint32)
    return lax.bitcast_convert_type(ab | (bb >> 16), F32)


def _unpack_pair(w):
    bits = lax.bitcast_convert_type(w, jnp.uint32)
    hi = lax.bitcast_convert_type(bits & jnp.uint32(0xFFFF0000), F32)
    lo = lax.bitcast_convert_type(bits << 16, F32)
    return hi, lo


def _rows_of_group(ref3, g):
    return jnp.swapaxes(ref3[:, pl.ds(pl.multiple_of(g * 8, 8), 8), :], 0, 1)


def _spectrum_body(hf_ref, hr_ref, wf_ref, wb_ref, dl_ref, g1_ref, g2_ref, or_ref, oi_ref,
                   k_ref, a_ref):
    n1, n2 = FFT_N1, FFT_N2
    delta = dl_ref[...]
    step = 1.0 / (SEQ - 1)
    rows = 512
    row_iota = lax.broadcasted_iota(jnp.int32, (rows, 1), 0)

    def taps(c, carry):
        r0 = pl.multiple_of(c * rows, rows)
        pos = (row_iota + r0).astype(F32)
        top = jnp.dot(hf_ref[pl.ds(r0, rows), :].astype(BF16), wf_ref[...].astype(BF16),
                      preferred_element_type=F32)
        bot = jnp.dot(hr_ref[pl.ds(r0, rows), :].astype(BF16), wb_ref[...].astype(BF16),
                      preferred_element_type=F32)
        blk = pl.multiple_of(c * (rows // n2), rows // n2)
        k_ref[pl.ds(blk, rows // n2)] = (top * jnp.exp(-(pos * step) * delta)).reshape(
            rows // n2, n2, HY_TC)
        k_ref[pl.ds(n1 // 2 + blk, rows // n2)] = (
            bot * jnp.exp(-((SEQ - pos) * step) * delta)).reshape(rows // n2, n2, HY_TC)
        return carry

    lax.fori_loop(0, SEQ // rows, taps, 0)

    def stage1(g, carry):
        xs = _rows_of_group(k_ref, g)
        for j in range(8):
            a = jnp.dot(g1_ref[g * 8 + j], xs[j].astype(BF16), preferred_element_type=F32)
            a_ref[g * 8 + j] = _pack_pair(a[:n1], a[n1:])
        return carry

    lax.fori_loop(0, n2 // 8, stage1, 0, unroll=FFT_UNROLL // 8)

    def stage2(g, carry):
        xs = _rows_of_group(a_ref, g)
        for j in range(8):
            a = jnp.concatenate(_unpack_pair(xs[j]), axis=0).astype(BF16)
            x = jnp.dot(g2_ref[...], a, preferred_element_type=F32)
            row = pl.multiple_of((g * 8 + j) * n2, n2)
            or_ref[pl.ds(row, n2), :] = x[:n2].astype(BF16)
            oi_ref[pl.ds(row, n2), :] = x[n2:].astype(BF16)
        return carry

    lax.fori_loop(0, n1 // 8, stage2, 0, unroll=FFT_UNROLL // 8)


def _const_spec(shape, n_grid):
    nd = len(shape)
    imap = (lambda a, b: (0,) * nd) if n_grid == 2 else (lambda a: (0,) * nd)
    return pl.BlockSpec(shape, imap, pipeline_mode=pl.Buffered(1))


def _spectrum(hid_f, hid_r, w4p, deltas, l, g1s, g2):
    tc = HY_TC
    nct = HY_W // tc
    hspec = pl.BlockSpec((SEQ, LANES), lambda f, ct: (0, 0))
    wspec = lambda d: pl.BlockSpec(
        (None, LANES, tc), functools.partial(lambda f, ct, d: (l, 0, (2 * f + d) * nct + ct), d=d))
    ospec = pl.BlockSpec((None, FFT_N, tc), lambda f, ct: (f, 0, ct))
    return pl.pallas_call(
        _spectrum_body,
        grid=(HY_N_FILT, nct),
        in_specs=[hspec, hspec, wspec(0), wspec(1),
                  pl.BlockSpec((1, tc), lambda f, ct: (0, ct)),
                  _const_spec((FFT_N2, 2 * FFT_N1, FFT_N1), 2),
                  _const_spec((2 * FFT_N2, 2 * FFT_N2), 2)],
        out_specs=[ospec, ospec],
        out_shape=[jax.ShapeDtypeStruct((HY_N_FILT, FFT_N, HY_W), BF16)] * 2,
        scratch_shapes=[pltpu.VMEM((FFT_N1, FFT_N2, tc), F32), pltpu.VMEM((FFT_N2, FFT_N1, tc), F32)],
        compiler_params=_cparams(2),
        name="hy_spectrum",
    )(hid_f, hid_r, w4p, w4p, deltas, g1s, g2)


def _conv3(u, w_ref, b_ref):
    n = u.shape[0]
    row = lax.broadcasted_iota(jnp.int32, u.shape, 0)
    prev = jnp.where(row == 0, 0.0, pltpu.roll(u, 1, 0))
    nxt = jnp.where(row == n - 1, 0.0, pltpu.roll(u, n - 1, 0))
    return prev * w_ref[0:1, :] + u * w_ref[1:2, :] + nxt * w_ref[2:3, :] + b_ref[...]


def _hyconv_body(s0_ref, s1_ref, m0_ref, m1_ref, cws_ref, cbs_ref, cwm_ref, cbm_ref, bias_ref,
                 kr_ref, ki_ref, g1_ref, g1i_ref, g2_ref, g2i_ref, o_ref,
                 vr_ref, vi_ref, vp_ref, a_ref, b_ref, *, conv_signal):
    n1, n2 = FFT_N1, FFT_N2
    h1 = n1 // 2
    for s_ref, v_ref in ((s0_ref, vr_ref), (s1_ref, vi_ref)):
        u = s_ref[...].astype(F32)
        v_ref[...] = _conv3(u, cws_ref, cbs_ref) if conv_signal else u
    vp_ref[...] = _pack_pair(vr_ref[...], vi_ref[...]).reshape(h1, n2, HY_TC)

    def stage1(g, carry):
        xs = _rows_of_group(vp_ref, g)
        for j in range(8):
            x = jnp.concatenate(_unpack_pair(xs[j]), axis=0).astype(BF16)
            a = jnp.dot(g1_ref[g * 8 + j], x, preferred_element_type=F32)
            a_ref[g * 8 + j] = _pack_pair(a[:n1], a[n1:])
        return carry

    lax.fori_loop(0, n2 // 8, stage1, 0, unroll=FFT_UNROLL // 8)

    def stage2(g, carry):
        xs = _rows_of_group(a_ref, g)
        for j in range(8):
            k1 = g * 8 + j
            a = jnp.concatenate(_unpack_pair(xs[j]), axis=0).astype(BF16)
            x = jnp.dot(g2_ref[...], a, preferred_element_type=F32)
            row = pl.multiple_of(k1 * n2, n2)
            fr = kr_ref[pl.ds(row, n2), :].astype(F32)
            fi = ki_ref[pl.ds(row, n2), :].astype(F32)
            xr, xi = x[:n2], x[n2:]
            y = jnp.concatenate([xr * fr - xi * fi, xr * fi + xi * fr], axis=0).astype(BF16)
            bq = jnp.dot(g2i_ref[...], y, preferred_element_type=F32)
            b_ref[k1] = _pack_pair(bq[:n2], bq[n2:])
        return carry

    lax.fori_loop(0, n1 // 8, stage2, 0, unroll=FFT_UNROLL // 8)

    def stage1_inv(g, carry):
        xs = _rows_of_group(b_ref, g)
        for j in range(8):
            bq = jnp.concatenate(_unpack_pair(xs[j]), axis=0).astype(BF16)
            y = jnp.dot(g1i_ref[g * 8 + j], bq, preferred_element_type=F32)
            a_ref[g * 8 + j, pl.ds(0, h1), :] = _pack_pair(y[:h1], y[h1:])
        return carry

    lax.fori_loop(0, n2 // 8, stage1_inv, 0, unroll=FFT_UNROLL // 8)

    bias = bias_ref[...]

    def unpermute(g, carry):
        xs = _rows_of_group(a_ref, g)
        for j in range(8):
            row = pl.multiple_of((g * 8 + j) * n2, n2)
            yr, yi = _unpack_pair(xs[j])
            vr_ref[pl.ds(row, n2), :] = yr + bias * vr_ref[pl.ds(row, n2), :]
            vi_ref[pl.ds(row, n2), :] = yi + bias * vi_ref[pl.ds(row, n2), :]
        return carry

    lax.fori_loop(0, h1 // 8, unpermute, 0, unroll=FFT_UNROLL // 8)

    for half, (m_ref, v_ref) in enumerate(((m0_ref, vr_ref), (m1_ref, vi_ref))):
        mult = _conv3(m_ref[...].astype(F32), cwm_ref, cbm_ref)
        o_ref[half * SEQ:(half + 1) * SEQ, :] = (mult * v_ref[...]).astype(o_ref.dtype)


def _hy_param_specs(conv_w, conv_b, bias, l, filt, sig_ch, mul_ch, tc):
    cw3 = conv_w.reshape(DEPTH, 3, 3 * HY_W)
    cb3 = conv_b.reshape(DEPTH, 1, 3 * HY_W)
    chan = lambda ch0, rows: pl.BlockSpec(
        (None, rows, tc), functools.partial(lambda ct, pr, c0: (l, 0, c0 + ct), c0=ch0))
    specs = [chan(sig_ch, 3), chan(sig_ch, 1), chan(mul_ch, 3), chan(mul_ch, 1),
             pl.BlockSpec((None, None, 1, tc), lambda ct, pr: (l, filt, 0, ct))]
    args = [cw3, cb3, cw3, cb3, bias.reshape(DEPTH, HY_N_FILT, 1, HY_W)]
    return specs, args


def _hyconv(sig, sig_cb, mul, mul_cb, conv_w, conv_b, bias, l, filt, sig_ch, mul_ch,
            kr, ki, tabs, conv_signal):
    g1d, g1i, g2, g2i = tabs
    tc = HY_TC
    blk = lambda arr_cb, odd: pl.BlockSpec(
        (SEQ, tc), functools.partial(lambda ct, pr, cb, o: (2 * pr + o, cb + ct), cb=arr_cb, o=odd))
    pspecs, pargs = _hy_param_specs(conv_w, conv_b, bias, l, filt, sig_ch, mul_ch, tc)
    kspec = pl.BlockSpec((None, FFT_N, tc), lambda ct, pr: (filt, 0, ct))
    in_specs = [blk(sig_cb, 0), blk(sig_cb, 1), blk(mul_cb, 0), blk(mul_cb, 1)] + pspecs + [
        kspec, kspec,
        _const_spec((FFT_N2, 2 * FFT_N1, FFT_N1), 2),
        _const_spec((FFT_N2, FFT_N1, 2 * FFT_N1), 2),
        _const_spec((2 * FFT_N2, 2 * FFT_N2), 2),
        _const_spec((2 * FFT_N2, 2 * FFT_N2), 2),
    ]
    return pl.pallas_call(
        functools.partial(_hyconv_body, conv_signal=conv_signal),
        grid=(HY_W // tc, BATCH // 2),
        in_specs=in_specs,
        out_specs=pl.BlockSpec((2 * SEQ, tc), lambda ct, pr: (pr, ct)),
        out_shape=jax.ShapeDtypeStruct((R_LAT, HY_W), BF16),
        scratch_shapes=[pltpu.VMEM((SEQ, tc), F32)] * 2 + [pltpu.VMEM((FFT_N1 // 2, FFT_N2, tc), F32),
                        pltpu.VMEM((FFT_N2, FFT_N1, tc), F32), pltpu.VMEM((FFT_N1, FFT_N2, tc), F32)],
        compiler_params=_cparams(2),
        name="hyconv",
    )(sig, sig, mul, mul, *pargs, kr, ki, g1d, g1i, g2, g2i)


def _ctx_dft_tables():
    n = 2 * CTX_LEN
    kk = np.arange(n)
    f = np.exp(-2j * np.pi * ((kk[:, None] * kk[None, :]) % n) / n)
    fd = f[:, :CTX_LEN]
    fwd = np.block([[fd.real, -fd.imag], [fd.imag, fd.real]])
    spec = np.concatenate([f.real, f.imag], axis=0)
    fi = np.conj(f)[:CTX_LEN, :] / n
    inv = np.block([[fi.real, -fi.imag], [fi.imag, fi.real]])
    as_bf = lambda a: jnp.asarray(np.asarray(a, np.float32)).astype(BF16)
    return as_bf(fwd), as_bf(spec), as_bf(inv)


def _hyconv_ctx_body(s0_ref, s1_ref, m0_ref, m1_ref, cws_ref, cbs_ref, cwm_ref, cbm_ref, bias_ref,
                     k_ref, fwd_ref, spec_ref, inv_ref, o_ref, *, conv_signal):
    n = 2 * CTX_LEN
    vs = []
    for s_ref in (s0_ref, s1_ref):
        u = s_ref[...].astype(F32)
        vs.append(_conv3(u, cws_ref, cbs_ref) if conv_signal else u)
    x = jnp.dot(fwd_ref[...], jnp.concatenate(vs, axis=0).astype(BF16), preferred_element_type=F32)
    kf = jnp.dot(spec_ref[...], k_ref[...].astype(BF16), preferred_element_type=F32)
    kf = kf.astype(BF16).astype(F32)
    xr, xi, fr, fi = x[:n], x[n:], kf[:n], kf[n:]
    y = jnp.concatenate([xr * fr - xi * fi, xr * fi + xi * fr], axis=0).astype(BF16)
    conv = jnp.dot(inv_ref[...], y, preferred_element_type=F32)
    bias = bias_ref[...]
    for half, m_ref in enumerate((m0_ref, m1_ref)):
        sl = slice(half * CTX_LEN, (half + 1) * CTX_LEN)
        mult = _conv3(m_ref[...].astype(F32), cwm_ref, cbm_ref)
        o_ref[sl, :] = (mult * (conv[sl] + bias * vs[half])).astype(o_ref.dtype)


def _hyconv_ctx(sig, sig_rb, sig_cb, mul, mul_rb, mul_cb, conv_w, conv_b, bias, l, filt,
                sig_ch, mul_ch, kctx, tabs, conv_signal):
    fwd, spec, inv = tabs
    tc = HY_TC
    blk = lambda rb, arr_cb, odd: pl.BlockSpec(
        (CTX_LEN, tc),
        functools.partial(lambda ct, pr, rb, cb, o: (rb + 2 * pr + o, cb + ct), rb=rb, cb=arr_cb, o=odd))
    pspecs, pargs = _hy_param_specs(conv_w, conv_b, bias, l, filt, sig_ch, mul_ch, tc)
    n = 2 * CTX_LEN
    in_specs = [blk(sig_rb, sig_cb, 0), blk(sig_rb, sig_cb, 1),
                blk(mul_rb, mul_cb, 0), blk(mul_rb, mul_cb, 1)] + pspecs + [
        pl.BlockSpec((None, n, tc), lambda ct, pr: (filt, 0, ct)),
        _const_spec((2 * n, n), 2),
        _const_spec((2 * n, n), 2),
        _const_spec((n, 2 * n), 2),
    ]
    return pl.pallas_call(
        functools.partial(_hyconv_ctx_body, conv_signal=conv_signal),
        grid=(HY_W // tc, BATCH // 2),
        in_specs=in_specs,
        out_specs=pl.BlockSpec((2 * CTX_LEN, tc), lambda ct, pr: (pr, ct)),
        out_shape=jax.ShapeDtypeStruct((R_CTX, HY_W), BF16),
        compiler_params=_cparams(2),
        name="hyconv_ctx",
    )(sig, sig, mul, mul, *pargs, kctx, fwd, spec, inv)


def _rope_tables():
    rows = SEQ // GRID_W
    row = jnp.repeat(jnp.arange(rows), GRID_W).astype(F32)
    col = jnp.tile(jnp.arange(GRID_W), rows).astype(F32)
    inv = ROPE_BASE ** (-jnp.arange(ROT_FREQS, dtype=F32) / ROT_FREQS)
    ar, ac = row[:, None] * inv, col[:, None] * inv
    cos = jnp.concatenate([jnp.cos(ar), jnp.cos(ac), jnp.cos(ar), jnp.cos(ac)], axis=1)
    sin = jnp.concatenate([-jnp.sin(ar), -jnp.sin(ac), jnp.sin(ar), jnp.sin(ac)], axis=1)
    cos = jnp.concatenate([cos, jnp.ones((TM, HEAD_DIM), F32)], axis=0)
    sin = jnp.concatenate([sin, jnp.zeros((TM, HEAD_DIM), F32)], axis=0)
    return cos, sin


def _hyena_positions(L, reverse):
    idx = jnp.arange(L, dtype=F32)
    return (L - idx) if reverse else idx


def _hyena_hidden(L, w1, b1, w2, b2, w3, b3, freq, reverse=False):
    pos = _hyena_positions(L, reverse)[:, None]
    t = pos / (L - 1)
    w = 2.0 * math.pi * pos / L
    f = jnp.linspace(1e-4, HY_BANDS - 1, HY_BANDS, dtype=F32)
    z = jnp.concatenate([t, jnp.cos(w * f), -jnp.sin(w * f)], axis=-1)
    dot = functools.partial(jnp.dot, precision=HIGHEST)
    hid = jnp.sin(freq * (dot(z, w1) + b1))
    hid = jnp.sin(freq * (dot(hid, w2) + b2))
    hid = jnp.sin(freq * (dot(hid, w3) + b3))
    if reverse:
        hid = jnp.where(jnp.arange(L)[:, None] > 0, hid, 0.0)
    return hid


def _hyena_deltas():
    return jnp.abs(jnp.linspace(math.log(HY_TARGET) / HY_SLOW_PCT,
                                math.log(HY_TARGET) / HY_FAST_PCT, HY_W, dtype=F32))


def _hyena_filters_ctx(hid, hid_rev, w4):
    L = hid.shape[0]
    w4 = w4.reshape(HY_FH, HY_N_FILT, 2, HY_W)
    halves = []
    for d, feats in enumerate((hid, hid_rev)):
        t = _hyena_positions(L, bool(d))[:, None] / (L - 1)
        taps = jnp.einsum('lh,hfc->flc', feats, w4[:, :, d], precision=HIGHEST)
        halves.append(taps * jnp.exp(-t * _hyena_deltas())[None])
    return jnp.concatenate(halves, axis=1)


def kernel(x, c, ctx, c_ctx, w_ada, b_ada, norm1, norm2, w_in, b_gate, q_norm, k_norm, lam_q1, lam_k1, lam_q2, lam_k2, sub_norm, hy_conv_w, hy_conv_b, hy_w1, hy_b1, hy_w2, hy_b2, hy_w3, hy_b3, hy_w4, hy_freq, hy_bias, w_br_a, w_br_b, w_out, ffn_w1, ffn_w3, ffn_w2, router, moe_w1, moe_w3, moe_w2):
    xs = jnp.concatenate([x.reshape(R_LAT, D_MODEL), ctx.reshape(R_CTX, D_MODEL)], axis=0)
    cs = jnp.concatenate([c, c_ctx[None], jnp.zeros((8 - BATCH - 1, D_MODEL), F32)], axis=0)
    mods = _adaln(cs, w_ada, b_ada)
    cos_t, sin_t = _rope_tables()
    g1d, g1s, g1i, g2, g2i = _dft_tables()
    ctx_tabs = _ctx_dft_tables()
    gains = jnp.zeros((DEPTH, 8, HEAD_DIM), F32)
    gains = gains.at[:, 0].set(_pair_major(q_norm) * (math.log2(math.e) / math.sqrt(HEAD_DIM)))
    gains = gains.at[:, 1].set(_pair_major(k_norm))
    w_qk = _pair_major(w_in[:, :, :OFF_V])
    router_pad = jnp.zeros((router.shape[0], D_MODEL, LANES), F32).at[:, :, :N_EXPERTS].set(router)
    hy_cb = ATT_W // HY_TC
    ch = HY_W // HY_TC
    w4p = jnp.pad(hy_w4, ((0, 0), (0, LANES - HY_FH), (0, 0)))
    deltas = _hyena_deltas()[None]

    for l in range(DEPTH):
        lam_init = 0.8 - 0.6 * math.exp(-0.3 * l)
        lam = (jnp.exp(jnp.sum(lam_q1[l] * lam_k1[l])) - jnp.exp(jnp.sum(lam_q2[l] * lam_k2[l]))
               + lam_init)
        h = _normmod(xs, norm1, mods, l, 0)
        qk, vu, gates = _inproj(h, w_in, w_qk, l, gains, cos_t, sin_t, b_gate)

        ya = (_attention(qk, vu, lam, sub_norm, l, lam_init),
              _attention(qk, vu, lam, sub_norm, l, lam_init, ctx_only=True))

        hp = (hy_w1[l], hy_b1[l], hy_w2[l], hy_b2[l], hy_w3[l], hy_b3[l], hy_freq[l])
        lane_pad = ((0, 0), (0, LANES - HY_FH))
        hid = jnp.pad(_hyena_hidden(SEQ, *hp), lane_pad)
        hid_rev = jnp.pad(_hyena_hidden(SEQ, *hp, reverse=True), lane_pad)
        kr, ki = _spectrum(hid, hid_rev, w4p, deltas, l, g1s, g2)
        kctx = _hyena_filters_ctx(_hyena_hidden(CTX_LEN, *hp),
                                  _hyena_hidden(CTX_LEN, *hp, reverse=True), hy_w4[l])
        common = (hy_conv_w, hy_conv_b, hy_bias, l)
        tabs = (g1d, g1i, g2, g2i)
        crb = R_LAT // CTX_LEN
        z = _hyconv(vu, hy_cb, vu, hy_cb + ch, *common, 0, 0, ch, kr, ki, tabs, True)
        zc = _hyconv_ctx(vu, crb, hy_cb, vu, crb, hy_cb + ch, *common, 0, 0, ch, kctx, ctx_tabs, True)
        yb = (_hyconv(z, 0, vu, hy_cb + 2 * ch, *common, 1, 0, 2 * ch, kr, ki, tabs, False),
              _hyconv_ctx(zc, 0, 0, vu, crb, hy_cb + 2 * ch, *common, 1, 0, 2 * ch, kctx, ctx_tabs, False))

        m = _merge(ya, yb, gates, w_br_a, w_br_b, l)
        xs = _resid_proj(m, w_out, (l,), xs, mods, l, 2, name="out_proj")

        i = l // 2
        if l % 2 == 0:
            h2 = _normmod(xs, norm2, mods, l, 3)
            g = _swiglu_up(h2, ffn_w1, ffn_w3, (i,))
            xs = _resid_proj(g, ffn_w2, (i,), xs, mods, l, 5, tm=FFD_TM, name="ffn_down")
        else:
            h32, logits = _normmod(xs, norm2, mods, l, 3, router=router_pad[i])
            xs = _moe(h32, logits, xs, mods, l, i, moe_w1, moe_w3, moe_w2)
    return xs[:R_LAT].reshape(BATCH, SEQ, D_MODEL)
```

```python
import functools
import math

import numpy as np
import jax
import jax.numpy as jnp
from jax import lax
from jax.experimental import pallas as pl
from jax.experimental.pallas import tpu as pltpu

D_MODEL = 2048
BATCH = 4
SEQ = 4096
DEPTH = 4
CTX_LEN = 256
GRID_W = 64
HEAD_DIM = 128
N_DIFF_HEADS = D_MODEL // (2 * HEAD_DIM)
V_HEAD_DIM = 2 * HEAD_DIM
ATT_W = N_DIFF_HEADS * V_HEAD_DIM
HY_W = D_MODEL
OFF_K = ATT_W
OFF_V = 2 * ATT_W
OFF_HY = 3 * ATT_W
OFF_G = 3 * ATT_W + 3 * HY_W
IN_COLS = OFF_G + 2 * D_MODEL
ROPE_BASE = 10000.0
ROT_FREQS = HEAD_DIM // 4
HY_EMB = 33
HY_BANDS = (HY_EMB - 1) // 2
HY_FH = 64
HY_N_FILT = 2
HY_TARGET = 1e-2
HY_FAST_PCT = 0.3
HY_SLOW_PCT = 1.5
D_FF = 5632
N_EXPERTS = 8
TOP_K = 2
EPS = 1e-6

R_LAT = BATCH * SEQ
R_CTX = BATCH * CTX_LEN
R_ALL = R_LAT + R_CTX
TM = 1024
CTX_TILE = R_LAT // TM
CTX_SEG = BATCH
TR = 512
LANES = 128
VMEM_LIMIT = 56 * 1024 * 1024

FFT_N = 2 * SEQ
FFT_N1 = 128
FFT_N2 = 64
HY_TC = 128

F32 = jnp.float32
BF16 = jnp.bfloat16
HIGHEST = lax.Precision.HIGHEST


def _cparams(n_axes):
    return pltpu.CompilerParams(
        dimension_semantics=("arbitrary",) * n_axes, vmem_limit_bytes=VMEM_LIMIT)


def _row_seg(i, tm=TM):
    return jnp.where(i >= R_LAT // tm, CTX_SEG, i // (SEQ // tm))


def _adaln_body(c_ref, w_ref, b_ref, o_ref):
    s = c_ref[...]
    s = s * jax.nn.sigmoid(s)
    o_ref[...] = jnp.dot(s, w_ref[...], preferred_element_type=F32, precision=HIGHEST) + b_ref[...]


def _adaln(cs, w_ada, b_ada):
    tn = 1536
    n = 6 * D_MODEL
    return pl.pallas_call(
        _adaln_body,
        grid=(DEPTH, n // tn),
        in_specs=[
            pl.BlockSpec((8, D_MODEL), lambda l, j: (0, 0)),
            pl.BlockSpec((None, D_MODEL, tn), lambda l, j: (l, 0, j)),
            pl.BlockSpec((None, 1, tn), lambda l, j: (l, 0, j)),
        ],
        out_specs=pl.BlockSpec((None, 8, tn), lambda l, j: (l, 0, j)),
        out_shape=jax.ShapeDtypeStruct((DEPTH, 8, n), F32),
        compiler_params=_cparams(2),
        name="adaln",
    )(cs, w_ada, b_ada.reshape(DEPTH, 1, n))


def _normmod_body(x_ref, g_ref, sh_ref, sc_ref, *rest, with_router):
    i = pl.program_id(0)
    seg = jnp.where(i >= R_LAT // TR, CTX_SEG, i // (SEQ // TR))
    x = x_ref[...]
    ms = jnp.mean(x * x, axis=-1, keepdims=True)
    y = x * lax.rsqrt(ms + EPS) * g_ref[...]
    h = y * (1.0 + sc_ref[pl.ds(seg, 1), :]) + sh_ref[pl.ds(seg, 1), :]
    if with_router:
        r_ref, o_ref, lg_ref = rest
        lg_ref[...] = jnp.dot(h, r_ref[...], preferred_element_type=F32, precision=HIGHEST)
    else:
        (o_ref,) = rest
    o_ref[...] = h.astype(o_ref.dtype)


def _normmod(x, gain, mods, l, which, router=None):
    in_specs = [
        pl.BlockSpec((TR, D_MODEL), lambda i: (i, 0)),
        pl.BlockSpec((None, 1, D_MODEL), lambda i: (l, 0, 0)),
        pl.BlockSpec((None, 8, D_MODEL), lambda i: (l, 0, which)),
        pl.BlockSpec((None, 8, D_MODEL), lambda i: (l, 0, which + 1)),
    ]
    args = [x, gain.reshape(DEPTH, 1, D_MODEL), mods, mods]
    out_specs = pl.BlockSpec((TR, D_MODEL), lambda i: (i, 0))
    out_shape = jax.ShapeDtypeStruct((R_ALL, D_MODEL), BF16 if router is None else F32)
    if router is not None:
        in_specs.append(pl.BlockSpec((D_MODEL, LANES), lambda i: (0, 0)))
        args.append(router)
        out_specs = [out_specs, pl.BlockSpec((TR, LANES), lambda i: (i, 0))]
        out_shape = [out_shape, jax.ShapeDtypeStruct((R_ALL, LANES), F32)]
    return pl.pallas_call(
        functools.partial(_normmod_body, with_router=router is not None),
        grid=(R_ALL // TR,),
        in_specs=in_specs,
        out_specs=out_specs,
        out_shape=out_shape,
        compiler_params=_cparams(1),
        name="normmod",
    )(*args)


def _mm_body(*refs, n_pref, n_a, n_w, n_aux, epilogue, a_of_w, split_ctx, w_cast):
    pref = refs[:n_pref]
    refs = refs[n_pref:]
    if split_ctx:
        a_refs, a_ctx_refs = refs[:2 * n_a:2], refs[1:2 * n_a:2]
        refs = refs[n_a:]
    else:
        a_refs = a_ctx_refs = refs[:n_a]
    w_refs = refs[n_a:n_a + n_w]
    aux_refs = refs[n_a + n_w:n_a + n_w + n_aux]
    o_ref = refs[n_a + n_w + n_aux]
    wbf_refs = refs[n_a + n_w + n_aux + 1:n_a + 2 * n_w + n_aux + 1]
    aux_refs = tuple(aux_refs) + tuple(refs[n_a + 2 * n_w + n_aux + 1:])
    j = pl.program_id(0)
    i = pl.program_id(1)
    if n_pref:
        te_ref, nu_ref = pref
        new_weights = (i == 0) | (te_ref[i] != te_ref[jnp.maximum(i - 1, 0)])
        valid = i < nu_ref[0]
    else:
        new_weights = i == 0
        valid = None

    @pl.when(new_weights)
    def _():
        for w_ref, wbf_ref in zip(w_refs, wbf_refs):
            wbf_ref[...] = w_cast(w_ref[...]).astype(BF16)

    def compute(srcs):
        accs = [jnp.dot(srcs[a_of_w[k]][...], wbf_refs[k][...], preferred_element_type=F32)
                for k in range(n_w)]
        epilogue(accs, aux_refs, o_ref, j, i)

    if split_ctx:
        pl.when(i < CTX_TILE)(functools.partial(compute, a_refs))
        pl.when(i >= CTX_TILE)(functools.partial(compute, a_ctx_refs))
    elif valid is None:
        compute(a_refs)
    else:
        pl.when(valid)(functools.partial(compute, a_refs))

        @pl.when(jnp.logical_not(valid))
        def _():
            o_ref[...] = jnp.zeros(o_ref.shape, o_ref.dtype)


def _mm(a_list, w_list, aux_list, epilogue, *, n_out, out_dtype, tn, a_of_w=None,
        n_rows=R_ALL, tm=TM, alias_aux=None, group=None, split_ctx=False, w_col0=0,
        epilogue_scratch=(), w_cast=lambda w: w, name="mm"):
    n_a, n_w, n_aux = len(a_list), len(w_list), len(aux_list)
    n_pref = 0 if group is None else 2
    a_of_w = tuple(a_of_w) if a_of_w is not None else tuple(range(n_w))
    in_specs, args, scratch = [], [], []
    for a in a_list:
        if split_ctx:
            lat, ctx = a
            assert tm == R_CTX and group is None
            in_specs.append(pl.BlockSpec((tm, lat.shape[1]),
                                         lambda j, i: (jnp.minimum(i, CTX_TILE - 1), 0)))
            in_specs.append(pl.BlockSpec((tm, ctx.shape[1]), lambda j, i: (0, 0),
                                         pipeline_mode=pl.Buffered(1)))
            args += [lat, ctx]
        else:
            in_specs.append(pl.BlockSpec((tm, a.shape[1]), lambda j, i, *_: (i, 0)))
            args.append(a)
    for w, prefix in w_list:
        k = w.shape[-2]
        if group is None:
            wmap = functools.partial(lambda j, i, p: p + (0, w_col0 + j), p=tuple(prefix))
        else:
            wmap = functools.partial(lambda j, i, te, nu, p: p[:-1] + (te[i], 0, j), p=tuple(prefix))
        in_specs.append(pl.BlockSpec((None,) * len(prefix) + (k, tn), wmap))
        args.append(w)
        scratch.append(pltpu.VMEM((k, tn), BF16))
    for arr, block, imap in aux_list:
        in_specs.append(pl.BlockSpec(block, functools.partial(lambda j, i, *_, f: f(j, i), f=imap)))
        args.append(arr)
    aliases = {}
    if alias_aux is not None:
        aliases = {n_pref + (2 * n_a if split_ctx else n_a) + n_w + alias_aux: 0}
    grid_spec = pltpu.PrefetchScalarGridSpec(
        num_scalar_prefetch=n_pref,
        grid=(n_out // tn, n_rows // tm),
        in_specs=in_specs,
        out_specs=pl.BlockSpec((tm, tn), lambda j, i, *_: (i, j)),
        scratch_shapes=scratch + list(epilogue_scratch),
    )
    return pl.pallas_call(
        functools.partial(_mm_body, n_pref=n_pref, n_a=n_a, n_w=n_w, n_aux=n_aux,
                          epilogue=epilogue, a_of_w=a_of_w, split_ctx=split_ctx, w_cast=w_cast),
        grid_spec=grid_spec,
        out_shape=jax.ShapeDtypeStruct((n_rows, n_out), out_dtype),
        input_output_aliases=aliases,
        compiler_params=_cparams(2),
        name=name,
    )(*(list(group) if group is not None else []), *args)


IN_TN = 512
IN_TN_WIDE = 1024

QK_ROWS = 512


def _qk_epilogue(accs, aux, o_ref, j, i):
    gains_ref, cos_ref, sin_ref, acc_ref = aux
    acc_ref[...] = accs[0]
    gain = gains_ref[pl.ds(j // (OFF_K // IN_TN), 1), :]
    ones = jnp.ones((HEAD_DIM, HEAD_DIM), BF16)

    def rows(r, carry):
        r0 = pl.multiple_of(r * QK_ROWS, QK_ROWS)
        cos = cos_ref[pl.ds(r0, QK_ROWS), :]
        sin = sin_ref[pl.ds(r0, QK_ROWS), :]
        for h in range(IN_TN // HEAD_DIM):
            xh = acc_ref[pl.ds(r0, QK_ROWS), h * HEAD_DIM:(h + 1) * HEAD_DIM]
            ss = jnp.dot((xh * xh).astype(BF16), ones, preferred_element_type=F32)
            y = xh * lax.rsqrt(ss * (1.0 / HEAD_DIM) + EPS) * gain
            partner = pltpu.roll(y, HEAD_DIM // 2, 1)
            o_ref[pl.ds(r0, QK_ROWS), h * HEAD_DIM:(h + 1) * HEAD_DIM] = (
                y * cos + partner * sin).astype(BF16)
        return carry

    lax.fori_loop(0, TM // QK_ROWS, rows, 0)


def _cast_epilogue(accs, aux, o_ref, j, i):
    o_ref[...] = accs[0].astype(BF16)


def _gate_epilogue(accs, aux, o_ref, j, i):
    o_ref[...] = jax.nn.sigmoid(accs[0] + aux[0][...]).astype(BF16)


def _pair_major(a, axis=-1):
    a = jnp.moveaxis(a, axis, -1)
    lead = a.shape[:-1]
    a = a.reshape(lead + (a.shape[-1] // HEAD_DIM, 2, 2, ROT_FREQS))
    a = jnp.swapaxes(a, -3, -2).reshape(lead + (-1,))
    return jnp.moveaxis(a, -1, axis)


def _pair_major_lanes(w):
    lane = lax.broadcasted_iota(jnp.int32, (1, HEAD_DIM), 1)
    second = (lane >= ROT_FREQS) & (lane < 2 * ROT_FREQS)
    third = (lane >= 2 * ROT_FREQS) & (lane < 3 * ROT_FREQS)
    heads = []
    for h in range(w.shape[1] // HEAD_DIM):
        x = w[:, h * HEAD_DIM:(h + 1) * HEAD_DIM]
        heads.append(jnp.where(second, pltpu.roll(x, HEAD_DIM - ROT_FREQS, 1),
                               jnp.where(third, pltpu.roll(x, ROT_FREQS, 1), x)))
    return jnp.concatenate(heads, axis=1)


def _inproj(h, w_in, l, gains, cos_t, sin_t, b_gate):
    def rope_map(j, i):
        return (jnp.where(i >= CTX_TILE, SEQ // TM, i % (SEQ // TM)), 0)

    w = [(w_in, (l,))]
    qk = _mm([h], w, [(gains, (None, 8, HEAD_DIM), lambda j, i: (l, 0, 0)),
                      (cos_t, (TM, HEAD_DIM), rope_map), (sin_t, (TM, HEAD_DIM), rope_map)],
             _qk_epilogue, n_out=OFF_V, out_dtype=BF16, tn=IN_TN,
             epilogue_scratch=[pltpu.VMEM((TM, IN_TN), F32)], w_cast=_pair_major_lanes,
             name="inproj_qk")
    vu = _mm([h], w, [], _cast_epilogue, n_out=OFF_G - OFF_V, out_dtype=BF16, tn=IN_TN_WIDE,
             w_col0=OFF_V // IN_TN_WIDE, name="inproj_vu")
    gates = _mm([h], w, [(b_gate.reshape(DEPTH, 1, 2 * D_MODEL), (None, 1, IN_TN_WIDE),
                          lambda j, i: (l, 0, j))],
                _gate_epilogue, n_out=2 * D_MODEL, out_dtype=BF16, tn=IN_TN_WIDE,
                w_col0=OFF_G // IN_TN_WIDE, name="inproj_gate")
    return qk, vu, gates


MG_TN = 512


def _merge_epilogue(accs, aux, o_ref, j, i):
    ga_ref, gb_ref = aux
    o_ref[...] = (ga_ref[...].astype(F32) * accs[0] + gb_ref[...].astype(F32) * accs[1]).astype(BF16)


def _merge(ya, yb, gates, w_br_a, w_br_b, l):
    aux = [
        (gates, (TM, MG_TN), lambda j, i: (i, j)),
        (gates, (TM, MG_TN), lambda j, i: (i, D_MODEL // MG_TN + j)),
    ]
    return _mm([ya, yb], [(w_br_a, (l,)), (w_br_b, (l,))], aux, _merge_epilogue,
               n_out=D_MODEL, out_dtype=BF16, tn=MG_TN, split_ctx=True, name="merge")


RS_TN = 512
FFD_TM = 512


def _resid_epilogue(accs, aux, o_ref, j, i, tm):
    x_ref, g_ref = aux
    o_ref[...] = x_ref[...] + g_ref[pl.ds(_row_seg(i, tm), 1), :] * accs[0]


def _resid_proj(a, w, prefix, x, mods, l, which, tm=TM, name="resid"):
    nb = D_MODEL // RS_TN
    aux = [
        (x, (tm, RS_TN), lambda j, i: (i, j)),
        (mods, (None, 8, RS_TN), lambda j, i: (l, 0, which * nb + j)),
    ]
    return _mm([a], [(w, prefix)], aux, functools.partial(_resid_epilogue, tm=tm),
               n_out=D_MODEL, out_dtype=F32, tm=tm, tn=RS_TN, alias_aux=0, name=name)


FF_TN = 512


def _swiglu_epilogue(accs, aux, o_ref, j, i):
    a = accs[0]
    o_ref[...] = (a * jax.nn.sigmoid(a) * accs[1]).astype(BF16)


def _swiglu_up(h, w1, w3, prefix, group=None):
    kw = dict(n_rows=MOE_P, tm=MOE_TM) if group is not None else {}
    return _mm([h], [(w1, prefix), (w3, prefix)], [], _swiglu_epilogue, n_out=D_FF,
               out_dtype=BF16, tn=FF_TN, a_of_w=(0, 0), group=group, name="swiglu_up", **kw)


MOE_TM = 512
MOE_A = TOP_K * R_ALL
MOE_NT = MOE_A // MOE_TM + N_EXPERTS
MOE_P = MOE_NT * MOE_TM
GATHER_ROWS = 256


def _route(logits):
    top_v, top_i = lax.top_k(logits[:, :N_EXPERTS], TOP_K)
    wts = jax.nn.softmax(top_v, axis=-1).reshape(-1)
    e_flat = top_i.reshape(-1).astype(jnp.int32)
    onehot = (e_flat[:, None] == jnp.arange(N_EXPERTS, dtype=jnp.int32)[None, :]).astype(jnp.int32)
    before = jnp.cumsum(onehot, axis=0) - onehot
    rank = jnp.sum(before * onehot, axis=1)
    counts = jnp.sum(onehot, axis=0)
    tiles = (counts + MOE_TM - 1) // MOE_TM
    tile_end = jnp.cumsum(tiles)
    pos = ((tile_end - tiles)[e_flat] * MOE_TM + rank).astype(jnp.int32)
    n_used = tile_end[-1:].astype(jnp.int32)
    tile_ids = jnp.minimum(jnp.arange(MOE_NT, dtype=jnp.int32), n_used[0] - 1)
    tile_expert = jnp.sum((tile_ids[:, None] >= tile_end[None, :]).astype(jnp.int32), axis=1)
    token = jnp.arange(MOE_A, dtype=jnp.int32) // TOP_K
    src_token = jnp.zeros((MOE_P,), jnp.int32).at[pos].set(token)
    return pos, src_token, wts, tile_expert.astype(jnp.int32), n_used


def _gather_body(src_ref, nu_ref, h_ref, o_ref, buf_ref, sem):
    t = pl.program_id(0)
    n_rows = nu_ref[0] * MOE_TM
    slot = t % 2

    def row_copy(slot_, r, src_row):
        return pltpu.make_async_copy(h_ref.at[pl.ds(src_row, 1), :],
                                     buf_ref.at[slot_, pl.ds(r, 1), :], sem.at[slot_])

    def issue(step, slot_):
        def body(r, carry):
            row_copy(slot_, r, src_ref[step * GATHER_ROWS + r]).start()
            return carry

        lax.fori_loop(0, GATHER_ROWS, body, 0, unroll=8)

    @pl.when((t == 0) & (n_rows > 0))
    def _():
        issue(0, 0)

    @pl.when((t + 1 < pl.num_programs(0)) & ((t + 1) * GATHER_ROWS < n_rows))
    def _():
        issue(t + 1, 1 - slot)

    @pl.when(t * GATHER_ROWS < n_rows)
    def _():
        def drain(r, carry):
            row_copy(slot, r, 0).wait()
            return carry

        lax.fori_loop(0, GATHER_ROWS, drain, 0, unroll=8)
        o_ref[...] = buf_ref[slot].astype(BF16)

    @pl.when(t * GATHER_ROWS >= n_rows)
    def _():
        o_ref[...] = jnp.zeros(o_ref.shape, o_ref.dtype)


def _moe_gather(h32, src_token, n_used):
    grid_spec = pltpu.PrefetchScalarGridSpec(
        num_scalar_prefetch=2,
        grid=(MOE_P // GATHER_ROWS,),
        in_specs=[pl.BlockSpec(memory_space=pl.ANY)],
        out_specs=pl.BlockSpec((GATHER_ROWS, D_MODEL), lambda t, *_: (t, 0)),
        scratch_shapes=[pltpu.VMEM((2, GATHER_ROWS, D_MODEL), F32), pltpu.SemaphoreType.DMA((2,))],
    )
    return pl.pallas_call(
        _gather_body,
        grid_spec=grid_spec,
        out_shape=jax.ShapeDtypeStruct((MOE_P, D_MODEL), BF16),
        compiler_params=_cparams(1),
        name="moe_gather",
    )(src_token, n_used, h32)


def _moe_down_epilogue(accs, aux, o_ref, j, i):
    o_ref[...] = accs[0]


def _combine_body(pos_ref, y_ref, w_ref, x_ref, g_ref, o_ref, buf_ref, sem):
    t = pl.program_id(0)
    slot = t % 2

    def row_copy(slot_, k, r, src_row):
        return pltpu.make_async_copy(y_ref.at[pl.ds(src_row, 1), :],
                                     buf_ref.at[slot_, k, pl.ds(r, 1), :], sem.at[slot_])

    def issue(step, slot_):
        def body(r, carry):
            for k in range(TOP_K):
                row_copy(slot_, k, r, pos_ref[TOP_K * (step * TR + r) + k]).start()
            return carry

        lax.fori_loop(0, TR, body, 0, unroll=4)

    @pl.when(t == 0)
    def _():
        issue(0, 0)

    @pl.when(t + 1 < pl.num_programs(0))
    def _():
        issue(t + 1, 1 - slot)

    def drain(r, carry):
        for k in range(TOP_K):
            row_copy(slot, k, r, 0).wait()
        return carry

    lax.fori_loop(0, TR, drain, 0, unroll=4)
    g = g_ref[pl.ds(_row_seg(t, TR), 1), :]
    y = w_ref[0] * buf_ref[slot, 0]
    for k in range(1, TOP_K):
        y = y + w_ref[k] * buf_ref[slot, k]
    o_ref[...] = x_ref[...] + g * y


def _moe_combine(ys, pos, wts, x, mods, l, which):
    w_cols = jnp.broadcast_to(wts.reshape(R_ALL, TOP_K).T[:, :, None], (TOP_K, R_ALL, 1))
    grid_spec = pltpu.PrefetchScalarGridSpec(
        num_scalar_prefetch=1,
        grid=(R_ALL // TR,),
        in_specs=[pl.BlockSpec(memory_space=pl.ANY),
                  pl.BlockSpec((TOP_K, TR, 1), lambda t, *_: (0, t, 0)),
                  pl.BlockSpec((TR, D_MODEL), lambda t, *_: (t, 0)),
                  pl.BlockSpec((None, 8, D_MODEL), lambda t, *_: (l, 0, which))],
        out_specs=pl.BlockSpec((TR, D_MODEL), lambda t, *_: (t, 0)),
        scratch_shapes=[pltpu.VMEM((2, TOP_K, TR, D_MODEL), F32), pltpu.SemaphoreType.DMA((2,))],
    )
    return pl.pallas_call(
        _combine_body,
        grid_spec=grid_spec,
        out_shape=jax.ShapeDtypeStruct((R_ALL, D_MODEL), F32),
        input_output_aliases={3: 0},
        compiler_params=_cparams(1),
        name="moe_combine",
    )(pos, ys, w_cols, x, mods)


def _moe(h32, logits, xs, mods, l, i, moe_w1, moe_w3, moe_w2):
    pos, src_token, wts, tile_expert, n_used = _route(logits)
    group = (tile_expert, n_used)
    hs = _moe_gather(h32, src_token, n_used)
    g = _swiglu_up(hs, moe_w1, moe_w3, (i, 0), group=group)
    ys = _mm([g], [(moe_w2, (i, 0))], [], _moe_down_epilogue, n_out=D_MODEL, out_dtype=F32,
             tn=RS_TN, n_rows=MOE_P, tm=MOE_TM, group=group, name="moe_down")
    return _moe_combine(ys, pos, wts, xs, mods, l, 5)


AT_TQ = 512
AT_TK = 1024


ATT_SAFE_BOUND = 60.0


def _attn_scores(qs, k_ref, sl, c0, c1):
    return lax.dot_general(qs, k_ref[c0:c1, sl], (((1,), (1,)), ((), ())),
                           preferred_element_type=F32)


def _attn_body(lam_ref, q_ref, *rest, n_seg, post_scale):
    k_refs = rest[:n_seg]
    v_refs = rest[n_seg:2 * n_seg]
    sub_ref = rest[2 * n_seg]
    o_ref = rest[2 * n_seg + 1]
    kmax_ref = rest[2 * n_seg + 2]
    num_ref = rest[2 * n_seg + 3]
    lam = lam_ref[0]
    subs = [slice(s * HEAD_DIM, (s + 1) * HEAD_DIM) for s in range(2)]

    @pl.when(pl.program_id(2) == 0)
    def _():
        for s, sl in enumerate(subs):
            best = None
            for k_ref in k_refs:
                kf = k_ref[:, sl].astype(F32)
                n2 = jnp.max(jnp.sum(kf * kf, axis=-1, keepdims=True))
                best = n2 if best is None else jnp.maximum(best, n2)
            kmax_ref[s] = best

    qs = [q_ref[:, sl] for sl in subs]
    bounds = []
    for s in range(2):
        qf = qs[s].astype(F32)
        qn2 = jnp.sum(qf * qf, axis=-1, keepdims=True)
        bounds.append(jnp.sqrt(qn2 * kmax_ref[s]) * 1.001 + 1e-6)
    worst = jnp.maximum(jnp.max(bounds[0]), jnp.max(bounds[1]))

    @pl.when(worst <= ATT_SAFE_BOUND)
    def _():
        for s, sl in enumerate(subs):
            den = num = None
            for k_ref, v_ref in zip(k_refs, v_refs):
                p = jnp.exp2(_attn_scores(qs[s], k_ref, sl, 0, k_ref.shape[0]) - bounds[s])
                ps = p.sum(axis=-1, keepdims=True)
                pv = jnp.dot(p.astype(BF16), v_ref[...], preferred_element_type=F32)
                den = ps if den is None else den + ps
                num = pv if num is None else num + pv
            num_ref[s] = num / den

    @pl.when(worst > ATT_SAFE_BOUND)
    def _():
        for s, sl in enumerate(subs):
            m = den = num = None
            for k_ref, v_ref in zip(k_refs, v_refs):
                seg_len = k_ref.shape[0]
                for c0 in range(0, seg_len, AT_TK):
                    c1 = min(c0 + AT_TK, seg_len)
                    sc = _attn_scores(qs[s], k_ref, sl, c0, c1)
                    mc = sc.max(axis=-1, keepdims=True)
                    m_new = mc if m is None else jnp.maximum(m, mc)
                    p = jnp.exp2(sc - m_new)
                    ps = p.sum(axis=-1, keepdims=True)
                    pv = jnp.dot(p.astype(BF16), v_ref[c0:c1, :], preferred_element_type=F32)
                    if m is None:
                        den, num = ps, pv
                    else:
                        alpha = jnp.exp2(m - m_new)
                        den = alpha * den + ps
                        num = alpha * num + pv
                    m = m_new
            num_ref[s] = num / den

    d = num_ref[0] - lam * num_ref[1]
    ms = jnp.mean(d * d, axis=-1, keepdims=True)
    o_ref[...] = (d * lax.rsqrt(ms + EPS) * sub_ref[...] * post_scale).astype(BF16)


def _attention(qk, vu, lam, sub_norm, l, lam_init, ctx_only=False):
    kb = OFF_K // V_HEAD_DIM
    ctx_blk0 = R_LAT // CTX_LEN
    if ctx_only:
        tq = CTX_LEN
        grid = (BATCH, N_DIFF_HEADS, 1)
        q_map = lambda b, h, t: (ctx_blk0 + b, h)
        o_map = lambda b, h, t: (b, h)
        segs = [CTX_LEN]
    else:
        tq = AT_TQ
        grid = (BATCH, N_DIFF_HEADS, SEQ // tq)
        q_map = o_map = lambda b, h, t: (b * (SEQ // tq) + t, h)
        segs = [CTX_LEN, SEQ]
    in_specs = [pl.BlockSpec(memory_space=pltpu.SMEM),
                pl.BlockSpec((tq, V_HEAD_DIM), q_map)]
    args = [lam.reshape(1), qk]
    for arr, off in ((qk, kb), (vu, 0)):
        for seg_len in segs:
            if seg_len == CTX_LEN:
                imap = functools.partial(lambda b, h, t, o: (ctx_blk0 + b, o + h), o=off)
            else:
                imap = functools.partial(lambda b, h, t, o: (b, o + h), o=off)
            in_specs.append(pl.BlockSpec((seg_len, V_HEAD_DIM), imap))
            args.append(arr)
    in_specs.append(pl.BlockSpec((None, 1, V_HEAD_DIM), lambda b, h, t: (l, 0, 0)))
    args.append(sub_norm.reshape(DEPTH, 1, V_HEAD_DIM))
    return pl.pallas_call(
        functools.partial(_attn_body, n_seg=len(segs), post_scale=1.0 - lam_init),
        grid=grid,
        in_specs=in_specs,
        out_specs=pl.BlockSpec((tq, V_HEAD_DIM), o_map),
        out_shape=jax.ShapeDtypeStruct((R_CTX if ctx_only else R_LAT, ATT_W), BF16),
        scratch_shapes=[pltpu.SMEM((2,), F32), pltpu.VMEM((2, tq, V_HEAD_DIM), F32)],
        compiler_params=_cparams(3),
        name="attn_ctx" if ctx_only else "attn",
    )(*args)


def _dft_tables():
    n, n1, n2 = FFT_N, FFT_N1, FFT_N2
    k1 = np.arange(n1)[:, None]
    tabs_d, tabs_s, tabs_i = [], [], []
    for b in range(n2):
        cols = n2 * np.arange(n1)[None, :] + b
        g = np.exp(-2j * np.pi * ((k1 * cols) % n) / n)
        gd = g[:, :n1 // 2]
        tabs_d.append(np.block([[gd.real, -gd.imag], [gd.imag, gd.real]]))
        tabs_s.append(np.concatenate([g.real, g.imag], axis=0))
        gi = np.conj(gd).T / n1
        tabs_i.append(np.block([[gi.real, -gi.imag], [gi.imag, gi.real]]))
    kk = np.arange(n2)
    f2 = np.exp(-2j * np.pi * ((kk[:, None] * kk[None, :]) % n2) / n2)
    g2 = np.block([[f2.real, -f2.imag], [f2.imag, f2.real]])
    f2i = np.conj(f2) / n2
    g2i = np.block([[f2i.real, -f2i.imag], [f2i.imag, f2i.real]])
    as_bf = lambda a: jnp.asarray(np.asarray(a, np.float32)).astype(BF16)
    return (as_bf(np.stack(tabs_d)), as_bf(np.stack(tabs_s)), as_bf(np.stack(tabs_i)),
            as_bf(g2), as_bf(g2i))


FFT_UNROLL = 32


def _pack_pair(a, b):
    ab = lax.bitcast_convert_type(a.astype(BF16).astype(F32), jnp.uint32)
    bb = lax.bitcast_convert_type(b.astype(BF16).astype(F32), jnp.uint32)
    return lax.bitcast_convert_type(ab | (bb >> 16), F32)


def _unpack_pair(w):
    bits = lax.bitcast_convert_type(w, jnp.uint32)
    hi = lax.bitcast_convert_type(bits & jnp.uint32(0xFFFF0000), F32)
    lo = lax.bitcast_convert_type(bits << 16, F32)
    return hi, lo


def _rows_of_group(ref3, g):
    return jnp.swapaxes(ref3[:, pl.ds(pl.multiple_of(g * 8, 8), 8), :], 0, 1)


def _spectrum_body(hf_ref, hr_ref, wf_ref, wb_ref, dl_ref, g1_ref, g2_ref, or_ref, oi_ref,
                   k_ref, a_ref):
    n1, n2 = FFT_N1, FFT_N2
    delta = dl_ref[...]
    step = 1.0 / (SEQ - 1)
    rows = 512
    row_iota = lax.broadcasted_iota(jnp.int32, (rows, 1), 0)

    def taps(c, carry):
        r0 = pl.multiple_of(c * rows, rows)
        pos = (row_iota + r0).astype(F32)
        top = jnp.dot(hf_ref[pl.ds(r0, rows), :].astype(BF16), wf_ref[...].astype(BF16),
                      preferred_element_type=F32)
        bot = jnp.dot(hr_ref[pl.ds(r0, rows), :].astype(BF16), wb_ref[...].astype(BF16),
                      preferred_element_type=F32)
        blk = pl.multiple_of(c * (rows // n2), rows // n2)
        k_ref[pl.ds(blk, rows // n2)] = (top * jnp.exp(-(pos * step) * delta)).reshape(
            rows // n2, n2, HY_TC)
        k_ref[pl.ds(n1 // 2 + blk, rows // n2)] = (
            bot * jnp.exp(-((SEQ - pos) * step) * delta)).reshape(rows // n2, n2, HY_TC)
        return carry

    lax.fori_loop(0, SEQ // rows, taps, 0)

    def stage1(g, carry):
        xs = _rows_of_group(k_ref, g)
        for j in range(8):
            a = jnp.dot(g1_ref[g * 8 + j], xs[j].astype(BF16), preferred_element_type=F32)
            a_ref[g * 8 + j] = _pack_pair(a[:n1], a[n1:])
        return carry

    lax.fori_loop(0, n2 // 8, stage1, 0, unroll=FFT_UNROLL // 8)

    def stage2(g, carry):
        xs = _rows_of_group(a_ref, g)
        for j in range(8):
            a = jnp.concatenate(_unpack_pair(xs[j]), axis=0).astype(BF16)
            x = jnp.dot(g2_ref[...], a, preferred_element_type=F32)
            row = pl.multiple_of((g * 8 + j) * n2, n2)
            or_ref[pl.ds(row, n2), :] = x[:n2].astype(BF16)
            oi_ref[pl.ds(row, n2), :] = x[n2:].astype(BF16)
        return carry

    lax.fori_loop(0, n1 // 8, stage2, 0, unroll=FFT_UNROLL // 8)


def _const_spec(shape, n_grid):
    nd = len(shape)
    imap = (lambda a, b: (0,) * nd) if n_grid == 2 else (lambda a: (0,) * nd)
    return pl.BlockSpec(shape, imap, pipeline_mode=pl.Buffered(1))


def _spectrum(hid_f, hid_r, w4p, deltas, l, g1s, g2):
    tc = HY_TC
    nct = HY_W // tc
    hspec = pl.BlockSpec((SEQ, LANES), lambda f, ct: (0, 0))
    wspec = lambda d: pl.BlockSpec(
        (None, LANES, tc), functools.partial(lambda f, ct, d: (l, 0, (2 * f + d) * nct + ct), d=d))
    ospec = pl.BlockSpec((None, FFT_N, tc), lambda f, ct: (f, 0, ct))
    return pl.pallas_call(
        _spectrum_body,
        grid=(HY_N_FILT, nct),
        in_specs=[hspec, hspec, wspec(0), wspec(1),
                  pl.BlockSpec((1, tc), lambda f, ct: (0, ct)),
                  _const_spec((FFT_N2, 2 * FFT_N1, FFT_N1), 2),
                  _const_spec((2 * FFT_N2, 2 * FFT_N2), 2)],
        out_specs=[ospec, ospec],
        out_shape=[jax.ShapeDtypeStruct((HY_N_FILT, FFT_N, HY_W), BF16)] * 2,
        scratch_shapes=[pltpu.VMEM((FFT_N1, FFT_N2, tc), F32), pltpu.VMEM((FFT_N2, FFT_N1, tc), F32)],
        compiler_params=_cparams(2),
        name="hy_spectrum",
    )(hid_f, hid_r, w4p, w4p, deltas, g1s, g2)


def _conv3(u, w_ref, b_ref):
    n = u.shape[0]
    row = lax.broadcasted_iota(jnp.int32, u.shape, 0)
    prev = jnp.where(row == 0, 0.0, pltpu.roll(u, 1, 0))
    nxt = jnp.where(row == n - 1, 0.0, pltpu.roll(u, n - 1, 0))
    return prev * w_ref[0:1, :] + u * w_ref[1:2, :] + nxt * w_ref[2:3, :] + b_ref[...]


def _hyconv_body(s0_ref, s1_ref, m0_ref, m1_ref, cws_ref, cbs_ref, cwm_ref, cbm_ref, bias_ref,
                 kr_ref, ki_ref, g1_ref, g1i_ref, g2_ref, g2i_ref, o_ref,
                 vr_ref, vi_ref, vp_ref, a_ref, b_ref, *, conv_signal):
    n1, n2 = FFT_N1, FFT_N2
    h1 = n1 // 2
    for s_ref, v_ref in ((s0_ref, vr_ref), (s1_ref, vi_ref)):
        u = s_ref[...].astype(F32)
        v_ref[...] = _conv3(u, cws_ref, cbs_ref) if conv_signal else u
    vp_ref[...] = _pack_pair(vr_ref[...], vi_ref[...]).reshape(h1, n2, HY_TC)

    def stage1(g, carry):
        xs = _rows_of_group(vp_ref, g)
        for j in range(8):
            x = jnp.concatenate(_unpack_pair(xs[j]), axis=0).astype(BF16)
            a = jnp.dot(g1_ref[g * 8 + j], x, preferred_element_type=F32)
            a_ref[g * 8 + j] = _pack_pair(a[:n1], a[n1:])
        return carry

    lax.fori_loop(0, n2 // 8, stage1, 0, unroll=FFT_UNROLL // 8)

    def stage2(g, carry):
        xs = _rows_of_group(a_ref, g)
        for j in range(8):
            k1 = g * 8 + j
            a = jnp.concatenate(_unpack_pair(xs[j]), axis=0).astype(BF16)
            x = jnp.dot(g2_ref[...], a, preferred_element_type=F32)
            row = pl.multiple_of(k1 * n2, n2)
            fr = kr_ref[pl.ds(row, n2), :].astype(F32)
            fi = ki_ref[pl.ds(row, n2), :].astype(F32)
            xr, xi = x[:n2], x[n2:]
            y = jnp.concatenate([xr * fr - xi * fi, xr * fi + xi * fr], axis=0).astype(BF16)
            bq = jnp.dot(g2i_ref[...], y, preferred_element_type=F32)
            b_ref[k1] = _pack_pair(bq[:n2], bq[n2:])
        return carry

    lax.fori_loop(0, n1 // 8, stage2, 0, unroll=FFT_UNROLL // 8)

    def stage1_inv(g, carry):
        xs = _rows_of_group(b_ref, g)
        for j in range(8):
            bq = jnp.concatenate(_unpack_pair(xs[j]), axis=0).astype(BF16)
            y = jnp.dot(g1i_ref[g * 8 + j], bq, preferred_element_type=F32)
            a_ref[g * 8 + j, pl.ds(0, h1), :] = _pack_pair(y[:h1], y[h1:])
        return carry

    lax.fori_loop(0, n2 // 8, stage1_inv, 0, unroll=FFT_UNROLL // 8)

    bias = bias_ref[...]

    def unpermute(g, carry):
        xs = _rows_of_group(a_ref, g)
        for j in range(8):
            row = pl.multiple_of((g * 8 + j) * n2, n2)
            yr, yi = _unpack_pair(xs[j])
            vr_ref[pl.ds(row, n2), :] = yr + bias * vr_ref[pl.ds(row, n2), :]
            vi_ref[pl.ds(row, n2), :] = yi + bias * vi_ref[pl.ds(row, n2), :]
        return carry

    lax.fori_loop(0, h1 // 8, unpermute, 0, unroll=FFT_UNROLL // 8)

    for half, (m_ref, v_ref) in enumerate(((m0_ref, vr_ref), (m1_ref, vi_ref))):
        mult = _conv3(m_ref[...].astype(F32), cwm_ref, cbm_ref)
        o_ref[half * SEQ:(half + 1) * SEQ, :] = (mult * v_ref[...]).astype(o_ref.dtype)


def _hy_param_specs(conv_w, conv_b, bias, l, filt, sig_ch, mul_ch, tc):
    cw3 = conv_w.reshape(DEPTH, 3, 3 * HY_W)
    cb3 = conv_b.reshape(DEPTH, 1, 3 * HY_W)
    chan = lambda ch0, rows: pl.BlockSpec(
        (None, rows, tc), functools.partial(lambda ct, pr, c0: (l, 0, c0 + ct), c0=ch0))
    specs = [chan(sig_ch, 3), chan(sig_ch, 1), chan(mul_ch, 3), chan(mul_ch, 1),
             pl.BlockSpec((None, None, 1, tc), lambda ct, pr: (l, filt, 0, ct))]
    args = [cw3, cb3, cw3, cb3, bias.reshape(DEPTH, HY_N_FILT, 1, HY_W)]
    return specs, args


def _hyconv(sig, sig_cb, mul, mul_cb, conv_w, conv_b, bias, l, filt, sig_ch, mul_ch,
            kr, ki, tabs, conv_signal):
    g1d, g1i, g2, g2i = tabs
    tc = HY_TC
    blk = lambda arr_cb, odd: pl.BlockSpec(
        (SEQ, tc), functools.partial(lambda ct, pr, cb, o: (2 * pr + o, cb + ct), cb=arr_cb, o=odd))
    pspecs, pargs = _hy_param_specs(conv_w, conv_b, bias, l, filt, sig_ch, mul_ch, tc)
    kspec = pl.BlockSpec((None, FFT_N, tc), lambda ct, pr: (filt, 0, ct))
    in_specs = [blk(sig_cb, 0), blk(sig_cb, 1), blk(mul_cb, 0), blk(mul_cb, 1)] + pspecs + [
        kspec, kspec,
        _const_spec((FFT_N2, 2 * FFT_N1, FFT_N1), 2),
        _const_spec((FFT_N2, FFT_N1, 2 * FFT_N1), 2),
        _const_spec((2 * FFT_N2, 2 * FFT_N2), 2),
        _const_spec((2 * FFT_N2, 2 * FFT_N2), 2),
    ]
    return pl.pallas_call(
        functools.partial(_hyconv_body, conv_signal=conv_signal),
        grid=(HY_W // tc, BATCH // 2),
        in_specs=in_specs,
        out_specs=pl.BlockSpec((2 * SEQ, tc), lambda ct, pr: (pr, ct)),
        out_shape=jax.ShapeDtypeStruct((R_LAT, HY_W), BF16),
        scratch_shapes=[pltpu.VMEM((SEQ, tc), F32)] * 2 + [pltpu.VMEM((FFT_N1 // 2, FFT_N2, tc), F32),
                        pltpu.VMEM((FFT_N2, FFT_N1, tc), F32), pltpu.VMEM((FFT_N1, FFT_N2, tc), F32)],
        compiler_params=_cparams(2),
        name="hyconv",
    )(sig, sig, mul, mul, *pargs, kr, ki, g1d, g1i, g2, g2i)


def _ctx_dft_tables():
    n = 2 * CTX_LEN
    kk = np.arange(n)
    f = np.exp(-2j * np.pi * ((kk[:, None] * kk[None, :]) % n) / n)
    fd = f[:, :CTX_LEN]
    fwd = np.block([[fd.real, -fd.imag], [fd.imag, fd.real]])
    spec = np.concatenate([f.real, f.imag], axis=0)
    fi = np.conj(f)[:CTX_LEN, :] / n
    inv = np.block([[fi.real, -fi.imag], [fi.imag, fi.real]])
    as_bf = lambda a: jnp.asarray(np.asarray(a, np.float32)).astype(BF16)
    return as_bf(fwd), as_bf(spec), as_bf(inv)


def _hyconv_ctx_body(s0_ref, s1_ref, m0_ref, m1_ref, cws_ref, cbs_ref, cwm_ref, cbm_ref, bias_ref,
                     k_ref, fwd_ref, spec_ref, inv_ref, o_ref, *, conv_signal):
    n = 2 * CTX_LEN
    vs = []
    for s_ref in (s0_ref, s1_ref):
        u = s_ref[...].astype(F32)
        vs.append(_conv3(u, cws_ref, cbs_ref) if conv_signal else u)
    x = jnp.dot(fwd_ref[...], jnp.concatenate(vs, axis=0).astype(BF16), preferred_element_type=F32)
    kf = jnp.dot(spec_ref[...], k_ref[...].astype(BF16), preferred_element_type=F32)
    kf = kf.astype(BF16).astype(F32)
    xr, xi, fr, fi = x[:n], x[n:], kf[:n], kf[n:]
    y = jnp.concatenate([xr * fr - xi * fi, xr * fi + xi * fr], axis=0).astype(BF16)
    conv = jnp.dot(inv_ref[...], y, preferred_element_type=F32)
    bias = bias_ref[...]
    for half, m_ref in enumerate((m0_ref, m1_ref)):
        sl = slice(half * CTX_LEN, (half + 1) * CTX_LEN)
        mult = _conv3(m_ref[...].astype(F32), cwm_ref, cbm_ref)
        o_ref[sl, :] = (mult * (conv[sl] + bias * vs[half])).astype(o_ref.dtype)


def _hyconv_ctx(sig, sig_rb, sig_cb, mul, mul_rb, mul_cb, conv_w, conv_b, bias, l, filt,
                sig_ch, mul_ch, kctx, tabs, conv_signal):
    fwd, spec, inv = tabs
    tc = HY_TC
    blk = lambda rb, arr_cb, odd: pl.BlockSpec(
        (CTX_LEN, tc),
        functools.partial(lambda ct, pr, rb, cb, o: (rb + 2 * pr + o, cb + ct), rb=rb, cb=arr_cb, o=odd))
    pspecs, pargs = _hy_param_specs(conv_w, conv_b, bias, l, filt, sig_ch, mul_ch, tc)
    n = 2 * CTX_LEN
    in_specs = [blk(sig_rb, sig_cb, 0), blk(sig_rb, sig_cb, 1),
                blk(mul_rb, mul_cb, 0), blk(mul_rb, mul_cb, 1)] + pspecs + [
        pl.BlockSpec((None, n, tc), lambda ct, pr: (filt, 0, ct)),
        _const_spec((2 * n, n), 2),
        _const_spec((2 * n, n), 2),
        _const_spec((n, 2 * n), 2),
    ]
    return pl.pallas_call(
        functools.partial(_hyconv_ctx_body, conv_signal=conv_signal),
        grid=(HY_W // tc, BATCH // 2),
        in_specs=in_specs,
        out_specs=pl.BlockSpec((2 * CTX_LEN, tc), lambda ct, pr: (pr, ct)),
        out_shape=jax.ShapeDtypeStruct((R_CTX, HY_W), BF16),
        compiler_params=_cparams(2),
        name="hyconv_ctx",
    )(sig, sig, mul, mul, *pargs, kctx, fwd, spec, inv)


def _rope_tables():
    rows = SEQ // GRID_W
    row = jnp.repeat(jnp.arange(rows), GRID_W).astype(F32)
    col = jnp.tile(jnp.arange(GRID_W), rows).astype(F32)
    inv = ROPE_BASE ** (-jnp.arange(ROT_FREQS, dtype=F32) / ROT_FREQS)
    ar, ac = row[:, None] * inv, col[:, None] * inv
    cos = jnp.concatenate([jnp.cos(ar), jnp.cos(ac), jnp.cos(ar), jnp.cos(ac)], axis=1)
    sin = jnp.concatenate([-jnp.sin(ar), -jnp.sin(ac), jnp.sin(ar), jnp.sin(ac)], axis=1)
    cos = jnp.concatenate([cos, jnp.ones((TM, HEAD_DIM), F32)], axis=0)
    sin = jnp.concatenate([sin, jnp.zeros((TM, HEAD_DIM), F32)], axis=0)
    return cos, sin


def _hyena_positions(L, reverse):
    idx = jnp.arange(L, dtype=F32)
    return (L - idx) if reverse else idx


def _hyena_hidden(L, w1, b1, w2, b2, w3, b3, freq, reverse=False):
    pos = _hyena_positions(L, reverse)[:, None]
    t = pos / (L - 1)
    w = 2.0 * math.pi * pos / L
    f = jnp.linspace(1e-4, HY_BANDS - 1, HY_BANDS, dtype=F32)
    z = jnp.concatenate([t, jnp.cos(w * f), -jnp.sin(w * f)], axis=-1)
    dot = functools.partial(jnp.dot, precision=HIGHEST)
    hid = jnp.sin(freq * (dot(z, w1) + b1))
    hid = jnp.sin(freq * (dot(hid, w2) + b2))
    hid = jnp.sin(freq * (dot(hid, w3) + b3))
    if reverse:
        hid = jnp.where(jnp.arange(L)[:, None] > 0, hid, 0.0)
    return hid


def _hyena_deltas():
    return jnp.abs(jnp.linspace(math.log(HY_TARGET) / HY_SLOW_PCT,
                                math.log(HY_TARGET) / HY_FAST_PCT, HY_W, dtype=F32))


def _hyena_filters_ctx(hid, hid_rev, w4):
    L = hid.shape[0]
    w4 = w4.reshape(HY_FH, HY_N_FILT, 2, HY_W)
    halves = []
    for d, feats in enumerate((hid, hid_rev)):
        t = _hyena_positions(L, bool(d))[:, None] / (L - 1)
        taps = jnp.einsum('lh,hfc->flc', feats, w4[:, :, d], precision=HIGHEST)
        halves.append(taps * jnp.exp(-t * _hyena_deltas())[None])
    return jnp.concatenate(halves, axis=1)


def kernel(x, c, ctx, c_ctx, w_ada, b_ada, norm1, norm2, w_in, b_gate, q_norm, k_norm, lam_q1, lam_k1, lam_q2, lam_k2, sub_norm, hy_conv_w, hy_conv_b, hy_w1, hy_b1, hy_w2, hy_b2, hy_w3, hy_b3, hy_w4, hy_freq, hy_bias, w_br_a, w_br_b, w_out, ffn_w1, ffn_w3, ffn_w2, router, moe_w1, moe_w3, moe_w2):
    xs = jnp.concatenate([x.reshape(R_LAT, D_MODEL), ctx.reshape(R_CTX, D_MODEL)], axis=0)
    cs = jnp.concatenate([c, c_ctx[None], jnp.zeros((8 - BATCH - 1, D_MODEL), F32)], axis=0)
    mods = _adaln(cs, w_ada, b_ada)
    cos_t, sin_t = _rope_tables()
    g1d, g1s, g1i, g2, g2i = _dft_tables()
    ctx_tabs = _ctx_dft_tables()
    gains = jnp.zeros((DEPTH, 8, HEAD_DIM), F32)
    gains = gains.at[:, 0].set(_pair_major(q_norm) * (math.log2(math.e) / math.sqrt(HEAD_DIM)))
    gains = gains.at[:, 1].set(_pair_major(k_norm))
    router_pad = jnp.zeros((router.shape[0], D_MODEL, LANES), F32).at[:, :, :N_EXPERTS].set(router)
    hy_cb = ATT_W // HY_TC
    ch = HY_W // HY_TC
    w4p = jnp.pad(hy_w4, ((0, 0), (0, LANES - HY_FH), (0, 0)))
    deltas = _hyena_deltas()[None]

    for l in range(DEPTH):
        lam_init = 0.8 - 0.6 * math.exp(-0.3 * l)
        lam = (jnp.exp(jnp.sum(lam_q1[l] * lam_k1[l])) - jnp.exp(jnp.sum(lam_q2[l] * lam_k2[l]))
               + lam_init)
        h = _normmod(xs, norm1, mods, l, 0)
        qk, vu, gates = _inproj(h, w_in, l, gains, cos_t, sin_t, b_gate)

        ya = (_attention(qk, vu, lam, sub_norm, l, lam_init),
              _attention(qk, vu, lam, sub_norm, l, lam_init, ctx_only=True))

        hp = (hy_w1[l], hy_b1[l], hy_w2[l], hy_b2[l], hy_w3[l], hy_b3[l], hy_freq[l])
        lane_pad = ((0, 0), (0, LANES - HY_FH))
        hid = jnp.pad(_hyena_hidden(SEQ, *hp), lane_pad)
        hid_rev = jnp.pad(_hyena_hidden(SEQ, *hp, reverse=True), lane_pad)
        kr, ki = _spectrum(hid, hid_rev, w4p, deltas, l, g1s, g2)
        kctx = _hyena_filters_ctx(_hyena_hidden(CTX_LEN, *hp),
                                  _hyena_hidden(CTX_LEN, *hp, reverse=True), hy_w4[l])
        common = (hy_conv_w, hy_conv_b, hy_bias, l)
        tabs = (g1d, g1i, g2, g2i)
        crb = R_LAT // CTX_LEN
        z = _hyconv(vu, hy_cb, vu, hy_cb + ch, *common, 0, 0, ch, kr, ki, tabs, True)
        zc = _hyconv_ctx(vu, crb, hy_cb, vu, crb, hy_cb + ch, *common, 0, 0, ch, kctx, ctx_tabs, True)
        yb = (_hyconv(z, 0, vu, hy_cb + 2 * ch, *common, 1, 0, 2 * ch, kr, ki, tabs, False),
              _hyconv_ctx(zc, 0, 0, vu, crb, hy_cb + 2 * ch, *common, 1, 0, 2 * ch, kctx, ctx_tabs, False))

        m = _merge(ya, yb, gates, w_br_a, w_br_b, l)
        xs = _resid_proj(m, w_out, (l,), xs, mods, l, 2, name="out_proj")

        i = l // 2
        if l % 2 == 0:
            h2 = _normmod(xs, norm2, mods, l, 3)
            g = _swiglu_up(h2, ffn_w1, ffn_w3, (i,))
            xs = _resid_proj(g, ffn_w2, (i,), xs, mods, l, 5, tm=FFD_TM, name="ffn_down")
        else:
            h32, logits = _normmod(xs, norm2, mods, l, 3, router=router_pad[i])
            xs = _moe(h32, logits, xs, mods, l, i, moe_w1, moe_w3, moe_w2)
    return xs[:R_LAT].reshape(BATCH, SEQ, D_MODEL)
```

```python
import functools
import math

import numpy as np
import jax
import jax.numpy as jnp
from jax import lax
from jax.experimental import pallas as pl
from jax.experimental.pallas import tpu as pltpu

D_MODEL = 2048
BATCH = 4
SEQ = 4096
DEPTH = 4
CTX_LEN = 256
GRID_W = 64
HEAD_DIM = 128
N_DIFF_HEADS = D_MODEL // (2 * HEAD_DIM)
V_HEAD_DIM = 2 * HEAD_DIM
ATT_W = N_DIFF_HEADS * V_HEAD_DIM
HY_W = D_MODEL
OFF_K = ATT_W
OFF_V = 2 * ATT_W
OFF_HY = 3 * ATT_W
OFF_G = 3 * ATT_W + 3 * HY_W
IN_COLS = OFF_G + 2 * D_MODEL
ROPE_BASE = 10000.0
ROT_FREQS = HEAD_DIM // 4
HY_EMB = 33
HY_BANDS = (HY_EMB - 1) // 2
HY_FH = 64
HY_N_FILT = 2
HY_TARGET = 1e-2
HY_FAST_PCT = 0.3
HY_SLOW_PCT = 1.5
D_FF = 5632
N_EXPERTS = 8
TOP_K = 2
EPS = 1e-6

R_LAT = BATCH * SEQ
R_CTX = BATCH * CTX_LEN
R_ALL = R_LAT + R_CTX
TM = 1024
CTX_TILE = R_LAT // TM
CTX_SEG = BATCH
TR = 512
LANES = 128
VMEM_LIMIT = 56 * 1024 * 1024

FFT_N = 2 * SEQ
FFT_N1 = 128
FFT_N2 = 64
HY_TC = 128

F32 = jnp.float32
BF16 = jnp.bfloat16
HIGHEST = lax.Precision.HIGHEST


def _cparams(n_axes):
    return pltpu.CompilerParams(
        dimension_semantics=("arbitrary",) * n_axes, vmem_limit_bytes=VMEM_LIMIT)


def _row_seg(i, tm=TM):
    return jnp.where(i >= R_LAT // tm, CTX_SEG, i // (SEQ // tm))


def _adaln_body(c_ref, w_ref, b_ref, o_ref):
    s = c_ref[...]
    s = s * jax.nn.sigmoid(s)
    o_ref[...] = jnp.dot(s, w_ref[...], preferred_element_type=F32, precision=HIGHEST) + b_ref[...]


def _adaln(cs, w_ada, b_ada):
    tn = 1536
    n = 6 * D_MODEL
    return pl.pallas_call(
        _adaln_body,
        grid=(DEPTH, n // tn),
        in_specs=[
            pl.BlockSpec((8, D_MODEL), lambda l, j: (0, 0)),
            pl.BlockSpec((None, D_MODEL, tn), lambda l, j: (l, 0, j)),
            pl.BlockSpec((None, 1, tn), lambda l, j: (l, 0, j)),
        ],
        out_specs=pl.BlockSpec((None, 8, tn), lambda l, j: (l, 0, j)),
        out_shape=jax.ShapeDtypeStruct((DEPTH, 8, n), F32),
        compiler_params=_cparams(2),
        name="adaln",
    )(cs, w_ada, b_ada.reshape(DEPTH, 1, n))


def _normmod_body(x_ref, g_ref, sh_ref, sc_ref, *rest, with_router):
    i = pl.program_id(0)
    seg = jnp.where(i >= R_LAT // TR, CTX_SEG, i // (SEQ // TR))
    x = x_ref[...]
    ms = jnp.mean(x * x, axis=-1, keepdims=True)
    y = x * lax.rsqrt(ms + EPS) * g_ref[...]
    h = y * (1.0 + sc_ref[pl.ds(seg, 1), :]) + sh_ref[pl.ds(seg, 1), :]
    if with_router:
        r_ref, o_ref, lg_ref = rest
        lg_ref[...] = jnp.dot(h, r_ref[...], preferred_element_type=F32, precision=HIGHEST)
    else:
        (o_ref,) = rest
    o_ref[...] = h.astype(o_ref.dtype)


def _normmod(x, gain, mods, l, which, router=None):
    in_specs = [
        pl.BlockSpec((TR, D_MODEL), lambda i: (i, 0)),
        pl.BlockSpec((None, 1, D_MODEL), lambda i: (l, 0, 0)),
        pl.BlockSpec((None, 8, D_MODEL), lambda i: (l, 0, which)),
        pl.BlockSpec((None, 8, D_MODEL), lambda i: (l, 0, which + 1)),
    ]
    args = [x, gain.reshape(DEPTH, 1, D_MODEL), mods, mods]
    out_specs = pl.BlockSpec((TR, D_MODEL), lambda i: (i, 0))
    out_shape = jax.ShapeDtypeStruct((R_ALL, D_MODEL), BF16 if router is None else F32)
    if router is not None:
        in_specs.append(pl.BlockSpec((D_MODEL, LANES), lambda i: (0, 0)))
        args.append(router)
        out_specs = [out_specs, pl.BlockSpec((TR, LANES), lambda i: (i, 0))]
        out_shape = [out_shape, jax.ShapeDtypeStruct((R_ALL, LANES), F32)]
    return pl.pallas_call(
        functools.partial(_normmod_body, with_router=router is not None),
        grid=(R_ALL // TR,),
        in_specs=in_specs,
        out_specs=out_specs,
        out_shape=out_shape,
        compiler_params=_cparams(1),
        name="normmod",
    )(*args)


def _mm_body(*refs, n_pref, n_a, n_w, n_aux, epilogue, a_of_w, split_ctx, w_cast):
    pref = refs[:n_pref]
    refs = refs[n_pref:]
    if split_ctx:
        a_refs, a_ctx_refs = refs[:2 * n_a:2], refs[1:2 * n_a:2]
        refs = refs[n_a:]
    else:
        a_refs = a_ctx_refs = refs[:n_a]
    w_refs = refs[n_a:n_a + n_w]
    aux_refs = refs[n_a + n_w:n_a + n_w + n_aux]
    o_ref = refs[n_a + n_w + n_aux]
    wbf_refs = refs[n_a + n_w + n_aux + 1:n_a + 2 * n_w + n_aux + 1]
    aux_refs = tuple(aux_refs) + tuple(refs[n_a + 2 * n_w + n_aux + 1:])
    j = pl.program_id(0)
    i = pl.program_id(1)
    if n_pref:
        te_ref, nu_ref = pref
        new_weights = (i == 0) | (te_ref[i] != te_ref[jnp.maximum(i - 1, 0)])
        valid = i < nu_ref[0]
    else:
        new_weights = i == 0
        valid = None

    @pl.when(new_weights)
    def _():
        for w_ref, wbf_ref in zip(w_refs, wbf_refs):
            wbf_ref[...] = w_cast(w_ref[...]).astype(BF16)

    def compute(srcs):
        accs = [jnp.dot(srcs[a_of_w[k]][...], wbf_refs[k][...], preferred_element_type=F32)
                for k in range(n_w)]
        epilogue(accs, aux_refs, o_ref, j, i)

    if split_ctx:
        pl.when(i < CTX_TILE)(functools.partial(compute, a_refs))
        pl.when(i >= CTX_TILE)(functools.partial(compute, a_ctx_refs))
    elif valid is None:
        compute(a_refs)
    else:
        pl.when(valid)(functools.partial(compute, a_refs))

        @pl.when(jnp.logical_not(valid))
        def _():
            o_ref[...] = jnp.zeros(o_ref.shape, o_ref.dtype)


def _mm(a_list, w_list, aux_list, epilogue, *, n_out, out_dtype, tn, a_of_w=None,
        n_rows=R_ALL, tm=TM, alias_aux=None, group=None, split_ctx=False, w_col0=0,
        epilogue_scratch=(), w_cast=lambda w: w, name="mm"):
    n_a, n_w, n_aux = len(a_list), len(w_list), len(aux_list)
    n_pref = 0 if group is None else 2
    a_of_w = tuple(a_of_w) if a_of_w is not None else tuple(range(n_w))
    in_specs, args, scratch = [], [], []
    for a in a_list:
        if split_ctx:
            lat, ctx = a
            assert tm == R_CTX and group is None
            in_specs.append(pl.BlockSpec((tm, lat.shape[1]),
                                         lambda j, i: (jnp.minimum(i, CTX_TILE - 1), 0)))
            in_specs.append(pl.BlockSpec((tm, ctx.shape[1]), lambda j, i: (0, 0),
                                         pipeline_mode=pl.Buffered(1)))
            args += [lat, ctx]
        else:
            in_specs.append(pl.BlockSpec((tm, a.shape[1]), lambda j, i, *_: (i, 0)))
            args.append(a)
    for w, prefix in w_list:
        k = w.shape[-2]
        if group is None:
            wmap = functools.partial(lambda j, i, p: p + (0, w_col0 + j), p=tuple(prefix))
        else:
            wmap = functools.partial(lambda j, i, te, nu, p: p[:-1] + (te[i], 0, j), p=tuple(prefix))
        in_specs.append(pl.BlockSpec((None,) * len(prefix) + (k, tn), wmap))
        args.append(w)
        scratch.append(pltpu.VMEM((k, tn), BF16))
    for arr, block, imap in aux_list:
        in_specs.append(pl.BlockSpec(block, functools.partial(lambda j, i, *_, f: f(j, i), f=imap)))
        args.append(arr)
    aliases = {}
    if alias_aux is not None:
        aliases = {n_pref + (2 * n_a if split_ctx else n_a) + n_w + alias_aux: 0}
    grid_spec = pltpu.PrefetchScalarGridSpec(
        num_scalar_prefetch=n_pref,
        grid=(n_out // tn, n_rows // tm),
        in_specs=in_specs,
        out_specs=pl.BlockSpec((tm, tn), lambda j, i, *_: (i, j)),
        scratch_shapes=scratch + list(epilogue_scratch),
    )
    return pl.pallas_call(
        functools.partial(_mm_body, n_pref=n_pref, n_a=n_a, n_w=n_w, n_aux=n_aux,
                          epilogue=epilogue, a_of_w=a_of_w, split_ctx=split_ctx, w_cast=w_cast),
        grid_spec=grid_spec,
        out_shape=jax.ShapeDtypeStruct((n_rows, n_out), out_dtype),
        input_output_aliases=aliases,
        compiler_params=_cparams(2),
        name=name,
    )(*(list(group) if group is not None else []), *args)


IN_TN = 512
IN_TN_WIDE = 1024

QK_ROWS = 512


def _qk_epilogue(accs, aux, o_ref, j, i):
    gains_ref, cos_ref, sin_ref, acc_ref = aux
    acc_ref[...] = accs[0]
    gain = gains_ref[pl.ds(j // (OFF_K // IN_TN), 1), :]
    ones = jnp.ones((HEAD_DIM, HEAD_DIM), BF16)

    def rows(r, carry):
        r0 = pl.multiple_of(r * QK_ROWS, QK_ROWS)
        cos = cos_ref[pl.ds(r0, QK_ROWS), :]
        sin = sin_ref[pl.ds(r0, QK_ROWS), :]
        for h in range(IN_TN // HEAD_DIM):
            xh = acc_ref[pl.ds(r0, QK_ROWS), h * HEAD_DIM:(h + 1) * HEAD_DIM]
            ss = jnp.dot((xh * xh).astype(BF16), ones, preferred_element_type=F32)
            y = xh * lax.rsqrt(ss * (1.0 / HEAD_DIM) + EPS) * gain
            partner = pltpu.roll(y, HEAD_DIM // 2, 1)
            o_ref[pl.ds(r0, QK_ROWS), h * HEAD_DIM:(h + 1) * HEAD_DIM] = (
                y * cos + partner * sin).astype(BF16)
        return carry

    lax.fori_loop(0, TM // QK_ROWS, rows, 0)


def _cast_epilogue(accs, aux, o_ref, j, i):
    o_ref[...] = accs[0].astype(BF16)


def _gate_epilogue(accs, aux, o_ref, j, i):
    o_ref[...] = jax.nn.sigmoid(accs[0] + aux[0][...]).astype(BF16)


def _pair_major(a, axis=-1):
    a = jnp.moveaxis(a, axis, -1)
    lead = a.shape[:-1]
    a = a.reshape(lead + (a.shape[-1] // HEAD_DIM, 2, 2, ROT_FREQS))
    a = jnp.swapaxes(a, -3, -2).reshape(lead + (-1,))
    return jnp.moveaxis(a, -1, axis)


def _pair_major_lanes(w):
    lane = lax.broadcasted_iota(jnp.int32, (1, HEAD_DIM), 1)
    second = (lane >= ROT_FREQS) & (lane < 2 * ROT_FREQS)
    third = (lane >= 2 * ROT_FREQS) & (lane < 3 * ROT_FREQS)
    heads = []
    for h in range(w.shape[1] // HEAD_DIM):
        x = w[:, h * HEAD_DIM:(h + 1) * HEAD_DIM]
        heads.append(jnp.where(second, pltpu.roll(x, HEAD_DIM - ROT_FREQS, 1),
                               jnp.where(third, pltpu.roll(x, ROT_FREQS, 1), x)))
    return jnp.concatenate(heads, axis=1)


def _inproj(h, w_in, l, gains, cos_t, sin_t, b_gate):
    def rope_map(j, i):
        return (jnp.where(i >= CTX_TILE, SEQ // TM, i % (SEQ // TM)), 0)

    w = [(w_in, (l,))]
    qk = _mm([h], w, [(gains, (None, 8, HEAD_DIM), lambda j, i: (l, 0, 0)),
                      (cos_t, (TM, HEAD_DIM), rope_map), (sin_t, (TM, HEAD_DIM), rope_map)],
             _qk_epilogue, n_out=OFF_V, out_dtype=BF16, tn=IN_TN,
             epilogue_scratch=[pltpu.VMEM((TM, IN_TN), F32)], w_cast=_pair_major_lanes,
             name="inproj_qk")
    vu = _mm([h], w, [], _cast_epilogue, n_out=OFF_G - OFF_V, out_dtype=BF16, tn=IN_TN_WIDE,
             w_col0=OFF_V // IN_TN_WIDE, name="inproj_vu")
    gates = _mm([h], w, [(b_gate.reshape(DEPTH, 1, 2 * D_MODEL), (None, 1, IN_TN_WIDE),
                          lambda j, i: (l, 0, j))],
                _gate_epilogue, n_out=2 * D_MODEL, out_dtype=BF16, tn=IN_TN_WIDE,
                w_col0=OFF_G // IN_TN_WIDE, name="inproj_gate")
    return qk, vu, gates


MG_TN = 512


def _merge_epilogue(accs, aux, o_ref, j, i):
    ga_ref, gb_ref = aux
    o_ref[...] = (ga_ref[...].astype(F32) * accs[0] + gb_ref[...].astype(F32) * accs[1]).astype(BF16)


def _merge(ya, yb, gates, w_br_a, w_br_b, l):
    aux = [
        (gates, (TM, MG_TN), lambda j, i: (i, j)),
        (gates, (TM, MG_TN), lambda j, i: (i, D_MODEL // MG_TN + j)),
    ]
    return _mm([ya, yb], [(w_br_a, (l,)), (w_br_b, (l,))], aux, _merge_epilogue,
               n_out=D_MODEL, out_dtype=BF16, tn=MG_TN, split_ctx=True, name="merge")


RS_TN = 512
FFD_TM = 512


def _resid_epilogue(accs, aux, o_ref, j, i, tm):
    x_ref, g_ref = aux
    o_ref[...] = x_ref[...] + g_ref[pl.ds(_row_seg(i, tm), 1), :] * accs[0]


def _resid_proj(a, w, prefix, x, mods, l, which, tm=TM, name="resid"):
    nb = D_MODEL // RS_TN
    aux = [
        (x, (tm, RS_TN), lambda j, i: (i, j)),
        (mods, (None, 8, RS_TN), lambda j, i: (l, 0, which * nb + j)),
    ]
    return _mm([a], [(w, prefix)], aux, functools.partial(_resid_epilogue, tm=tm),
               n_out=D_MODEL, out_dtype=F32, tm=tm, tn=RS_TN, alias_aux=0, name=name)


FF_TN = 512


def _swiglu_epilogue(accs, aux, o_ref, j, i):
    a = accs[0]
    o_ref[...] = (a * jax.nn.sigmoid(a) * accs[1]).astype(BF16)


def _swiglu_up(h, w1, w3, prefix, group=None):
    kw = dict(n_rows=MOE_P, tm=MOE_TM) if group is not None else {}
    return _mm([h], [(w1, prefix), (w3, prefix)], [], _swiglu_epilogue, n_out=D_FF,
               out_dtype=BF16, tn=FF_TN, a_of_w=(0, 0), group=group, name="swiglu_up", **kw)


MOE_TM = 512
MOE_A = TOP_K * R_ALL
MOE_NT = MOE_A // MOE_TM + N_EXPERTS
MOE_P = MOE_NT * MOE_TM
GATHER_ROWS = 256


def _route(logits):
    top_v, top_i = lax.top_k(logits[:, :N_EXPERTS], TOP_K)
    wts = jax.nn.softmax(top_v, axis=-1).reshape(-1)
    e_flat = top_i.reshape(-1).astype(jnp.int32)
    onehot = (e_flat[:, None] == jnp.arange(N_EXPERTS, dtype=jnp.int32)[None, :]).astype(jnp.int32)
    before = jnp.cumsum(onehot, axis=0) - onehot
    rank = jnp.sum(before * onehot, axis=1)
    counts = jnp.sum(onehot, axis=0)
    tiles = (counts + MOE_TM - 1) // MOE_TM
    tile_end = jnp.cumsum(tiles)
    pos = ((tile_end - tiles)[e_flat] * MOE_TM + rank).astype(jnp.int32)
    n_used = tile_end[-1:].astype(jnp.int32)
    tile_ids = jnp.minimum(jnp.arange(MOE_NT, dtype=jnp.int32), n_used[0] - 1)
    tile_expert = jnp.sum((tile_ids[:, None] >= tile_end[None, :]).astype(jnp.int32), axis=1)
    token = jnp.arange(MOE_A, dtype=jnp.int32) // TOP_K
    src_token = jnp.zeros((MOE_P,), jnp.int32).at[pos].set(token)
    return pos, src_token, wts, tile_expert.astype(jnp.int32), n_used


def _gather_body(src_ref, nu_ref, h_ref, o_ref, buf_ref, sem):
    t = pl.program_id(0)
    n_rows = nu_ref[0] * MOE_TM
    slot = t % 2

    def row_copy(slot_, r, src_row):
        return pltpu.make_async_copy(h_ref.at[pl.ds(src_row, 1), :],
                                     buf_ref.at[slot_, pl.ds(r, 1), :], sem.at[slot_])

    def issue(step, slot_):
        def body(rp, carry):
            for prio in range(2):
                r = 2 * rp + prio
                row_copy(slot_, r, src_ref[step * GATHER_ROWS + r]).start(priority=prio)
            return carry

        lax.fori_loop(0, GATHER_ROWS // 2, body, 0, unroll=4)

    @pl.when((t == 0) & (n_rows > 0))
    def _():
        issue(0, 0)

    @pl.when((t + 1 < pl.num_programs(0)) & ((t + 1) * GATHER_ROWS < n_rows))
    def _():
        issue(t + 1, 1 - slot)

    @pl.when(t * GATHER_ROWS < n_rows)
    def _():
        def drain(r, carry):
            row_copy(slot, r, 0).wait()
            return carry

        lax.fori_loop(0, GATHER_ROWS, drain, 0, unroll=8)
        o_ref[...] = buf_ref[slot].astype(BF16)

    @pl.when(t * GATHER_ROWS >= n_rows)
    def _():
        o_ref[...] = jnp.zeros(o_ref.shape, o_ref.dtype)


def _moe_gather(h32, src_token, n_used):
    grid_spec = pltpu.PrefetchScalarGridSpec(
        num_scalar_prefetch=2,
        grid=(MOE_P // GATHER_ROWS,),
        in_specs=[pl.BlockSpec(memory_space=pl.ANY)],
        out_specs=pl.BlockSpec((GATHER_ROWS, D_MODEL), lambda t, *_: (t, 0)),
        scratch_shapes=[pltpu.VMEM((2, GATHER_ROWS, D_MODEL), F32), pltpu.SemaphoreType.DMA((2,))],
    )
    return pl.pallas_call(
        _gather_body,
        grid_spec=grid_spec,
        out_shape=jax.ShapeDtypeStruct((MOE_P, D_MODEL), BF16),
        compiler_params=_cparams(1),
        name="moe_gather",
    )(src_token, n_used, h32)


def _moe_down_epilogue(accs, aux, o_ref, j, i):
    o_ref[...] = accs[0]


def _combine_body(pos_ref, y_ref, w_ref, x_ref, g_ref, o_ref, buf_ref, sem):
    t = pl.program_id(0)
    slot = t % 2

    def row_copy(slot_, k, r, src_row):
        return pltpu.make_async_copy(y_ref.at[pl.ds(src_row, 1), :],
                                     buf_ref.at[slot_, k, pl.ds(r, 1), :], sem.at[slot_])

    def issue(step, slot_):
        def body(r, carry):
            for k in range(TOP_K):
                row_copy(slot_, k, r, pos_ref[TOP_K * (step * TR + r) + k]).start(priority=k % 2)
            return carry

        lax.fori_loop(0, TR, body, 0, unroll=4)

    @pl.when(t == 0)
    def _():
        issue(0, 0)

    @pl.when(t + 1 < pl.num_programs(0))
    def _():
        issue(t + 1, 1 - slot)

    def drain(r, carry):
        for k in range(TOP_K):
            row_copy(slot, k, r, 0).wait()
        return carry

    lax.fori_loop(0, TR, drain, 0, unroll=4)
    g = g_ref[pl.ds(_row_seg(t, TR), 1), :]
    y = w_ref[0] * buf_ref[slot, 0]
    for k in range(1, TOP_K):
        y = y + w_ref[k] * buf_ref[slot, k]
    o_ref[...] = x_ref[...] + g * y


def _moe_combine(ys, pos, wts, x, mods, l, which):
    w_cols = jnp.broadcast_to(wts.reshape(R_ALL, TOP_K).T[:, :, None], (TOP_K, R_ALL, 1))
    grid_spec = pltpu.PrefetchScalarGridSpec(
        num_scalar_prefetch=1,
        grid=(R_ALL // TR,),
        in_specs=[pl.BlockSpec(memory_space=pl.ANY),
                  pl.BlockSpec((TOP_K, TR, 1), lambda t, *_: (0, t, 0)),
                  pl.BlockSpec((TR, D_MODEL), lambda t, *_: (t, 0)),
                  pl.BlockSpec((None, 8, D_MODEL), lambda t, *_: (l, 0, which))],
        out_specs=pl.BlockSpec((TR, D_MODEL), lambda t, *_: (t, 0)),
        scratch_shapes=[pltpu.VMEM((2, TOP_K, TR, D_MODEL), F32), pltpu.SemaphoreType.DMA((2,))],
    )
    return pl.pallas_call(
        _combine_body,
        grid_spec=grid_spec,
        out_shape=jax.ShapeDtypeStruct((R_ALL, D_MODEL), F32),
        input_output_aliases={3: 0},
        compiler_params=_cparams(1),
        name="moe_combine",
    )(pos, ys, w_cols, x, mods)


def _moe(h32, logits, xs, mods, l, i, moe_w1, moe_w3, moe_w2):
    pos, src_token, wts, tile_expert, n_used = _route(logits)
    group = (tile_expert, n_used)
    hs = _moe_gather(h32, src_token, n_used)
    g = _swiglu_up(hs, moe_w1, moe_w3, (i, 0), group=group)
    ys = _mm([g], [(moe_w2, (i, 0))], [], _moe_down_epilogue, n_out=D_MODEL, out_dtype=F32,
             tn=RS_TN, n_rows=MOE_P, tm=MOE_TM, group=group, name="moe_down")
    return _moe_combine(ys, pos, wts, xs, mods, l, 5)


AT_TQ = 512
AT_TK = 1024


ATT_SAFE_BOUND = 60.0


def _attn_scores(qs, k_ref, sl, c0, c1):
    return lax.dot_general(qs, k_ref[c0:c1, sl], (((1,), (1,)), ((), ())),
                           preferred_element_type=F32)


def _attn_body(lam_ref, q_ref, *rest, n_seg, post_scale):
    k_refs = rest[:n_seg]
    v_refs = rest[n_seg:2 * n_seg]
    sub_ref = rest[2 * n_seg]
    o_ref = rest[2 * n_seg + 1]
    kmax_ref = rest[2 * n_seg + 2]
    num_ref = rest[2 * n_seg + 3]
    lam = lam_ref[0]
    subs = [slice(s * HEAD_DIM, (s + 1) * HEAD_DIM) for s in range(2)]

    @pl.when(pl.program_id(2) == 0)
    def _():
        for s, sl in enumerate(subs):
            best = None
            for k_ref in k_refs:
                kf = k_ref[:, sl].astype(F32)
                n2 = jnp.max(jnp.sum(kf * kf, axis=-1, keepdims=True))
                best = n2 if best is None else jnp.maximum(best, n2)
            kmax_ref[s] = best

    qs = [q_ref[:, sl] for sl in subs]
    bounds = []
    for s in range(2):
        qf = qs[s].astype(F32)
        qn2 = jnp.sum(qf * qf, axis=-1, keepdims=True)
        bounds.append(jnp.sqrt(qn2 * kmax_ref[s]) * 1.001 + 1e-6)
    worst = jnp.maximum(jnp.max(bounds[0]), jnp.max(bounds[1]))

    @pl.when(worst <= ATT_SAFE_BOUND)
    def _():
        for s, sl in enumerate(subs):
            den = num = None
            for k_ref, v_ref in zip(k_refs, v_refs):
                p = jnp.exp2(_attn_scores(qs[s], k_ref, sl, 0, k_ref.shape[0]) - bounds[s])
                ps = p.sum(axis=-1, keepdims=True)
                pv = jnp.dot(p.astype(BF16), v_ref[...], preferred_element_type=F32)
                den = ps if den is None else den + ps
                num = pv if num is None else num + pv
            num_ref[s] = num / den

    @pl.when(worst > ATT_SAFE_BOUND)
    def _():
        for s, sl in enumerate(subs):
            m = den = num = None
            for k_ref, v_ref in zip(k_refs, v_refs):
                seg_len = k_ref.shape[0]
                for c0 in range(0, seg_len, AT_TK):
                    c1 = min(c0 + AT_TK, seg_len)
                    sc = _attn_scores(qs[s], k_ref, sl, c0, c1)
                    mc = sc.max(axis=-1, keepdims=True)
                    m_new = mc if m is None else jnp.maximum(m, mc)
                    p = jnp.exp2(sc - m_new)
                    ps = p.sum(axis=-1, keepdims=True)
                    pv = jnp.dot(p.astype(BF16), v_ref[c0:c1, :], preferred_element_type=F32)
                    if m is None:
                        den, num = ps, pv
                    else:
                        alpha = jnp.exp2(m - m_new)
                        den = alpha * den + ps
                        num = alpha * num + pv
                    m = m_new
            num_ref[s] = num / den

    d = num_ref[0] - lam * num_ref[1]
    ms = jnp.mean(d * d, axis=-1, keepdims=True)
    o_ref[...] = (d * lax.rsqrt(ms + EPS) * sub_ref[...] * post_scale).astype(BF16)


def _attention(qk, vu, lam, sub_norm, l, lam_init, ctx_only=False):
    kb = OFF_K // V_HEAD_DIM
    ctx_blk0 = R_LAT // CTX_LEN
    if ctx_only:
        tq = CTX_LEN
        grid = (BATCH, N_DIFF_HEADS, 1)
        q_map = lambda b, h, t: (ctx_blk0 + b, h)
        o_map = lambda b, h, t: (b, h)
        segs = [CTX_LEN]
    else:
        tq = AT_TQ
        grid = (BATCH, N_DIFF_HEADS, SEQ // tq)
        q_map = o_map = lambda b, h, t: (b * (SEQ // tq) + t, h)
        segs = [CTX_LEN, SEQ]
    in_specs = [pl.BlockSpec(memory_space=pltpu.SMEM),
                pl.BlockSpec((tq, V_HEAD_DIM), q_map)]
    args = [lam.reshape(1), qk]
    for arr, off in ((qk, kb), (vu, 0)):
        for seg_len in segs:
            if seg_len == CTX_LEN:
                imap = functools.partial(lambda b, h, t, o: (ctx_blk0 + b, o + h), o=off)
            else:
                imap = functools.partial(lambda b, h, t, o: (b, o + h), o=off)
            in_specs.append(pl.BlockSpec((seg_len, V_HEAD_DIM), imap))
            args.append(arr)
    in_specs.append(pl.BlockSpec((None, 1, V_HEAD_DIM), lambda b, h, t: (l, 0, 0)))
    args.append(sub_norm.reshape(DEPTH, 1, V_HEAD_DIM))
    return pl.pallas_call(
        functools.partial(_attn_body, n_seg=len(segs), post_scale=1.0 - lam_init),
        grid=grid,
        in_specs=in_specs,
        out_specs=pl.BlockSpec((tq, V_HEAD_DIM), o_map),
        out_shape=jax.ShapeDtypeStruct((R_CTX if ctx_only else R_LAT, ATT_W), BF16),
        scratch_shapes=[pltpu.SMEM((2,), F32), pltpu.VMEM((2, tq, V_HEAD_DIM), F32)],
        compiler_params=_cparams(3),
        name="attn_ctx" if ctx_only else "attn",
    )(*args)


def _dft_tables():
    n, n1, n2 = FFT_N, FFT_N1, FFT_N2
    k1 = np.arange(n1)[:, None]
    tabs_d, tabs_s, tabs_i = [], [], []
    for b in range(n2):
        cols = n2 * np.arange(n1)[None, :] + b
        g = np.exp(-2j * np.pi * ((k1 * cols) % n) / n)
        gd = g[:, :n1 // 2]
        tabs_d.append(np.block([[gd.real, -gd.imag], [gd.imag, gd.real]]))
        tabs_s.append(np.concatenate([g.real, g.imag], axis=0))
        gi = np.conj(gd).T / n1
        tabs_i.append(np.block([[gi.real, -gi.imag], [gi.imag, gi.real]]))
    kk = np.arange(n2)
    f2 = np.exp(-2j * np.pi * ((kk[:, None] * kk[None, :]) % n2) / n2)
    g2 = np.block([[f2.real, -f2.imag], [f2.imag, f2.real]])
    f2i = np.conj(f2) / n2
    g2i = np.block([[f2i.real, -f2i.imag], [f2i.imag, f2i.real]])
    as_bf = lambda a: jnp.asarray(np.asarray(a, np.float32)).astype(BF16)
    return (as_bf(np.stack(tabs_d)), as_bf(np.stack(tabs_s)), as_bf(np.stack(tabs_i)),
            as_bf(g2), as_bf(g2i))


FFT_UNROLL = 32


def _pack_pair(a, b):
    ab = lax.bitcast_convert_type(a.astype(BF16).astype(F32), jnp.uint32)
    bb = lax.bitcast_convert_type(b.astype(BF16).astype(F32), jnp.uint32)
    return lax.bitcast_convert_type(ab | (bb >> 16), F32)


def _unpack_pair(w):
    bits = lax.bitcast_convert_type(w, jnp.uint32)
    hi = lax.bitcast_convert_type(bits & jnp.uint32(0xFFFF0000), F32)
    lo = lax.bitcast_convert_type(bits << 16, F32)
    return hi, lo


def _rows_of_group(ref3, g):
    return jnp.swapaxes(ref3[:, pl.ds(pl.multiple_of(g * 8, 8), 8), :], 0, 1)


def _spectrum_body(hf_ref, hr_ref, wf_ref, wb_ref, dl_ref, g1_ref, g2_ref, or_ref, oi_ref,
                   k_ref, a_ref):
    n1, n2 = FFT_N1, FFT_N2
    delta = dl_ref[...]
    step = 1.0 / (SEQ - 1)
    rows = 512
    row_iota = lax.broadcasted_iota(jnp.int32, (rows, 1), 0)

    def taps(c, carry):
        r0 = pl.multiple_of(c * rows, rows)
        pos = (row_iota + r0).astype(F32)
        top = jnp.dot(hf_ref[pl.ds(r0, rows), :].astype(BF16), wf_ref[...].astype(BF16),
                      preferred_element_type=F32)
        bot = jnp.dot(hr_ref[pl.ds(r0, rows), :].astype(BF16), wb_ref[...].astype(BF16),
                      preferred_element_type=F32)
        blk = pl.multiple_of(c * (rows // n2), rows // n2)
        k_ref[pl.ds(blk, rows // n2)] = (top * jnp.exp(-(pos * step) * delta)).reshape(
            rows // n2, n2, HY_TC)
        k_ref[pl.ds(n1 // 2 + blk, rows // n2)] = (
            bot * jnp.exp(-((SEQ - pos) * step) * delta)).reshape(rows // n2, n2, HY_TC)
        return carry

    lax.fori_loop(0, SEQ // rows, taps, 0)

    def stage1(g, carry):
        xs = _rows_of_group(k_ref, g)
        for j in range(8):
            a = jnp.dot(g1_ref[g * 8 + j], xs[j].astype(BF16), preferred_element_type=F32)
            a_ref[g * 8 + j] = _pack_pair(a[:n1], a[n1:])
        return carry

    lax.fori_loop(0, n2 // 8, stage1, 0, unroll=FFT_UNROLL // 8)

    def stage2(g, carry):
        xs = _rows_of_group(a_ref, g)
        for j in range(8):
            a = jnp.concatenate(_unpack_pair(xs[j]), axis=0).astype(BF16)
            x = jnp.dot(g2_ref[...], a, preferred_element_type=F32)
            row = pl.multiple_of((g * 8 + j) * n2, n2)
            or_ref[pl.ds(row, n2), :] = x[:n2].astype(BF16)
            oi_ref[pl.ds(row, n2), :] = x[n2:].astype(BF16)
        return carry

    lax.fori_loop(0, n1 // 8, stage2, 0, unroll=FFT_UNROLL // 8)


def _const_spec(shape, n_grid):
    nd = len(shape)
    imap = (lambda a, b: (0,) * nd) if n_grid == 2 else (lambda a: (0,) * nd)
    return pl.BlockSpec(shape, imap, pipeline_mode=pl.Buffered(1))


def _spectrum(hid_f, hid_r, w4p, deltas, l, g1s, g2):
    tc = HY_TC
    nct = HY_W // tc
    hspec = pl.BlockSpec((SEQ, LANES), lambda f, ct: (0, 0))
    wspec = lambda d: pl.BlockSpec(
        (None, LANES, tc), functools.partial(lambda f, ct, d: (l, 0, (2 * f + d) * nct + ct), d=d))
    ospec = pl.BlockSpec((None, FFT_N, tc), lambda f, ct: (f, 0, ct))
    return pl.pallas_call(
        _spectrum_body,
        grid=(HY_N_FILT, nct),
        in_specs=[hspec, hspec, wspec(0), wspec(1),
                  pl.BlockSpec((1, tc), lambda f, ct: (0, ct)),
                  _const_spec((FFT_N2, 2 * FFT_N1, FFT_N1), 2),
                  _const_spec((2 * FFT_N2, 2 * FFT_N2), 2)],
        out_specs=[ospec, ospec],
        out_shape=[jax.ShapeDtypeStruct((HY_N_FILT, FFT_N, HY_W), BF16)] * 2,
        scratch_shapes=[pltpu.VMEM((FFT_N1, FFT_N2, tc), F32), pltpu.VMEM((FFT_N2, FFT_N1, tc), F32)],
        compiler_params=_cparams(2),
        name="hy_spectrum",
    )(hid_f, hid_r, w4p, w4p, deltas, g1s, g2)


def _conv3(u, w_ref, b_ref):
    n = u.shape[0]
    row = lax.broadcasted_iota(jnp.int32, u.shape, 0)
    prev = jnp.where(row == 0, 0.0, pltpu.roll(u, 1, 0))
    nxt = jnp.where(row == n - 1, 0.0, pltpu.roll(u, n - 1, 0))
    return prev * w_ref[0:1, :] + u * w_ref[1:2, :] + nxt * w_ref[2:3, :] + b_ref[...]


def _hyconv_body(s0_ref, s1_ref, m0_ref, m1_ref, cws_ref, cbs_ref, cwm_ref, cbm_ref, bias_ref,
                 kr_ref, ki_ref, g1_ref, g1i_ref, g2_ref, g2i_ref, o_ref,
                 vr_ref, vi_ref, vp_ref, a_ref, b_ref, *, conv_signal):
    n1, n2 = FFT_N1, FFT_N2
    h1 = n1 // 2
    for s_ref, v_ref in ((s0_ref, vr_ref), (s1_ref, vi_ref)):
        u = s_ref[...].astype(F32)
        v_ref[...] = _conv3(u, cws_ref, cbs_ref) if conv_signal else u
    vp_ref[...] = _pack_pair(vr_ref[...], vi_ref[...]).reshape(h1, n2, HY_TC)

    def stage1(g, carry):
        xs = _rows_of_group(vp_ref, g)
        for j in range(8):
            x = jnp.concatenate(_unpack_pair(xs[j]), axis=0).astype(BF16)
            a = jnp.dot(g1_ref[g * 8 + j], x, preferred_element_type=F32)
            a_ref[g * 8 + j] = _pack_pair(a[:n1], a[n1:])
        return carry

    lax.fori_loop(0, n2 // 8, stage1, 0, unroll=FFT_UNROLL // 8)

    def stage2(g, carry):
        xs = _rows_of_group(a_ref, g)
        for j in range(8):
            k1 = g * 8 + j
            a = jnp.concatenate(_unpack_pair(xs[j]), axis=0).astype(BF16)
            x = jnp.dot(g2_ref[...], a, preferred_element_type=F32)
            row = pl.multiple_of(k1 * n2, n2)
            fr = kr_ref[pl.ds(row, n2), :].astype(F32)
            fi = ki_ref[pl.ds(row, n2), :].astype(F32)
            xr, xi = x[:n2], x[n2:]
            y = jnp.concatenate([xr * fr - xi * fi, xr * fi + xi * fr], axis=0).astype(BF16)
            bq = jnp.dot(g2i_ref[...], y, preferred_element_type=F32)
            b_ref[k1] = _pack_pair(bq[:n2], bq[n2:])
        return carry

    lax.fori_loop(0, n1 // 8, stage2, 0, unroll=FFT_UNROLL // 8)

    def stage1_inv(g, carry):
        xs = _rows_of_group(b_ref, g)
        for j in range(8):
            bq = jnp.concatenate(_unpack_pair(xs[j]), axis=0).astype(BF16)
            y = jnp.dot(g1i_ref[g * 8 + j], bq, preferred_element_type=F32)
            a_ref[g * 8 + j, pl.ds(0, h1), :] = _pack_pair(y[:h1], y[h1:])
        return carry

    lax.fori_loop(0, n2 // 8, stage1_inv, 0, unroll=FFT_UNROLL // 8)

    bias = bias_ref[...]

    def unpermute(g, carry):
        xs = _rows_of_group(a_ref, g)
        for j in range(8):
            row = pl.multiple_of((g * 8 + j) * n2, n2)
            yr, yi = _unpack_pair(xs[j])
            vr_ref[pl.ds(row, n2), :] = yr + bias * vr_ref[pl.ds(row, n2), :]
            vi_ref[pl.ds(row, n2), :] = yi + bias * vi_ref[pl.ds(row, n2), :]
        return carry

    lax.fori_loop(0, h1 // 8, unpermute, 0, unroll=FFT_UNROLL // 8)

    for half, (m_ref, v_ref) in enumerate(((m0_ref, vr_ref), (m1_ref, vi_ref))):
        mult = _conv3(m_ref[...].astype(F32), cwm_ref, cbm_ref)
        o_ref[half * SEQ:(half + 1) * SEQ, :] = (mult * v_ref[...]).astype(o_ref.dtype)


def _hy_param_specs(conv_w, conv_b, bias, l, filt, sig_ch, mul_ch, tc):
    cw3 = conv_w.reshape(DEPTH, 3, 3 * HY_W)
    cb3 = conv_b.reshape(DEPTH, 1, 3 * HY_W)
    chan = lambda ch0, rows: pl.BlockSpec(
        (None, rows, tc), functools.partial(lambda ct, pr, c0: (l, 0, c0 + ct), c0=ch0))
    specs = [chan(sig_ch, 3), chan(sig_ch, 1), chan(mul_ch, 3), chan(mul_ch, 1),
             pl.BlockSpec((None, None, 1, tc), lambda ct, pr: (l, filt, 0, ct))]
    args = [cw3, cb3, cw3, cb3, bias.reshape(DEPTH, HY_N_FILT, 1, HY_W)]
    return specs, args


def _hyconv(sig, sig_cb, mul, mul_cb, conv_w, conv_b, bias, l, filt, sig_ch, mul_ch,
            kr, ki, tabs, conv_signal):
    g1d, g1i, g2, g2i = tabs
    tc = HY_TC
    blk = lambda arr_cb, odd: pl.BlockSpec(
        (SEQ, tc), functools.partial(lambda ct, pr, cb, o: (2 * pr + o, cb + ct), cb=arr_cb, o=odd))
    pspecs, pargs = _hy_param_specs(conv_w, conv_b, bias, l, filt, sig_ch, mul_ch, tc)
    kspec = pl.BlockSpec((None, FFT_N, tc), lambda ct, pr: (filt, 0, ct))
    in_specs = [blk(sig_cb, 0), blk(sig_cb, 1), blk(mul_cb, 0), blk(mul_cb, 1)] + pspecs + [
        kspec, kspec,
        _const_spec((FFT_N2, 2 * FFT_N1, FFT_N1), 2),
        _const_spec((FFT_N2, FFT_N1, 2 * FFT_N1), 2),
        _const_spec((2 * FFT_N2, 2 * FFT_N2), 2),
        _const_spec((2 * FFT_N2, 2 * FFT_N2), 2),
    ]
    return pl.pallas_call(
        functools.partial(_hyconv_body, conv_signal=conv_signal),
        grid=(HY_W // tc, BATCH // 2),
        in_specs=in_specs,
        out_specs=pl.BlockSpec((2 * SEQ, tc), lambda ct, pr: (pr, ct)),
        out_shape=jax.ShapeDtypeStruct((R_LAT, HY_W), BF16),
        scratch_shapes=[pltpu.VMEM((SEQ, tc), F32)] * 2 + [pltpu.VMEM((FFT_N1 // 2, FFT_N2, tc), F32),
                        pltpu.VMEM((FFT_N2, FFT_N1, tc), F32), pltpu.VMEM((FFT_N1, FFT_N2, tc), F32)],
        compiler_params=_cparams(2),
        name="hyconv",
    )(sig, sig, mul, mul, *pargs, kr, ki, g1d, g1i, g2, g2i)


def _ctx_dft_tables():
    n = 2 * CTX_LEN
    kk = np.arange(n)
    f = np.exp(-2j * np.pi * ((kk[:, None] * kk[None, :]) % n) / n)
    fd = f[:, :CTX_LEN]
    fwd = np.block([[fd.real, -fd.imag], [fd.imag, fd.real]])
    spec = np.concatenate([f.real, f.imag], axis=0)
    fi = np.conj(f)[:CTX_LEN, :] / n
    inv = np.block([[fi.real, -fi.imag], [fi.imag, fi.real]])
    as_bf = lambda a: jnp.asarray(np.asarray(a, np.float32)).astype(BF16)
    return as_bf(fwd), as_bf(spec), as_bf(inv)


def _hyconv_ctx_body(s0_ref, s1_ref, m0_ref, m1_ref, cws_ref, cbs_ref, cwm_ref, cbm_ref, bias_ref,
                     k_ref, fwd_ref, spec_ref, inv_ref, o_ref, *, conv_signal):
    n = 2 * CTX_LEN
    vs = []
    for s_ref in (s0_ref, s1_ref):
        u = s_ref[...].astype(F32)
        vs.append(_conv3(u, cws_ref, cbs_ref) if conv_signal else u)
    x = jnp.dot(fwd_ref[...], jnp.concatenate(vs, axis=0).astype(BF16), preferred_element_type=F32)
    kf = jnp.dot(spec_ref[...], k_ref[...].astype(BF16), preferred_element_type=F32)
    kf = kf.astype(BF16).astype(F32)
    xr, xi, fr, fi = x[:n], x[n:], kf[:n], kf[n:]
    y = jnp.concatenate([xr * fr - xi * fi, xr * fi + xi * fr], axis=0).astype(BF16)
    conv = jnp.dot(inv_ref[...], y, preferred_element_type=F32)
    bias = bias_ref[...]
    for half, m_ref in enumerate((m0_ref, m1_ref)):
        sl = slice(half * CTX_LEN, (half + 1) * CTX_LEN)
        mult = _conv3(m_ref[...].astype(F32), cwm_ref, cbm_ref)
        o_ref[sl, :] = (mult * (conv[sl] + bias * vs[half])).astype(o_ref.dtype)


def _hyconv_ctx(sig, sig_rb, sig_cb, mul, mul_rb, mul_cb, conv_w, conv_b, bias, l, filt,
                sig_ch, mul_ch, kctx, tabs, conv_signal):
    fwd, spec, inv = tabs
    tc = HY_TC
    blk = lambda rb, arr_cb, odd: pl.BlockSpec(
        (CTX_LEN, tc),
        functools.partial(lambda ct, pr, rb, cb, o: (rb + 2 * pr + o, cb + ct), rb=rb, cb=arr_cb, o=odd))
    pspecs, pargs = _hy_param_specs(conv_w, conv_b, bias, l, filt, sig_ch, mul_ch, tc)
    n = 2 * CTX_LEN
    in_specs = [blk(sig_rb, sig_cb, 0), blk(sig_rb, sig_cb, 1),
                blk(mul_rb, mul_cb, 0), blk(mul_rb, mul_cb, 1)] + pspecs + [
        pl.BlockSpec((None, n, tc), lambda ct, pr: (filt, 0, ct)),
        _const_spec((2 * n, n), 2),
        _const_spec((2 * n, n), 2),
        _const_spec((n, 2 * n), 2),
    ]
    return pl.pallas_call(
        functools.partial(_hyconv_ctx_body, conv_signal=conv_signal),
        grid=(HY_W // tc, BATCH // 2),
        in_specs=in_specs,
        out_specs=pl.BlockSpec((2 * CTX_LEN, tc), lambda ct, pr: (pr, ct)),
        out_shape=jax.ShapeDtypeStruct((R_CTX, HY_W), BF16),
        compiler_params=_cparams(2),
        name="hyconv_ctx",
    )(sig, sig, mul, mul, *pargs, kctx, fwd, spec, inv)


def _rope_tables():
    rows = SEQ // GRID_W
    row = jnp.repeat(jnp.arange(rows), GRID_W).astype(F32)
    col = jnp.tile(jnp.arange(GRID_W), rows).astype(F32)
    inv = ROPE_BASE ** (-jnp.arange(ROT_FREQS, dtype=F32) / ROT_FREQS)
    ar, ac = row[:, None] * inv, col[:, None] * inv
    cos = jnp.concatenate([jnp.cos(ar), jnp.cos(ac), jnp.cos(ar), jnp.cos(ac)], axis=1)
    sin = jnp.concatenate([-jnp.sin(ar), -jnp.sin(ac), jnp.sin(ar), jnp.sin(ac)], axis=1)
    cos = jnp.concatenate([cos, jnp.ones((TM, HEAD_DIM), F32)], axis=0)
    sin = jnp.concatenate([sin, jnp.zeros((TM, HEAD_DIM), F32)], axis=0)
    return cos, sin


def _hyena_positions(L, reverse):
    idx = jnp.arange(L, dtype=F32)
    return (L - idx) if reverse else idx


def _hyena_hidden(L, w1, b1, w2, b2, w3, b3, freq, reverse=False):
    pos = _hyena_positions(L, reverse)[:, None]
    t = pos / (L - 1)
    w = 2.0 * math.pi * pos / L
    f = jnp.linspace(1e-4, HY_BANDS - 1, HY_BANDS, dtype=F32)
    z = jnp.concatenate([t, jnp.cos(w * f), -jnp.sin(w * f)], axis=-1)
    dot = functools.partial(jnp.dot, precision=HIGHEST)
    hid = jnp.sin(freq * (dot(z, w1) + b1))
    hid = jnp.sin(freq * (dot(hid, w2) + b2))
    hid = jnp.sin(freq * (dot(hid, w3) + b3))
    if reverse:
        hid = jnp.where(jnp.arange(L)[:, None] > 0, hid, 0.0)
    return hid


def _hyena_deltas():
    return jnp.abs(jnp.linspace(math.log(HY_TARGET) / HY_SLOW_PCT,
                                math.log(HY_TARGET) / HY_FAST_PCT, HY_W, dtype=F32))


def _hyena_filters_ctx(hid, hid_rev, w4):
    L = hid.shape[0]
    w4 = w4.reshape(HY_FH, HY_N_FILT, 2, HY_W)
    halves = []
    for d, feats in enumerate((hid, hid_rev)):
        t = _hyena_positions(L, bool(d))[:, None] / (L - 1)
        taps = jnp.einsum('lh,hfc->flc', feats, w4[:, :, d], precision=HIGHEST)
        halves.append(taps * jnp.exp(-t * _hyena_deltas())[None])
    return jnp.concatenate(halves, axis=1)


def kernel(x, c, ctx, c_ctx, w_ada, b_ada, norm1, norm2, w_in, b_gate, q_norm, k_norm, lam_q1, lam_k1, lam_q2, lam_k2, sub_norm, hy_conv_w, hy_conv_b, hy_w1, hy_b1, hy_w2, hy_b2, hy_w3, hy_b3, hy_w4, hy_freq, hy_bias, w_br_a, w_br_b, w_out, ffn_w1, ffn_w3, ffn_w2, router, moe_w1, moe_w3, moe_w2):
    xs = jnp.concatenate([x.reshape(R_LAT, D_MODEL), ctx.reshape(R_CTX, D_MODEL)], axis=0)
    cs = jnp.concatenate([c, c_ctx[None], jnp.zeros((8 - BATCH - 1, D_MODEL), F32)], axis=0)
    mods = _adaln(cs, w_ada, b_ada)
    cos_t, sin_t = _rope_tables()
    g1d, g1s, g1i, g2, g2i = _dft_tables()
    ctx_tabs = _ctx_dft_tables()
    gains = jnp.zeros((DEPTH, 8, HEAD_DIM), F32)
    gains = gains.at[:, 0].set(_pair_major(q_norm) * (math.log2(math.e) / math.sqrt(HEAD_DIM)))
    gains = gains.at[:, 1].set(_pair_major(k_norm))
    router_pad = jnp.zeros((router.shape[0], D_MODEL, LANES), F32).at[:, :, :N_EXPERTS].set(router)
    hy_cb = ATT_W // HY_TC
    ch = HY_W // HY_TC
    w4p = jnp.pad(hy_w4, ((0, 0), (0, LANES - HY_FH), (0, 0)))
    deltas = _hyena_deltas()[None]

    for l in range(DEPTH):
        lam_init = 0.8 - 0.6 * math.exp(-0.3 * l)
        lam = (jnp.exp(jnp.sum(lam_q1[l] * lam_k1[l])) - jnp.exp(jnp.sum(lam_q2[l] * lam_k2[l]))
               + lam_init)
        h = _normmod(xs, norm1, mods, l, 0)
        qk, vu, gates = _inproj(h, w_in, l, gains, cos_t, sin_t, b_gate)

        ya = (_attention(qk, vu, lam, sub_norm, l, lam_init),
              _attention(qk, vu, lam, sub_norm, l, lam_init, ctx_only=True))

        hp = (hy_w1[l], hy_b1[l], hy_w2[l], hy_b2[l], hy_w3[l], hy_b3[l], hy_freq[l])
        lane_pad = ((0, 0), (0, LANES - HY_FH))
        hid = jnp.pad(_hyena_hidden(SEQ, *hp), lane_pad)
        hid_rev = jnp.pad(_hyena_hidden(SEQ, *hp, reverse=True), lane_pad)
        kr, ki = _spectrum(hid, hid_rev, w4p, deltas, l, g1s, g2)
        kctx = _hyena_filters_ctx(_hyena_hidden(CTX_LEN, *hp),
                                  _hyena_hidden(CTX_LEN, *hp, reverse=True), hy_w4[l])
        common = (hy_conv_w, hy_conv_b, hy_bias, l)
        tabs = (g1d, g1i, g2, g2i)
        crb = R_LAT // CTX_LEN
        z = _hyconv(vu, hy_cb, vu, hy_cb + ch, *common, 0, 0, ch, kr, ki, tabs, True)
        zc = _hyconv_ctx(vu, crb, hy_cb, vu, crb, hy_cb + ch, *common, 0, 0, ch, kctx, ctx_tabs, True)
        yb = (_hyconv(z, 0, vu, hy_cb + 2 * ch, *common, 1, 0, 2 * ch, kr, ki, tabs, False),
              _hyconv_ctx(zc, 0, 0, vu, crb, hy_cb + 2 * ch, *common, 1, 0, 2 * ch, kctx, ctx_tabs, False))

        m = _merge(ya, yb, gates, w_br_a, w_br_b, l)
        xs = _resid_proj(m, w_out, (l,), xs, mods, l, 2, name="out_proj")

        i = l // 2
        if l % 2 == 0:
            h2 = _normmod(xs, norm2, mods, l, 3)
            g = _swiglu_up(h2, ffn_w1, ffn_w3, (i,))
            xs = _resid_proj(g, ffn_w2, (i,), xs, mods, l, 5, tm=FFD_TM, name="ffn_down")
        else:
            h32, logits = _normmod(xs, norm2, mods, l, 3, router=router_pad[i])
            xs = _moe(h32, logits, xs, mods, l, i, moe_w1, moe_w3, moe_w2)
    return xs[:R_LAT].reshape(BATCH, SEQ, D_MODEL)
```
